```python
import math
import jax, jax.numpy as jnp
from jax import lax
import numpy as np

D_MODEL = 2048
BATCH = 4
SEQ = 2048
DEPTH = 2
DEC_BATCH = 8
DEC_SEQ = 4096
PAST_LEN = 128

RW_HEADS = 12
RW_HEAD_DIM = 64
RW_WIDTH = RW_HEADS * RW_HEAD_DIM
DECAY_LORA = 64
ICLR_LORA = 64
GATE_LORA = 128
N_DIR = 2
GN_EPS = 64e-5
MLA_HEADS = 8
MLA_NOPE = 64
MLA_ROPE = 32
MLA_V = 64
MLA_Q_LORA = 512
MLA_KV_LORA = 256
MLA_WIDTH = MLA_HEADS * MLA_V
DF_HEADS = 6
DF_HEAD_DIM = 64
DF_WIDTH = DF_HEADS * 2 * DF_HEAD_DIM
DF_EPS = 1e-5
N_BRANCH = 3
N_EXPERTS = 16
N_GROUPS = 4
EXPERTS_PER_GROUP = N_EXPERTS // N_GROUPS
TOP_K = 2
D_EXPERT = 1024
ROPE_THETA = 10000.0
Q_BLOCK = 128
LN_EPS = 1e-5
RMS_EPS = 1e-6
ALPHA = (2 * DEPTH) ** 0.25
BETA = (8 * DEPTH) ** -0.25

RW_IN = 3 * RW_WIDTH + N_DIR * DECAY_LORA + N_DIR * ICLR_LORA + GATE_LORA
MLA_IN = MLA_Q_LORA + MLA_KV_LORA + MLA_ROPE
DF_IN = 3 * DF_WIDTH
GATE_IN = N_BRANCH * D_MODEL
IN_WIDTH = RW_IN + MLA_IN + DF_IN + GATE_IN
SPLIT_POINTS = [RW_IN, RW_IN + MLA_IN, RW_IN + MLA_IN + DF_IN]
RW_SPLIT_POINTS = [RW_WIDTH, 2 * RW_WIDTH, 3 * RW_WIDTH,
                   3 * RW_WIDTH + N_DIR * DECAY_LORA,
                   3 * RW_WIDTH + N_DIR * DECAY_LORA + N_DIR * ICLR_LORA]

kernel_name = 'hybrid_rwkv7_mla_diffattn_moe_encoder'

F32 = jnp.float32


def _layer_norm(x, g, b):
    xf = x.astype(F32)
    mu = jnp.mean(xf, -1, keepdims=True)
    var = jnp.mean(jnp.square(xf - mu), -1, keepdims=True)
    return ((xf - mu) * lax.rsqrt(var + LN_EPS) * g + b).astype(x.dtype)


def _rms_norm(x, g, eps):
    xf = x.astype(F32)
    return (xf * lax.rsqrt(jnp.mean(xf * xf, -1, keepdims=True) + eps) * g).astype(x.dtype)


def _rope(x):
    S, d = x.shape[1], x.shape[-1]
    half = d // 2
    inv_freq = jnp.power(ROPE_THETA, -jnp.arange(half, dtype=F32) * (2.0 / d))
    ang = jnp.arange(S, dtype=F32)[:, None] * inv_freq[None, :]
    shape = (1, S) + (1,) * (x.ndim - 3) + (half,)
    cos = jnp.cos(ang).reshape(shape)
    sin = jnp.sin(ang).reshape(shape)
    xf = x.astype(F32)
    x1, x2 = xf[..., :half], xf[..., half:]
    return jnp.concatenate([x1 * cos - x2 * sin, x2 * cos + x1 * sin], -1).astype(x.dtype)


def _sweep_query_blocks(block_fn, q):
    B, S, H, dq = q.shape
    nb = S // Q_BLOCK
    qb = jnp.moveaxis(q.reshape(B, nb, Q_BLOCK, H, dq), 1, 0)
    out = lax.map(block_fn, qb)
    return jnp.moveaxis(out, 0, 1).reshape(B, S, H, out.shape[-1])


def _centred_shift(z, mu_prev, mu_next):
    prev = jnp.pad(z, ((0, 0), (1, 0), (0, 0)))[:, :-1]
    nxt = jnp.pad(z, ((0, 0), (0, 1), (0, 0)))[:, 1:]
    return z + mu_prev * (prev - z) + mu_next * (nxt - z)


def _wkv7_scan(r, w, k, v, kk, a, reverse):
    B, S, H, N = r.shape

    def step(state, inp):
        r_t, w_t, k_t, v_t, kk_t, a_t = inp
        sa = jnp.einsum('bhvk,bhk->bhv', state, -kk_t)
        state = (state * w_t[:, :, None, :]
                 + sa[..., None] * (kk_t * a_t)[:, :, None, :]
                 + v_t[..., None] * k_t[:, :, None, :])
        return state, jnp.einsum('bhvk,bhk->bhv', state, r_t)

    xs = tuple(jnp.moveaxis(t, 1, 0) for t in (r, w, k, v, kk, a))
    s0 = jnp.zeros((B, H, N, N), F32)
    _, ys = lax.scan(step, s0, xs, reverse=reverse)
    return jnp.moveaxis(ys, 0, 1)


def _rwkv7_mixer(z, mu_prev, mu_next, w0, w2, a0, a2, g2, k_k, k_a, r_k, lnx_g, lnx_b):
    dtype = z.dtype
    z = _centred_shift(z.astype(F32), mu_prev, mu_next)
    B, S, _ = z.shape
    H, N, C = RW_HEADS, RW_HEAD_DIM, RW_WIDTH
    r, k, v, wl, al, gl = jnp.split(z, RW_SPLIT_POINTS, axis=-1)
    wl = wl.reshape(B, S, N_DIR, DECAY_LORA)
    al = al.reshape(B, S, N_DIR, ICLR_LORA)
    w_raw = w0 + jnp.einsum('bsdr,drc->bsdc', jnp.tanh(wl), w2)
    decay = jnp.exp(-jnp.exp(-jax.nn.softplus(-w_raw) - 0.5)).reshape(B, S, N_DIR, H, N)
    a = jax.nn.sigmoid(a0 + jnp.einsum('bsdr,drc->bsdc', al, a2)).reshape(B, S, N_DIR, H, N)
    g = jnp.einsum('bsr,rc->bsc', jax.nn.sigmoid(gl), g2)
    r = r.reshape(B, S, H, N)
    k = k.reshape(B, S, H, N)
    v = v.reshape(B, S, H, N)
    kk = k * k_k.reshape(H, N)
    kk = kk * lax.rsqrt(jnp.sum(kk * kk, -1, keepdims=True) + 1e-12)
    k_dir = k[:, :, None] * (1.0 + (a - 1.0) * k_a.reshape(H, N))
    y = (_wkv7_scan(r, decay[:, :, 0], k_dir[:, :, 0], v, kk, a[:, :, 0], reverse=False)
         + _wkv7_scan(r, decay[:, :, 1], k_dir[:, :, 1], v, kk, a[:, :, 1], reverse=True))
    mu = jnp.mean(y, -1, keepdims=True)
    var = jnp.mean(jnp.square(y - mu), -1, keepdims=True)
    y = ((y - mu) * lax.rsqrt(var + GN_EPS)).reshape(B, S, C) * lnx_g + lnx_b
    bonus = jnp.sum(jnp.sum(r[:, :, None] * k_dir * r_k, -1, keepdims=True) * v[:, :, None], axis=2)
    return ((y + bonus.reshape(B, S, C)) * g).astype(dtype)


def _mla_mixer(z, q_norm_g, kv_norm_g, w_uq, w_ukv):
    B, S, _ = z.shape
    H = MLA_HEADS
    c_q, c_kv, k_rope = jnp.split(z, [MLA_Q_LORA, MLA_Q_LORA + MLA_KV_LORA], axis=-1)
    c_q = _rms_norm(c_q, q_norm_g, RMS_EPS)
    c_kv = _rms_norm(c_kv, kv_norm_g, RMS_EPS)
    q = jnp.einsum('bsr,re->bse', c_q, w_uq).reshape(B, S, H, MLA_NOPE + MLA_ROPE)
    q = jnp.concatenate([q[..., :MLA_NOPE], _rope(q[..., MLA_NOPE:])], -1)
    kv = jnp.einsum('bsr,re->bse', c_kv, w_ukv).reshape(B, S, H, MLA_NOPE + MLA_V)
    k_nope, v = kv[..., :MLA_NOPE], kv[..., MLA_NOPE:]
    k_rope = jnp.broadcast_to(_rope(k_rope)[:, :, None, :], (B, S, H, MLA_ROPE))
    k = jnp.concatenate([k_nope, k_rope], -1)
    scale = (MLA_NOPE + MLA_ROPE) ** -0.5

    def block(qb):
        s = jnp.einsum('bqhd,bkhd->bhqk', qb, k).astype(F32) * scale
        p = jax.nn.softmax(s, axis=-1).astype(v.dtype)
        return jnp.einsum('bhqk,bkhd->bqhd', p, v)

    return _sweep_query_blocks(block, q).reshape(B, S, MLA_WIDTH)


def _diff_mixer(z, lq1, lk1, lq2, lk2, subln_g, lambda_init):
    B, S, _ = z.shape
    H, dh = DF_HEADS, DF_HEAD_DIM
    q, k, v = jnp.split(z, [DF_WIDTH, 2 * DF_WIDTH], axis=-1)
    q = _rope(q.reshape(B, S, H, 2, dh)).reshape(B, S, H, 2 * dh)
    k = _rope(k.reshape(B, S, H, 2, dh))
    k1, k2 = k[..., 0, :], k[..., 1, :]
    v = v.reshape(B, S, H, 2 * dh)
    lam = (jnp.exp(jnp.sum(lq1.astype(F32) * lk1.astype(F32)))
           - jnp.exp(jnp.sum(lq2.astype(F32) * lk2.astype(F32))) + lambda_init)
    scale = dh ** -0.5

    def block(qb):
        s1 = jnp.einsum('bqhd,bkhd->bhqk', qb[..., :dh], k1).astype(F32) * scale
        s2 = jnp.einsum('bqhd,bkhd->bhqk', qb[..., dh:], k2).astype(F32) * scale
        p = jax.nn.softmax(s1, axis=-1) - lam * jax.nn.softmax(s2, axis=-1)
        return jnp.einsum('bhqk,bkhd->bqhd', p.astype(v.dtype), v)

    o = _sweep_query_blocks(block, q)
    o = _rms_norm(o, subln_g, DF_EPS) * (1.0 - lambda_init)
    return o.reshape(B, S, DF_WIDTH)


def _moe(x, router_w, router_bias, w_gate, w_up, w_down):
    B, S, D = x.shape
    xf = x.reshape(B * S, D)
    scores = jax.nn.sigmoid(jnp.einsum('nd,de->ne', xf, router_w).astype(F32))
    sel = (scores + router_bias.astype(F32)).reshape(-1, N_GROUPS, EXPERTS_PER_GROUP)
    group_score = jnp.sum(lax.top_k(sel, TOP_K)[0], -1)
    best = jnp.argmax(group_score, -1)
    in_group = best[:, None] == jnp.arange(N_GROUPS)[None, :]
    masked = jnp.where(in_group[..., None], sel, -jnp.inf).reshape(-1, N_EXPERTS)
    _, idx = lax.top_k(masked, TOP_K)
    wts = jnp.take_along_axis(scores, idx, -1)
    wts = wts / jnp.sum(wts, -1, keepdims=True)
    gates = jnp.einsum('nk,nke->ne', wts, jax.nn.one_hot(idx, N_EXPERTS, dtype=F32))
    y = jnp.zeros((B * S, D), F32)
    for e in range(N_EXPERTS):
        h = jax.nn.silu(xf @ w_gate[e]) * (xf @ w_up[e])
        y = y + gates[:, e:e + 1] * (h @ w_down[e]).astype(F32)
    return y.astype(x.dtype).reshape(B, S, D)


def _trunk(x, w_in, shift_prev, shift_next, rw_w0, rw_w2, rw_a0, rw_a2, rw_g2, rw_k_k, rw_k_a,
           rw_r_k, rw_lnx_g, rw_lnx_b, mla_q_norm, mla_kv_norm, mla_w_uq, mla_w_ukv,
           df_lq1, df_lk1, df_lq2, df_lk2, df_subln, w_up_rw, w_up_mla, w_up_df, w_o,
           ln1_g, ln1_b, ln2_g, ln2_b, router_w, router_bias, ex_w_gate, ex_w_up, ex_w_down):
    B, S, D = x.shape
    for l in range(DEPTH):
        lambda_init = 0.8 - 0.6 * math.exp(-0.3 * l)
        z = jnp.einsum('bsd,de->bse', x, w_in[l])
        z_rw, z_mla, z_df, z_gate = jnp.split(z, SPLIT_POINTS, axis=-1)
        o_rw = _rwkv7_mixer(z_rw, shift_prev[l], shift_next[l], rw_w0[l], rw_w2[l], rw_a0[l],
                            rw_a2[l], rw_g2[l], rw_k_k[l], rw_k_a[l], rw_r_k[l],
                            rw_lnx_g[l], rw_lnx_b[l])
        o_mla = _mla_mixer(z_mla, mla_q_norm[l], mla_kv_norm[l], mla_w_uq[l], mla_w_ukv[l])
        o_df = _diff_mixer(z_df, df_lq1[l], df_lk1[l], df_lq2[l], df_lk2[l], df_subln[l],
                           lambda_init)
        g = jax.nn.sigmoid(z_gate.astype(F32)).reshape(B, S, N_BRANCH, D)
        merged = (g[:, :, 0] * jnp.einsum('bsc,cd->bsd', o_rw, w_up_rw[l])
                  + g[:, :, 1] * jnp.einsum('bsc,cd->bsd', o_mla, w_up_mla[l])
                  + g[:, :, 2] * jnp.einsum('bsc,cd->bsd', o_df, w_up_df[l]))
        mix = jnp.einsum('bsd,de->bse', merged.astype(x.dtype), w_o[l])
        x = _layer_norm(ALPHA * x + mix, ln1_g[l], ln1_b[l])
        ffn = _moe(x, router_w, router_bias, ex_w_gate[l], ex_w_up[l], ex_w_down[l])
        x = _layer_norm(ALPHA * x + ffn, ln2_g[l], ln2_b[l])
    return x


def setup_inputs(seed: int = 0) -> dict:
    key = jax.random.key(seed)
    ks = jax.random.split(key, 40)

    def nrm(i, shape, scale):
        return scale * jax.random.normal(ks[i], shape, F32)

    def unif(i, shape, lo, hi):
        return jax.random.uniform(ks[i], shape, F32, lo, hi)

    L, D, C = DEPTH, D_MODEL, RW_WIDTH
    return {
        'x_prompt': nrm(0, (BATCH, SEQ, D), 1.0),
        'x_sample': nrm(1, (DEC_BATCH, DEC_SEQ, D), 1.0),
        'w_in': nrm(2, (L, D, IN_WIDTH), D ** -0.5),
        'shift_prev': unif(3, (L, RW_IN), 0.05, 0.5),
        'shift_next': unif(4, (L, RW_IN), 0.05, 0.5),
        'rw_w0': unif(5, (L, N_DIR, C), -6.0, -1.0),
        'rw_w2': nrm(6, (L, N_DIR, DECAY_LORA, C), 0.1 * DECAY_LORA ** -0.5),
        'rw_a0': nrm(7, (L, N_DIR, C), 0.1),
        'rw_a2': nrm(8, (L, N_DIR, ICLR_LORA, C), 0.1 * ICLR_LORA ** -0.5),
        'rw_g2': nrm(9, (L, GATE_LORA, C), GATE_LORA ** -0.5),
        'rw_k_k': 0.85 + nrm(10, (L, C), 0.05),
        'rw_k_a': 1.0 + nrm(11, (L, C), 0.05),
        'rw_r_k': nrm(12, (L, RW_HEADS, RW_HEAD_DIM), 0.1),
        'rw_lnx_g': 1.0 + nrm(13, (L, C), 0.05),
        'rw_lnx_b': nrm(14, (L, C), 0.01),
        'mla_q_norm': 1.0 + nrm(15, (L, MLA_Q_LORA), 0.05),
        'mla_kv_norm': 1.0 + nrm(16, (L, MLA_KV_LORA), 0.05),
        'mla_w_uq': nrm(17, (L, MLA_Q_LORA, MLA_HEADS * (MLA_NOPE + MLA_ROPE)), MLA_Q_LORA ** -0.5),
        'mla_w_ukv': nrm(18, (L, MLA_KV_LORA, MLA_HEADS * (MLA_NOPE + MLA_V)), MLA_KV_LORA ** -0.5),
        'df_lq1': nrm(19, (L, DF_HEAD_DIM), 0.1),
        'df_lk1': nrm(20, (L, DF_HEAD_DIM), 0.1),
        'df_lq2': nrm(21, (L, DF_HEAD_DIM), 0.1),
        'df_lk2': nrm(22, (L, DF_HEAD_DIM), 0.1),
        'df_subln': 1.0 + nrm(23, (L, 2 * DF_HEAD_DIM), 0.05),
        'w_up_rw': nrm(24, (L, RW_WIDTH, D), RW_WIDTH ** -0.5),
        'w_up_mla': nrm(25, (L, MLA_WIDTH, D), MLA_WIDTH ** -0.5),
        'w_up_df': nrm(26, (L, DF_WIDTH, D), DF_WIDTH ** -0.5),
        'w_o': nrm(27, (L, D, D), BETA * D ** -0.5),
        'ln1_g': 1.0 + nrm(28, (L, D), 0.05),
        'ln1_b': nrm(29, (L, D), 0.01),
        'ln2_g': 1.0 + nrm(30, (L, D), 0.05),
        'ln2_b': nrm(31, (L, D), 0.01),
        'router_w': nrm(32, (D, N_EXPERTS), D ** -0.5),
        'router_bias': nrm(33, (N_EXPERTS,), 0.01),
        'ex_w_gate': nrm(34, (L, N_EXPERTS, D, D_EXPERT), D ** -0.5),
        'ex_w_up': nrm(35, (L, N_EXPERTS, D, D_EXPERT), D ** -0.5),
        'ex_w_down': nrm(36, (L, N_EXPERTS, D_EXPERT, D), BETA * D_EXPERT ** -0.5),
    }


def reference(x_prompt, x_sample, w_in, shift_prev, shift_next, rw_w0, rw_w2, rw_a0, rw_a2,
              rw_g2, rw_k_k, rw_k_a, rw_r_k, rw_lnx_g, rw_lnx_b, mla_q_norm, mla_kv_norm,
              mla_w_uq, mla_w_ukv, df_lq1, df_lk1, df_lq2, df_lk2, df_subln, w_up_rw, w_up_mla,
              w_up_df, w_o, ln1_g, ln1_b, ln2_g, ln2_b, router_w, router_bias, ex_w_gate,
              ex_w_up, ex_w_down):
    params = (w_in, shift_prev, shift_next, rw_w0, rw_w2, rw_a0, rw_a2, rw_g2, rw_k_k, rw_k_a,
              rw_r_k, rw_lnx_g, rw_lnx_b, mla_q_norm, mla_kv_norm, mla_w_uq, mla_w_ukv,
              df_lq1, df_lk1, df_lq2, df_lk2, df_subln, w_up_rw, w_up_mla, w_up_df, w_o,
              ln1_g, ln1_b, ln2_g, ln2_b, router_w, router_bias, ex_w_gate, ex_w_up, ex_w_down)
    y_prompt = _trunk(x_prompt, *params)
    y_sample = _trunk(x_sample, *params)
    return (y_prompt, y_sample)
```

```python
import functools
import math

import jax
import jax.numpy as jnp
from jax import lax
from jax.experimental import pallas as pl
from jax.experimental.pallas import tpu as pltpu

F32 = jnp.float32
BF16 = jnp.bfloat16

D_MODEL = 2048
DEPTH = 2
RW_HEADS, RW_HEAD_DIM = 12, 64
RW_WIDTH = RW_HEADS * RW_HEAD_DIM
DECAY_LORA, ICLR_LORA, GATE_LORA, N_DIR = 64, 64, 128, 2
GN_EPS = 64e-5
MLA_HEADS, MLA_NOPE, MLA_ROPE, MLA_V = 8, 64, 32, 64
MLA_Q_LORA, MLA_KV_LORA = 512, 256
MLA_WIDTH = MLA_HEADS * MLA_V
DF_HEADS, DF_HEAD_DIM = 6, 64
DF_WIDTH = DF_HEADS * 2 * DF_HEAD_DIM
DF_EPS = 1e-5
N_BRANCH = 3
N_EXPERTS, N_GROUPS, TOP_K, D_EXPERT = 16, 4, 2, 1024
EXPERTS_PER_GROUP = N_EXPERTS // N_GROUPS
ROPE_THETA = 10000.0
LN_EPS = 1e-5
RMS_EPS = 1e-6
ALPHA = (2 * DEPTH) ** 0.25
RW_IN = 3 * RW_WIDTH + N_DIR * DECAY_LORA + N_DIR * ICLR_LORA + GATE_LORA
MLA_IN = MLA_Q_LORA + MLA_KV_LORA + MLA_ROPE
DF_IN = 3 * DF_WIDTH
GATE_IN = N_BRANCH * D_MODEL

LANES = 128
MLA_IN_PAD = 896
MLA_HEAD_PAD = 128
RW_CHUNK = 64
RW_PAIRS = RW_WIDTH // LANES
VMEM_LIMIT = 56 * 1024 * 1024
FFN_TILE = 512
ROW_SUB = D_MODEL // LANES


def _cparams(sem):
    return pltpu.CompilerParams(dimension_semantics=sem, vmem_limit_bytes=VMEM_LIMIT)


def _dot(a, b):
    return jnp.dot(a, b, preferred_element_type=F32)


def _dot_nt(a, b):
    return lax.dot_general(a, b, (((1,), (1,)), ((), ())), preferred_element_type=F32)


def _dot_tn(a, b):
    return lax.dot_general(a, b, (((0,), (0,)), ((), ())), preferred_element_type=F32)


def _split_bf16(x):
    hi = x.astype(BF16)
    lo = (x - hi.astype(F32)).astype(BF16)
    return hi, lo


def _pick_tile(n, candidates):
    for c in candidates:
        if n % c == 0:
            return c
    raise ValueError(f"no tile for {n}")


def _mm_kernel(x_ref, w_ref, o_ref, *, act):
    acc = _dot(x_ref[...], w_ref[...])
    if act == "sigmoid":
        acc = jax.nn.sigmoid(acc)
    o_ref[...] = acc.astype(o_ref.dtype)


def _matmul(x, w, out_dtype, act=None):
    m, k = x.shape
    n = w.shape[1]
    tm = _pick_tile(m, (1024, 512, 256, 128))
    tn = _pick_tile(n, (1024, 896, 768, 512, 256, 128))
    return pl.pallas_call(
        functools.partial(_mm_kernel, act=act),
        grid=(n // tn, m // tm),
        in_specs=[pl.BlockSpec((tm, k), lambda j, i: (i, 0)),
                  pl.BlockSpec((k, tn), lambda j, i: (0, j))],
        out_specs=pl.BlockSpec((tm, tn), lambda j, i: (i, j)),
        out_shape=jax.ShapeDtypeStruct((m, n), out_dtype),
        compiler_params=_cparams(("parallel", "parallel")),
    )(x, w)


def _rw_prep_kernel(z_ref, hp_ref, hn_ref, mup_ref, mun_ref, w2_ref, a2_ref, g2_ref, w0_ref,
                    a0_ref, kk_k_ref, k_a_ref, r_k_ref, bd_ref,
                    r_o, v_o, kk_o, g_o, bonus_o, logd_o, a_o, kdir_o):
    z = z_ref[0]
    tr = z.shape[0]
    row = lax.broadcasted_iota(jnp.int32, (tr, 1), 0)
    prev = jnp.where(row == 0, hp_ref[0, 0], pltpu.roll(z, 1, 0))
    nxt = jnp.where(row == tr - 1, hn_ref[0, 0], pltpu.roll(z, tr - 1, 0))
    zs = z + mup_ref[...] * (prev - z) + mun_ref[...] * (nxt - z)
    c = RW_WIDTH
    r, k, v = zs[:, 0:c], zs[:, c:2 * c], zs[:, 2 * c:3 * c]
    wl = zs[:, 3 * c:3 * c + 128]
    al = zs[:, 3 * c + 128:3 * c + 256]
    gl = zs[:, 3 * c + 256:3 * c + 384]
    w_raw = w0_ref[...] + _dot(jnp.tanh(wl).astype(BF16), w2_ref[...])
    a = jax.nn.sigmoid(a0_ref[...] + _dot(al.astype(BF16), a2_ref[...]))
    g = _dot(jax.nn.sigmoid(gl).astype(BF16), g2_ref[...])
    logd = (-math.exp(-0.5)) * jax.nn.sigmoid(w_raw)
    bd = bd_ref[...]

    def head_sum(x):
        hi, lo = _split_bf16(x)
        return _dot(hi, bd) + _dot(lo, bd)

    kk = k * kk_k_ref[...]
    kk = kk * lax.rsqrt(head_sum(kk * kk) + 1e-12)
    k_a = k_a_ref[...]
    kd0 = k * (1.0 + (a[:, 0:c] - 1.0) * k_a)
    kd1 = k * (1.0 + (a[:, c:2 * c] - 1.0) * k_a)
    bonus = head_sum(r * r_k_ref[...] * (kd0 + kd1)) * v
    r_o[0] = r
    v_o[0] = v
    kk_o[0] = kk
    g_o[0] = g
    bonus_o[0] = bonus
    logd_o[0] = logd
    a_o[0] = a
    kdir_o[0, :, 0:c] = kd0
    kdir_o[0, :, c:2 * c] = kd1


def _rw_prep(z_rw, p):
    b, s, _ = z_rw.shape
    tr = 256
    nt = s // tr
    zero = jnp.zeros((b, 1, RW_IN), F32)
    last = z_rw[:, tr - 1::tr]
    first = z_rw[:, 0::tr]
    halo_prev = jnp.concatenate([zero, last[:, :-1]], axis=1).reshape(b, nt, 1, RW_IN)
    halo_next = jnp.concatenate([first[:, 1:], zero], axis=1).reshape(b, nt, 1, RW_IN)
    c = RW_WIDTH
    tile = lambda w: pl.BlockSpec((1, tr, w), lambda bi, ti: (bi, ti, 0))
    halo = pl.BlockSpec((1, 1, 1, RW_IN), lambda bi, ti: (bi, ti, 0, 0))
    full = lambda a: pl.BlockSpec(a.shape, lambda bi, ti: (0,) * a.ndim)
    consts = [p["mu_prev"], p["mu_next"], p["w2cat"], p["a2cat"], p["g2"], p["w0"], p["a0"],
              p["k_k"], p["k_a"], p["r_k"], p["bd"]]
    out_w = [c, c, c, c, c, 2 * c, 2 * c, 2 * c]
    return pl.pallas_call(
        _rw_prep_kernel,
        grid=(b, nt),
        in_specs=[tile(RW_IN), halo, halo] + [full(a) for a in consts],
        out_specs=[tile(w) for w in out_w],
        out_shape=[jax.ShapeDtypeStruct((b, s, w), F32) for w in out_w],
        compiler_params=_cparams(("parallel", "parallel")),
    )(z_rw, halo_prev, halo_next, *consts)


def _rw_chunk(r, v, kk, ld, a, kd, z_ref, reverse, masks):
    incl, strict, incl_bf, lane_m, bdmask = masks
    ld_hi, ld_lo = _split_bf16(ld)
    logp = _dot(incl_bf, ld_hi) + _dot(incl_bf, ld_lo)
    ones = jnp.ones((RW_CHUNK, LANES), BF16)
    logpc = _dot_tn(ld_hi, ones) + _dot_tn(ld_lo, ones)
    pfull = jnp.exp(logp)
    pinv = jnp.exp(-logp)
    pex = jnp.exp(logp - ld)
    at = -(kk * pex)
    bt = (kk * a * pinv).astype(BF16)
    kt = (kd * pinv).astype(BF16)
    rt = r * pfull
    z = z_ref[...]
    zb = z.astype(BF16)
    vb = v.astype(BF16)
    zero = jnp.zeros_like(at)
    y = jnp.zeros((RW_CHUNK, LANES), F32)
    u = jnp.zeros((RW_CHUNK, LANES), F32)
    for h in range(2):
        m = lane_m[h]
        lhs = jnp.concatenate([jnp.where(m, at, zero), jnp.where(m, rt, zero)], axis=0).astype(BF16)
        gb = _dot_nt(lhs, bt)
        gk = _dot_nt(lhs, kt)
        a_ab = jnp.where(strict, gb[0:RW_CHUNK], 0.0).astype(BF16)
        a_ak = jnp.where(strict, gk[0:RW_CHUNK], 0.0).astype(BF16)
        a_rb = jnp.where(incl, gb[RW_CHUNK:], 0.0).astype(BF16)
        a_rk = jnp.where(incl, gk[RW_CHUNK:], 0.0).astype(BF16)
        vh = jnp.where(m, v, zero).astype(BF16)
        x = _dot(lhs[0:RW_CHUNK], zb) + _dot(a_ak, vh)
        ap = a_ab
        for i in range(6):
            x = x + _dot(ap, x.astype(BF16))
            if i < 5:
                ap = _dot(ap, ap).astype(BF16)
        u = u + x
        y = y + _dot(lhs[RW_CHUNK:], zb) + _dot(a_rb, x.astype(BF16)) + _dot(a_rk, vh)
    upd = _dot_tn(jnp.concatenate([bt, kt], axis=0),
                  jnp.concatenate([u.astype(BF16), vb], axis=0))
    z_ref[...] = jnp.where(bdmask, jnp.exp(logpc) * (z + upd), 0.0)
    return y


def _rw_scan_kernel(rf, vf, kkf, ldf, af, kdf, rb, vb, kkb, ldb, ab, kdb, yf_o, yb_o, z_scr):
    @pl.when(pl.program_id(1) == 0)
    def _():
        z_scr[...] = jnp.zeros_like(z_scr)

    ti = lax.broadcasted_iota(jnp.int32, (RW_CHUNK, RW_CHUNK), 0)
    tj = lax.broadcasted_iota(jnp.int32, (RW_CHUNK, RW_CHUNK), 1)
    lane = lax.broadcasted_iota(jnp.int32, (1, LANES), 1)
    lane_m = (lane < RW_HEAD_DIM, lane >= RW_HEAD_DIM)
    bi = lax.broadcasted_iota(jnp.int32, (LANES, LANES), 0) // RW_HEAD_DIM
    bj = lax.broadcasted_iota(jnp.int32, (LANES, LANES), 1) // RW_HEAD_DIM
    bdmask = bi == bj
    for d, (refs, y_o) in enumerate((((rf, vf, kkf, ldf, af, kdf), yf_o),
                                     ((rb, vb, kkb, ldb, ab, kdb), yb_o))):
        incl = (ti >= tj) if d == 0 else (ti <= tj)
        strict = (ti > tj) if d == 0 else (ti < tj)
        masks = (incl, strict, incl.astype(BF16), lane_m, bdmask)
        for p in range(RW_PAIRS):
            sl = slice(p * LANES, (p + 1) * LANES)
            args = [ref[0, :, sl] for ref in refs]
            y_o[0, :, sl] = _rw_chunk(*args, z_scr.at[d, p], d == 1, masks)


def _rw_scan(r, v, kk, logd, a, kdir):
    b, s, c = r.shape
    nc = s // RW_CHUNK
    fwd = pl.BlockSpec((1, RW_CHUNK, c), lambda bi, ci: (bi, ci, 0))
    bwd = pl.BlockSpec((1, RW_CHUNK, c), lambda bi, ci: (bi, nc - 1 - ci, 0))
    bwd_dir = pl.BlockSpec((1, RW_CHUNK, c), lambda bi, ci: (bi, nc - 1 - ci, 1))
    return pl.pallas_call(
        _rw_scan_kernel,
        grid=(b, nc),
        in_specs=[fwd, fwd, fwd, fwd, fwd, fwd, bwd, bwd, bwd, bwd_dir, bwd_dir, bwd_dir],
        out_specs=[fwd, bwd],
        out_shape=[jax.ShapeDtypeStruct((b, s, c), F32)] * 2,
        scratch_shapes=[pltpu.VMEM((N_DIR, RW_PAIRS, LANES, LANES), F32)],
        compiler_params=_cparams(("parallel", "arbitrary")),
    )(r, v, kk, logd, a, kdir, r, v, kk, logd, a, kdir)


def _rw_post_kernel(yf_ref, yb_ref, bonus_ref, g_ref, lg_ref, lb_ref, bd_ref, o_ref):
    y = yf_ref[...] + yb_ref[...]
    bd = bd_ref[...]

    def head_mean(x):
        hi, lo = _split_bf16(x)
        return (_dot(hi, bd) + _dot(lo, bd)) * (1.0 / RW_HEAD_DIM)

    mu = head_mean(y)
    yc = y - mu
    var = head_mean(yc * yc)
    yn = yc * lax.rsqrt(var + GN_EPS) * lg_ref[...] + lb_ref[...]
    o_ref[...] = ((yn + bonus_ref[...]) * g_ref[...]).astype(o_ref.dtype)


def _rw_post(yf, yb, bonus, g, p):
    t, c = yf.shape
    tm = _pick_tile(t, (1024, 512, 256))
    tile = pl.BlockSpec((tm, c), lambda i: (i, 0))
    full = lambda a: pl.BlockSpec(a.shape, lambda i: (0,) * a.ndim)
    consts = [p["lnx_g"], p["lnx_b"], p["bd"]]
    return pl.pallas_call(
        _rw_post_kernel,
        grid=(t // tm,),
        in_specs=[tile] * 4 + [full(a) for a in consts],
        out_specs=tile,
        out_shape=jax.ShapeDtypeStruct((t, c), BF16),
        compiler_params=_cparams(("parallel",)),
    )(yf, yb, bonus, g, *consts)


def _rope_block(x, cos, s_up, s_dn, half):
    return x * cos + pltpu.roll(x, half, 1) * s_up + pltpu.roll(x, LANES - half, 1) * s_dn


def _rope_tables(s, dim, lane_of_x1, period):
    half = dim // 2
    inv_freq = jnp.power(ROPE_THETA, -jnp.arange(half, dtype=F32) * (2.0 / dim))
    ang = jnp.arange(s, dtype=F32)[:, None] * inv_freq[None, :]
    lane = jnp.arange(LANES) % period - lane_of_x1
    in_x1 = (lane >= 0) & (lane < half)
    in_x2 = (lane >= half) & (lane < dim)
    j = jnp.clip(jnp.where(in_x2, lane - half, lane), 0, half - 1)
    cos = jnp.cos(ang)[:, j]
    sin = jnp.sin(ang)[:, j]
    cos_t = jnp.where(in_x1 | in_x2, cos, 1.0)
    s_up = jnp.where(in_x2, sin, 0.0)
    s_dn = jnp.where(in_x1, -sin, 0.0)
    return cos_t.astype(F32), s_up.astype(F32), s_dn.astype(F32)


def _mla_prep_kernel(z_ref, qg_ref, kvg_ref, wq_ref, wk_ref, wv_ref,
                     cq_ref, squ_ref, sqd_ref, ck_ref, sku_ref, skd_ref, q_o, k_o, v_o):
    z = z_ref[0]
    c_q = z[:, 0:MLA_Q_LORA]
    c_kv = z[:, MLA_Q_LORA:MLA_Q_LORA + MLA_KV_LORA]
    kr = z[:, MLA_Q_LORA + MLA_KV_LORA:MLA_IN_PAD]
    c_q = c_q * lax.rsqrt(jnp.mean(c_q * c_q, -1, keepdims=True) + RMS_EPS) * qg_ref[...]
    c_kv = c_kv * lax.rsqrt(jnp.mean(c_kv * c_kv, -1, keepdims=True) + RMS_EPS) * kvg_ref[...]
    q = _dot(c_q.astype(BF16), wq_ref[...])
    scale = (MLA_NOPE + MLA_ROPE) ** -0.5
    cq, squ, sqd = cq_ref[...], squ_ref[...], sqd_ref[...]
    for h in range(MLA_HEADS):
        sl = slice(h * MLA_HEAD_PAD, (h + 1) * MLA_HEAD_PAD)
        q_o[0, :, sl] = (_rope_block(q[:, sl], cq, squ, sqd, MLA_ROPE // 2) * scale).astype(BF16)
    kr = _rope_block(kr, ck_ref[...], sku_ref[...], skd_ref[...], MLA_ROPE // 2)
    ckv_b = c_kv.astype(BF16)
    k_in = jnp.concatenate([ckv_b, kr.astype(BF16)], axis=1)
    k_o[0] = _dot(k_in, wk_ref[...]).astype(BF16)
    v_o[0] = _dot(ckv_b, wv_ref[...]).astype(BF16)


def _mla_prep(z_mla, p, tabs):
    b, s, _ = z_mla.shape
    tr = 256
    tile = lambda w: pl.BlockSpec((1, tr, w), lambda bi, ti: (bi, ti, 0))
    full = lambda a: pl.BlockSpec(a.shape, lambda bi, ti: (0,) * a.ndim)
    tab = pl.BlockSpec((tr, LANES), lambda bi, ti: (ti, 0))
    consts = [p["q_norm"], p["kv_norm"], p["wq"], p["wk"], p["wv"]]
    hq = MLA_HEADS * MLA_HEAD_PAD
    return pl.pallas_call(
        _mla_prep_kernel,
        grid=(b, s // tr),
        in_specs=[tile(MLA_IN_PAD)] + [full(a) for a in consts] + [tab] * 6,
        out_specs=[tile(hq), tile(hq), tile(MLA_WIDTH)],
        out_shape=[jax.ShapeDtypeStruct((b, s, hq), BF16), jax.ShapeDtypeStruct((b, s, hq), BF16),
                   jax.ShapeDtypeStruct((b, s, MLA_WIDTH), BF16)],
        compiler_params=_cparams(("parallel", "parallel")),
    )(z_mla, *consts, *tabs["mla_q"], *tabs["mla_k"])


def _mla_attn_kernel(q_ref, k_ref, v_ref, o_ref):
    v = v_ref[0]
    outs = []
    for h in range(2):
        sl = slice(h * MLA_HEAD_PAD, (h + 1) * MLA_HEAD_PAD)
        s = _dot_nt(q_ref[0, :, sl], k_ref[0, :, sl])
        pr = jnp.exp(s - jnp.max(s, -1, keepdims=True))
        l = jnp.sum(pr, -1, keepdims=True)
        outs.append(_dot(pr.astype(BF16), v) * (1.0 / l))
    lane = lax.broadcasted_iota(jnp.int32, (1, LANES), 1)
    o_ref[0] = jnp.where(lane < MLA_V, outs[0], outs[1]).astype(o_ref.dtype)


def _mla_attn(q, k, v):
    b, s, _ = q.shape
    tq = 256
    npair = MLA_HEADS // 2
    return pl.pallas_call(
        _mla_attn_kernel,
        grid=(b, npair, s // tq),
        in_specs=[pl.BlockSpec((1, tq, 2 * MLA_HEAD_PAD), lambda bi, pi, qi: (bi, qi, pi)),
                  pl.BlockSpec((1, s, 2 * MLA_HEAD_PAD), lambda bi, pi, qi: (bi, 0, pi)),
                  pl.BlockSpec((1, s, LANES), lambda bi, pi, qi: (bi, 0, pi))],
        out_specs=pl.BlockSpec((1, tq, LANES), lambda bi, pi, qi: (bi, qi, pi)),
        out_shape=jax.ShapeDtypeStruct((b, s, MLA_WIDTH), BF16),
        compiler_params=_cparams(("parallel", "parallel", "parallel")),
    )(q, k, v)


def _df_prep_kernel(z_ref, c_ref, su_ref, sd_ref, q_o, k_o, v_o):
    cos, s_up, s_dn = c_ref[...], su_ref[...], sd_ref[...]
    scale = DF_HEAD_DIM ** -0.5
    for h in range(DF_HEADS):
        sl = slice(h * LANES, (h + 1) * LANES)
        q_o[0, :, sl] = (_rope_block(z_ref[0, :, sl], cos, s_up, s_dn, DF_HEAD_DIM // 2)
                         * scale).astype(BF16)
        slk = slice(DF_WIDTH + h * LANES, DF_WIDTH + (h + 1) * LANES)
        k_o[0, :, sl] = _rope_block(z_ref[0, :, slk], cos, s_up, s_dn,
                                    DF_HEAD_DIM // 2).astype(BF16)
    v_o[0] = z_ref[0, :, 2 * DF_WIDTH:3 * DF_WIDTH].astype(BF16)


def _df_prep(z_df, tabs):
    b, s, _ = z_df.shape
    tr = 256
    tile = lambda w: pl.BlockSpec((1, tr, w), lambda bi, ti: (bi, ti, 0))
    tab = pl.BlockSpec((tr, LANES), lambda bi, ti: (ti, 0))
    return pl.pallas_call(
        _df_prep_kernel,
        grid=(b, s // tr),
        in_specs=[tile(DF_IN)] + [tab] * 3,
        out_specs=[tile(DF_WIDTH)] * 3,
        out_shape=[jax.ShapeDtypeStruct((b, s, DF_WIDTH), BF16)] * 3,
        compiler_params=_cparams(("parallel", "parallel")),
    )(z_df, *tabs["df"])


def _df_attn_kernel(q_ref, k_ref, v_ref, lq1, lk1, lq2, lk2, g_ref, o_ref, *, lambda_init):
    lam = (jnp.exp(jnp.sum(lq1[...] * lk1[...], -1, keepdims=True))
           - jnp.exp(jnp.sum(lq2[...] * lk2[...], -1, keepdims=True)) + lambda_init)
    q = q_ref[0]
    k = k_ref[0]
    v = v_ref[0]
    lane = lax.broadcasted_iota(jnp.int32, (1, LANES), 1)
    zero = jnp.zeros_like(q)
    outs = []
    for h in range(2):
        m = (lane < DF_HEAD_DIM) if h == 0 else (lane >= DF_HEAD_DIM)
        s = _dot_nt(jnp.where(m, q, zero), k)
        pr = jnp.exp(s - jnp.max(s, -1, keepdims=True))
        l = jnp.sum(pr, -1, keepdims=True)
        outs.append(_dot(pr.astype(BF16), v) * (1.0 / l))
    o = outs[0] - lam * outs[1]
    o = o * lax.rsqrt(jnp.mean(o * o, -1, keepdims=True) + DF_EPS) * g_ref[...]
    o_ref[0] = (o * (1.0 - lambda_init)).astype(o_ref.dtype)


def _df_attn(q, k, v, p, lambda_init):
    b, s, _ = q.shape
    tq = 256
    full = lambda a: pl.BlockSpec(a.shape, lambda bi, hi, qi: (0,) * a.ndim)
    consts = [p["lq1"], p["lk1"], p["lq2"], p["lk2"], p["subln"]]
    return pl.pallas_call(
        functools.partial(_df_attn_kernel, lambda_init=lambda_init),
        grid=(b, DF_HEADS, s // tq),
        in_specs=[pl.BlockSpec((1, tq, LANES), lambda bi, hi, qi: (bi, qi, hi)),
                  pl.BlockSpec((1, s, LANES), lambda bi, hi, qi: (bi, 0, hi)),
                  pl.BlockSpec((1, s, LANES), lambda bi, hi, qi: (bi, 0, hi))]
                 + [full(a) for a in consts],
        out_specs=pl.BlockSpec((1, tq, LANES), lambda bi, hi, qi: (bi, qi, hi)),
        out_shape=jax.ShapeDtypeStruct((b, s, DF_WIDTH), BF16),
        compiler_params=_cparams(("parallel", "parallel", "parallel")),
    )(q, k, v, *consts)


def _merge_kernel(orw, omla, odf, g0, g1, g2, w0, w1, w2, o_ref):
    acc = g0[...].astype(F32) * _dot(orw[...], w0[...])
    acc = acc + g1[...].astype(F32) * _dot(omla[...], w1[...])
    acc = acc + g2[...].astype(F32) * _dot(odf[...], w2[...])
    o_ref[...] = acc.astype(o_ref.dtype)


def _merge(o_rw, o_mla, o_df, gates, p):
    t = o_rw.shape[0]
    tm = _pick_tile(t, (1024, 512, 256))
    tn = 1024
    nj = D_MODEL // tn
    act = lambda w: pl.BlockSpec((tm, w), lambda i, j: (i, 0))
    gate = lambda br: pl.BlockSpec((tm, tn), lambda i, j: (i, j + br * nj))
    wt = lambda w: pl.BlockSpec((w, tn), lambda i, j: (0, j))
    return pl.pallas_call(
        _merge_kernel,
        grid=(t // tm, nj),
        in_specs=[act(RW_WIDTH), act(MLA_WIDTH), act(DF_WIDTH), gate(0), gate(1), gate(2),
                  wt(RW_WIDTH), wt(MLA_WIDTH), wt(DF_WIDTH)],
        out_specs=pl.BlockSpec((tm, tn), lambda i, j: (i, j)),
        out_shape=jax.ShapeDtypeStruct((t, D_MODEL), BF16),
        compiler_params=_cparams(("parallel", "parallel")),
    )(o_rw, o_mla, o_df, gates, gates, gates, p["w_up_rw"], p["w_up_mla"], p["w_up_df"])


def _layer_norm(y, g, b):
    mu = jnp.mean(y, -1, keepdims=True)
    yc = y - mu
    var = jnp.mean(yc * yc, -1, keepdims=True)
    return yc * lax.rsqrt(var + LN_EPS) * g + b


def _wo_ln_kernel(m_ref, w_ref, x_ref, g_ref, b_ref, o_ref, ob_ref):
    y = ALPHA * x_ref[...] + _dot(m_ref[...], w_ref[...])
    out = _layer_norm(y, g_ref[...], b_ref[...])
    o_ref[...] = out
    ob_ref[...] = out.astype(BF16)


def _wo_ln(merged, x, p):
    t = x.shape[0]
    tm = 256
    tile = pl.BlockSpec((tm, D_MODEL), lambda i: (i, 0))
    full = lambda a: pl.BlockSpec(a.shape, lambda i: (0,) * a.ndim)
    return pl.pallas_call(
        _wo_ln_kernel,
        grid=(t // tm,),
        in_specs=[tile, full(p["w_o"]), tile, full(p["ln1_g"]), full(p["ln1_b"])],
        out_specs=[tile, tile],
        out_shape=[jax.ShapeDtypeStruct((t, D_MODEL), F32), jax.ShapeDtypeStruct((t, D_MODEL), BF16)],
        compiler_params=_cparams(("parallel",)),
    )(merged, p["w_o"], x, p["ln1_g"], p["ln1_b"])


ROUTER_TILE = 1024


def _router_kernel(x_ref, wh_ref, wl_ref, bias_ref, tri_ref, idx_o, wts_o, rank_o, cnt_o, cnt_scr):
    @pl.when(pl.program_id(0) == 0)
    def _():
        cnt_scr[...] = jnp.zeros_like(cnt_scr)

    xh, xl = _split_bf16(x_ref[...])
    wh, wl = wh_ref[...], wl_ref[...]
    logits = _dot_nt(wh, xh) + (_dot_nt(wh, xl) + _dot_nt(wl, xh))
    scores = jax.nn.sigmoid(logits)
    sel = scores + bias_ref[...]
    tm = sel.shape[1]

    def row(a, i):
        return a[i:i + 1, :]

    best = jnp.zeros((1, tm), jnp.int32)
    best_s = None
    for g in range(N_GROUPS):
        a, b, c, d = (row(sel, EXPERTS_PER_GROUP * g + j) for j in range(4))
        hi1, lo1 = jnp.maximum(a, b), jnp.minimum(a, b)
        hi2, lo2 = jnp.maximum(c, d), jnp.minimum(c, d)
        gs = jnp.maximum(hi1, hi2) + jnp.maximum(jnp.minimum(hi1, hi2), jnp.maximum(lo1, lo2))
        if g == 0:
            best_s = gs
        else:
            upd = gs > best_s
            best = jnp.where(upd, g, best)
            best_s = jnp.where(upd, gs, best_s)

    def pick(a, j):
        out = row(a, j)
        for g in range(1, N_GROUPS):
            out = jnp.where(best == g, row(a, EXPERTS_PER_GROUP * g + j), out)
        return out

    cand = [pick(sel, j) for j in range(EXPERTS_PER_GROUP)]
    csc = [pick(scores, j) for j in range(EXPERTS_PER_GROUP)]
    neg = jnp.float32(-jnp.inf)

    def argmax4(vals):
        bi, bv = jnp.zeros((1, tm), jnp.int32), vals[0]
        for j in range(1, EXPERTS_PER_GROUP):
            upd = vals[j] > bv
            bi = jnp.where(upd, j, bi)
            bv = jnp.where(upd, vals[j], bv)
        return bi

    i1 = argmax4(cand)
    i2 = argmax4([jnp.where(i1 == j, neg, cand[j]) for j in range(EXPERTS_PER_GROUP)])

    def take(vals, i):
        out = vals[0]
        for j in range(1, EXPERTS_PER_GROUP):
            out = jnp.where(i == j, vals[j], out)
        return out

    w1, w2 = take(csc, i1), take(csc, i2)
    tot = w1 + w2
    e1 = best * EXPERTS_PER_GROUP + i1
    e2 = best * EXPERTS_PER_GROUP + i2
    eid = lax.broadcasted_iota(jnp.int32, (N_EXPERTS, tm), 0)
    oh1 = eid == e1
    oh2 = eid == e2
    oh = (oh1 | oh2).astype(BF16)
    before = _dot(oh, tri_ref[...]) + cnt_scr[...][:, 0:1]
    r1 = jnp.sum(jnp.where(oh1, before, 0.0), 0, keepdims=True)
    r2 = jnp.sum(jnp.where(oh2, before, 0.0), 0, keepdims=True)
    idx_o[0:1, :] = e1
    idx_o[1:2, :] = e2
    wts_o[0:1, :] = w1 / tot
    wts_o[1:2, :] = w2 / tot
    rank_o[0:1, :] = r1.astype(jnp.int32)
    rank_o[1:2, :] = r2.astype(jnp.int32)
    new_cnt = cnt_scr[...] + jnp.sum(oh.astype(F32), 1, keepdims=True)
    cnt_scr[...] = new_cnt
    cnt_o[...] = new_cnt


def _router(x, p):
    t = x.shape[0]
    tm = ROUTER_TILE
    full = lambda a: pl.BlockSpec(a.shape, lambda i: (0,) * a.ndim)
    tok = pl.BlockSpec((TOP_K, tm), lambda i: (0, i))
    consts = [p["router_wt_hi"], p["router_wt_lo"], p["router_bias"], p["router_tri"]]
    return pl.pallas_call(
        _router_kernel,
        grid=(t // tm,),
        in_specs=[pl.BlockSpec((tm, D_MODEL), lambda i: (i, 0))] + [full(a) for a in consts],
        out_specs=[tok, tok, tok, pl.BlockSpec((N_EXPERTS, LANES), lambda i: (0, 0))],
        out_shape=[jax.ShapeDtypeStruct((TOP_K, t), jnp.int32), jax.ShapeDtypeStruct((TOP_K, t), F32),
                   jax.ShapeDtypeStruct((TOP_K, t), jnp.int32),
                   jax.ShapeDtypeStruct((N_EXPERTS, LANES), F32)],
        scratch_shapes=[pltpu.VMEM((N_EXPERTS, LANES), F32)],
        compiler_params=_cparams(("arbitrary",)),
    )(x, *consts)


DISPATCH_TILE = 512


def _dispatch_kernel(dest_ref, x_ref, init_ref, xs_ref, sem):
    del init_ref
    base = pl.program_id(0) * DISPATCH_TILE

    def copy(r, k):
        return pltpu.make_async_copy(x_ref.at[base + r],
                                     xs_ref.at[dest_ref[0, TOP_K * r + k]], sem)

    def start(r, carry):
        copy(r, 0).start()
        copy(r, 1).start()
        return carry

    def wait(r, carry):
        copy(r, 0).wait()
        copy(r, 1).wait()
        return carry

    lax.fori_loop(0, DISPATCH_TILE, start, 0)
    lax.fori_loop(0, DISPATCH_TILE, wait, 0)


def _dispatch(xb, dest, n_rows):
    t = xb.shape[0]
    nt = t // DISPATCH_TILE
    dest2 = dest.reshape(nt, 1, DISPATCH_TILE * TOP_K)
    xb = xb.reshape(t, ROW_SUB, LANES)
    init = jnp.zeros((n_rows, ROW_SUB, LANES), BF16)
    xs = pl.pallas_call(
        _dispatch_kernel,
        grid=(nt,),
        in_specs=[pl.BlockSpec((None, 1, DISPATCH_TILE * TOP_K), lambda i: (i, 0, 0),
                               memory_space=pltpu.SMEM),
                  pl.BlockSpec(memory_space=pl.ANY), pl.BlockSpec(memory_space=pl.ANY)],
        out_specs=pl.BlockSpec(memory_space=pl.ANY),
        out_shape=jax.ShapeDtypeStruct((n_rows, ROW_SUB, LANES), BF16),
        scratch_shapes=[pltpu.SemaphoreType.DMA(())],
        input_output_aliases={2: 0},
        compiler_params=_cparams(("arbitrary",)),
    )(dest2, xb, init)
    return xs.reshape(n_rows, D_MODEL)


def _ffn_kernel(te_ref, nu_ref, xs_ref, wg_ref, wu_ref, wd_ref, y_ref):
    del te_ref

    @pl.when(pl.program_id(0) < nu_ref[0])
    def _():
        xs = xs_ref[...]
        h = jax.nn.silu(_dot(xs, wg_ref[0])) * _dot(xs, wu_ref[0])
        y_ref[...] = _dot(h.astype(BF16), wd_ref[0])

    @pl.when(pl.program_id(0) >= nu_ref[0])
    def _():
        y_ref[...] = jnp.zeros_like(y_ref)


def _ffn(xs, tile_expert, n_used, p):
    n_rows = xs.shape[0]
    tm = FFN_TILE
    grid_spec = pltpu.PrefetchScalarGridSpec(
        num_scalar_prefetch=2,
        grid=(n_rows // tm,),
        in_specs=[pl.BlockSpec((tm, D_MODEL), lambda i, te, nu: (i, 0)),
                  pl.BlockSpec((1, D_MODEL, D_EXPERT), lambda i, te, nu: (te[i], 0, 0)),
                  pl.BlockSpec((1, D_MODEL, D_EXPERT), lambda i, te, nu: (te[i], 0, 0)),
                  pl.BlockSpec((1, D_EXPERT, D_MODEL), lambda i, te, nu: (te[i], 0, 0))],
        out_specs=pl.BlockSpec((tm, D_MODEL), lambda i, te, nu: (i, 0)),
    )
    return pl.pallas_call(
        _ffn_kernel,
        grid_spec=grid_spec,
        out_shape=jax.ShapeDtypeStruct((n_rows, D_MODEL), F32),
        compiler_params=_cparams(("arbitrary",)),
    )(tile_expert, n_used, xs, p["ex_w_gate"], p["ex_w_up"], p["ex_w_down"])


COMBINE_TILE = 256


def _combine_kernel(dest_ref, y_ref, w0_ref, w1_ref, o_ref, buf, sem):
    def copy(r, k):
        return pltpu.make_async_copy(y_ref.at[dest_ref[0, TOP_K * r + k]], buf.at[k, r], sem)

    def start(r, carry):
        copy(r, 0).start()
        copy(r, 1).start()
        return carry

    def wait(r, carry):
        copy(r, 0).wait()
        copy(r, 1).wait()
        return carry

    lax.fori_loop(0, COMBINE_TILE, start, 0)
    lax.fori_loop(0, COMBINE_TILE, wait, 0)
    o_ref[...] = w0_ref[...] * buf[0] + w1_ref[...] * buf[1]


def _combine(y, dest, wts):
    t = wts.shape[1]
    tm = COMBINE_TILE
    nt = t // tm
    dest2 = dest.reshape(nt, 1, tm * TOP_K)
    y = y.reshape(y.shape[0], ROW_SUB, LANES)
    w0 = wts[0].reshape(t, 1, 1)
    w1 = wts[1].reshape(t, 1, 1)
    wspec = pl.BlockSpec((tm, 1, 1), lambda i: (i, 0, 0))
    out = pl.pallas_call(
        _combine_kernel,
        grid=(nt,),
        in_specs=[pl.BlockSpec((None, 1, tm * TOP_K), lambda i: (i, 0, 0), memory_space=pltpu.SMEM),
                  pl.BlockSpec(memory_space=pl.ANY), wspec, wspec],
        out_specs=pl.BlockSpec((tm, ROW_SUB, LANES), lambda i: (i, 0, 0)),
        out_shape=jax.ShapeDtypeStruct((t, ROW_SUB, LANES), F32),
        scratch_shapes=[pltpu.VMEM((TOP_K, tm, ROW_SUB, LANES), F32), pltpu.SemaphoreType.DMA(())],
        compiler_params=_cparams(("arbitrary",)),
    )(dest2, y, w0, w1)
    return out.reshape(t, D_MODEL)


def _res_ln_kernel(x_ref, f_ref, g_ref, b_ref, o_ref):
    o_ref[...] = _layer_norm(ALPHA * x_ref[...] + f_ref[...], g_ref[...], b_ref[...])


def _res_ln(x, f, g, b):
    t = x.shape[0]
    tm = 512
    tile = pl.BlockSpec((tm, D_MODEL), lambda i: (i, 0))
    full = lambda a: pl.BlockSpec(a.shape, lambda i: (0,) * a.ndim)
    return pl.pallas_call(
        _res_ln_kernel,
        grid=(t // tm,),
        in_specs=[tile, tile, full(g), full(b)],
        out_specs=tile,
        out_shape=jax.ShapeDtypeStruct((t, D_MODEL), F32),
        compiler_params=_cparams(("parallel",)),
    )(x, f, g, b)


def _moe_ln(x, xb, p):
    t = x.shape[0]
    idx, wts, rank, counts = _router(x, p)
    counts = counts[:, 0].astype(jnp.int32)
    tiles = (counts + FFN_TILE - 1) // FFN_TILE
    tile_end = jnp.cumsum(tiles)
    row_start = (tile_end - tiles) * FFN_TILE
    n_tiles = (t * TOP_K) // FFN_TILE + N_EXPERTS
    dest = (row_start[idx] + rank).T.reshape(-1)
    tile_expert = jnp.minimum(jnp.searchsorted(tile_end, jnp.arange(n_tiles), side="right"),
                              N_EXPERTS - 1).astype(jnp.int32)
    n_used = tile_end[-1:].astype(jnp.int32)
    xs = _dispatch(xb, dest, n_tiles * FFN_TILE)
    y = _ffn(xs, tile_expert, n_used, p)
    return _res_ln(x, _combine(y, dest, wts), p["ln2_g"], p["ln2_b"])


def _block_diag_ones(n, blk):
    i = jnp.arange(n) // blk
    return (i[:, None] == i[None, :]).astype(BF16)


def _prep_layer(l, w):
    row = lambda a: a.reshape(1, -1).astype(F32)
    w_in = w["w_in"][l]
    o1, o2, o3 = RW_IN, RW_IN + MLA_IN, RW_IN + MLA_IN + DF_IN
    p = {
        "w_in_rw": w_in[:, :o1].astype(BF16),
        "w_in_mla": jnp.pad(w_in[:, o1:o2], ((0, 0), (0, MLA_IN_PAD - MLA_IN))).astype(BF16),
        "w_in_df": w_in[:, o2:o3].astype(BF16),
        "w_in_gate": w_in[:, o3:].astype(BF16),
        "mu_prev": row(w["shift_prev"][l]), "mu_next": row(w["shift_next"][l]),
        "w0": row(w["rw_w0"][l]), "a0": row(w["rw_a0"][l]),
        "g2": w["rw_g2"][l].astype(BF16),
        "k_k": row(w["rw_k_k"][l]), "k_a": row(w["rw_k_a"][l]), "r_k": row(w["rw_r_k"][l]),
        "lnx_g": row(w["rw_lnx_g"][l]), "lnx_b": row(w["rw_lnx_b"][l]),
        "bd": _block_diag_ones(RW_WIDTH, RW_HEAD_DIM),
        "q_norm": row(w["mla_q_norm"][l]), "kv_norm": row(w["mla_kv_norm"][l]),
        "lq1": row(w["df_lq1"][l]), "lk1": row(w["df_lk1"][l]),
        "lq2": row(w["df_lq2"][l]), "lk2": row(w["df_lk2"][l]),
        "subln": row(w["df_subln"][l]),
        "w_up_rw": w["w_up_rw"][l].astype(BF16), "w_up_mla": w["w_up_mla"][l].astype(BF16),
        "w_up_df": w["w_up_df"][l].astype(BF16), "w_o": w["w_o"][l].astype(BF16),
        "ln1_g": row(w["ln1_g"][l]), "ln1_b": row(w["ln1_b"][l]),
        "ln2_g": row(w["ln2_g"][l]), "ln2_b": row(w["ln2_b"][l]),
        "ex_w_gate": w["ex_w_gate"][l].astype(BF16), "ex_w_up": w["ex_w_up"][l].astype(BF16),
        "ex_w_down": w["ex_w_down"][l].astype(BF16),
    }
    zc = jnp.zeros((DECAY_LORA, RW_WIDTH), F32)
    w2 = w["rw_w2"][l]
    a2 = w["rw_a2"][l]
    p["w2cat"] = jnp.block([[w2[0], zc], [zc, w2[1]]]).astype(BF16)
    p["a2cat"] = jnp.block([[a2[0], zc], [zc, a2[1]]]).astype(BF16)
    wq = w["mla_w_uq"][l].reshape(MLA_Q_LORA, MLA_HEADS, MLA_NOPE + MLA_ROPE)
    p["wq"] = jnp.pad(wq, ((0, 0), (0, 0), (0, MLA_HEAD_PAD - MLA_NOPE - MLA_ROPE))
                      ).reshape(MLA_Q_LORA, -1).astype(BF16)
    wkv = w["mla_w_ukv"][l].reshape(MLA_KV_LORA, MLA_HEADS, MLA_NOPE + MLA_V)
    wk_nope = jnp.pad(wkv[:, :, :MLA_NOPE], ((0, 0), (0, 0), (0, MLA_HEAD_PAD - MLA_NOPE)))
    place = jnp.zeros((LANES, MLA_HEADS, MLA_HEAD_PAD), F32)
    j = jnp.arange(MLA_ROPE)
    place = place.at[j, :, MLA_NOPE + j].set(1.0)
    p["wk"] = jnp.concatenate([wk_nope, place], axis=0).reshape(MLA_KV_LORA + LANES, -1).astype(BF16)
    p["wv"] = wkv[:, :, MLA_NOPE:].reshape(MLA_KV_LORA, -1).astype(BF16)
    return p


def _trunk(x3, layers, shared):
    b, s, d = x3.shape
    t = b * s
    tabs = {
        "mla_q": _rope_tables(s, MLA_ROPE, MLA_NOPE, LANES),
        "mla_k": _rope_tables(s, MLA_ROPE, 0, LANES),
        "df": _rope_tables(s, DF_HEAD_DIM, 0, DF_HEAD_DIM),
    }
    x = x3.reshape(t, d)
    xb = x.astype(BF16)
    for l, p in enumerate(layers):
        p = dict(p, **shared)
        lambda_init = 0.8 - 0.6 * math.exp(-0.3 * l)
        z_rw = _matmul(xb, p["w_in_rw"], F32).reshape(b, s, -1)
        z_mla = _matmul(xb, p["w_in_mla"], F32).reshape(b, s, -1)
        z_df = _matmul(xb, p["w_in_df"], F32).reshape(b, s, -1)
        gates = _matmul(xb, p["w_in_gate"], BF16, act="sigmoid")
        r, v, kk, g, bonus, logd, a, kdir = _rw_prep(z_rw, p)
        yf, yb = _rw_scan(r, v, kk, logd, a, kdir)
        o_rw = _rw_post(yf.reshape(t, -1), yb.reshape(t, -1), bonus.reshape(t, -1),
                        g.reshape(t, -1), p)
        q, k, v2 = _mla_prep(z_mla, p, tabs)
        o_mla = _mla_attn(q, k, v2).reshape(t, -1)
        q, k, v2 = _df_prep(z_df, tabs)
        o_df = _df_attn(q, k, v2, p, lambda_init).reshape(t, -1)
        merged = _merge(o_rw, o_mla, o_df, gates, p)
        x, xb = _wo_ln(merged, x, p)
        x = _moe_ln(x, xb, p)
        xb = x.astype(BF16)
    return x.reshape(b, s, d)


def kernel(x_prompt, x_sample, w_in, shift_prev, shift_next, rw_w0, rw_w2, rw_a0, rw_a2, rw_g2,
           rw_k_k, rw_k_a, rw_r_k, rw_lnx_g, rw_lnx_b, mla_q_norm, mla_kv_norm, mla_w_uq,
           mla_w_ukv, df_lq1, df_lk1, df_lq2, df_lk2, df_subln, w_up_rw, w_up_mla, w_up_df, w_o,
           ln1_g, ln1_b, ln2_g, ln2_b, router_w, router_bias, ex_w_gate, ex_w_up, ex_w_down):
    w = dict(w_in=w_in, shift_prev=shift_prev, shift_next=shift_next, rw_w0=rw_w0, rw_w2=rw_w2,
             rw_a0=rw_a0, rw_a2=rw_a2, rw_g2=rw_g2, rw_k_k=rw_k_k, rw_k_a=rw_k_a, rw_r_k=rw_r_k,
             rw_lnx_g=rw_lnx_g, rw_lnx_b=rw_lnx_b, mla_q_norm=mla_q_norm,
             mla_kv_norm=mla_kv_norm, mla_w_uq=mla_w_uq, mla_w_ukv=mla_w_ukv, df_lq1=df_lq1,
             df_lk1=df_lk1, df_lq2=df_lq2, df_lk2=df_lk2, df_subln=df_subln, w_up_rw=w_up_rw,
             w_up_mla=w_up_mla, w_up_df=w_up_df, w_o=w_o, ln1_g=ln1_g, ln1_b=ln1_b, ln2_g=ln2_g,
             ln2_b=ln2_b, ex_w_gate=ex_w_gate, ex_w_up=ex_w_up, ex_w_down=ex_w_down)
    layers = [_prep_layer(l, w) for l in range(DEPTH)]
    rwt = router_w.T.astype(F32)
    rwt_hi = rwt.astype(BF16)
    ti = jnp.arange(ROUTER_TILE)
    shared = {
        "router_wt_hi": rwt_hi,
        "router_wt_lo": (rwt - rwt_hi.astype(F32)).astype(BF16),
        "router_bias": router_bias.reshape(N_EXPERTS, 1).astype(F32),
        "router_tri": (ti[:, None] < ti[None, :]).astype(BF16),
    }
    return (_trunk(x_prompt, layers, shared), _trunk(x_sample, layers, shared))
```

```python
import functools
import math

import jax
import jax.numpy as jnp
from jax import lax
from jax.experimental import pallas as pl
from jax.experimental.pallas import tpu as pltpu

F32 = jnp.float32
BF16 = jnp.bfloat16

D_MODEL = 2048
DEPTH = 2
RW_HEADS, RW_HEAD_DIM = 12, 64
RW_WIDTH = RW_HEADS * RW_HEAD_DIM
DECAY_LORA, ICLR_LORA, GATE_LORA, N_DIR = 64, 64, 128, 2
GN_EPS = 64e-5
MLA_HEADS, MLA_NOPE, MLA_ROPE, MLA_V = 8, 64, 32, 64
MLA_Q_LORA, MLA_KV_LORA = 512, 256
MLA_WIDTH = MLA_HEADS * MLA_V
DF_HEADS, DF_HEAD_DIM = 6, 64
DF_WIDTH = DF_HEADS * 2 * DF_HEAD_DIM
DF_EPS = 1e-5
N_BRANCH = 3
N_EXPERTS, N_GROUPS, TOP_K, D_EXPERT = 16, 4, 2, 1024
EXPERTS_PER_GROUP = N_EXPERTS // N_GROUPS
ROPE_THETA = 10000.0
LN_EPS = 1e-5
RMS_EPS = 1e-6
ALPHA = (2 * DEPTH) ** 0.25
RW_IN = 3 * RW_WIDTH + N_DIR * DECAY_LORA + N_DIR * ICLR_LORA + GATE_LORA
MLA_IN = MLA_Q_LORA + MLA_KV_LORA + MLA_ROPE
DF_IN = 3 * DF_WIDTH
GATE_IN = N_BRANCH * D_MODEL

LANES = 128
MLA_IN_PAD = 896
MLA_HEAD_PAD = 128
RW_CHUNK = 64
RW_PAIRS = RW_WIDTH // LANES
VMEM_LIMIT = 56 * 1024 * 1024
FFN_TILE = 512
LOG2E = math.log2(math.e)
ROW_SUB = D_MODEL // LANES


def _cparams(sem):
    return pltpu.CompilerParams(dimension_semantics=sem, vmem_limit_bytes=VMEM_LIMIT)


def _dot(a, b):
    return jnp.dot(a, b, preferred_element_type=F32)


def _dot_nt(a, b):
    return lax.dot_general(a, b, (((1,), (1,)), ((), ())), preferred_element_type=F32)


def _dot_tn(a, b):
    return lax.dot_general(a, b, (((0,), (0,)), ((), ())), preferred_element_type=F32)


def _split_bf16(x):
    hi = x.astype(BF16)
    lo = (x - hi.astype(F32)).astype(BF16)
    return hi, lo


def _pick_tile(n, candidates):
    for c in candidates:
        if n % c == 0:
            return c
    raise ValueError(f"no tile for {n}")


def _mm_kernel(x_ref, w_ref, o_ref, *, act):
    acc = _dot(x_ref[...], w_ref[...])
    if act == "sigmoid":
        acc = jax.nn.sigmoid(acc)
    o_ref[...] = acc.astype(o_ref.dtype)


def _matmul(x, w, out_dtype, act=None):
    m, k = x.shape
    n = w.shape[1]
    tm = _pick_tile(m, (1024, 512, 256, 128))
    tn = _pick_tile(n, (1024, 896, 768, 512, 256, 128))
    return pl.pallas_call(
        functools.partial(_mm_kernel, act=act),
        name="mm",
        grid=(n // tn, m // tm),
        in_specs=[pl.BlockSpec((tm, k), lambda j, i: (i, 0)),
                  pl.BlockSpec((k, tn), lambda j, i: (0, j))],
        out_specs=pl.BlockSpec((tm, tn), lambda j, i: (i, j)),
        out_shape=jax.ShapeDtypeStruct((m, n), out_dtype),
        compiler_params=_cparams(("parallel", "parallel")),
    )(x, w)


def _rw_prep_kernel(z_ref, hp_ref, hn_ref, mup_ref, mun_ref, w2_ref, a2_ref, g2_ref, w0_ref,
                    a0_ref, kk_k_ref, k_a_ref, r_k_ref, bd_ref,
                    r_o, v_o, kk_o, g_o, bonus_o, logd_o, a_o, kdir_o):
    z = z_ref[0]
    tr = z.shape[0]
    row = lax.broadcasted_iota(jnp.int32, (tr, 1), 0)
    prev = jnp.where(row == 0, hp_ref[0, 0], pltpu.roll(z, 1, 0))
    nxt = jnp.where(row == tr - 1, hn_ref[0, 0], pltpu.roll(z, tr - 1, 0))
    zs = z + mup_ref[...] * (prev - z) + mun_ref[...] * (nxt - z)
    c = RW_WIDTH
    r, k, v = zs[:, 0:c], zs[:, c:2 * c], zs[:, 2 * c:3 * c]
    wl = zs[:, 3 * c:3 * c + 128]
    al = zs[:, 3 * c + 128:3 * c + 256]
    gl = zs[:, 3 * c + 256:3 * c + 384]
    w_raw = w0_ref[...] + _dot(jnp.tanh(wl).astype(BF16), w2_ref[...])
    a = jax.nn.sigmoid(a0_ref[...] + _dot(al.astype(BF16), a2_ref[...]))
    g = _dot(jax.nn.sigmoid(gl).astype(BF16), g2_ref[...])
    logd = (-math.exp(-0.5)) * jax.nn.sigmoid(w_raw)
    bd = bd_ref[...]

    def head_sum(x):
        hi, lo = _split_bf16(x)
        return _dot(hi, bd) + _dot(lo, bd)

    kk = k * kk_k_ref[...]
    kk = kk * lax.rsqrt(head_sum(kk * kk) + 1e-12)
    k_a = k_a_ref[...]
    kd0 = k * (1.0 + (a[:, 0:c] - 1.0) * k_a)
    kd1 = k * (1.0 + (a[:, c:2 * c] - 1.0) * k_a)
    bonus = head_sum(r * r_k_ref[...] * (kd0 + kd1)) * v
    r_o[0] = r
    v_o[0] = v
    kk_o[0] = kk
    g_o[0] = g
    bonus_o[0] = bonus
    logd_o[0] = logd
    a_o[0] = a
    kdir_o[0, :, 0:c] = kd0
    kdir_o[0, :, c:2 * c] = kd1


def _rw_prep(z_rw, p):
    b, s, _ = z_rw.shape
    tr = 256
    nt = s // tr
    zero = jnp.zeros((b, 1, RW_IN), F32)
    last = z_rw[:, tr - 1::tr]
    first = z_rw[:, 0::tr]
    halo_prev = jnp.concatenate([zero, last[:, :-1]], axis=1).reshape(b, nt, 1, RW_IN)
    halo_next = jnp.concatenate([first[:, 1:], zero], axis=1).reshape(b, nt, 1, RW_IN)
    c = RW_WIDTH
    tile = lambda w: pl.BlockSpec((1, tr, w), lambda bi, ti: (bi, ti, 0))
    halo = pl.BlockSpec((1, 1, 1, RW_IN), lambda bi, ti: (bi, ti, 0, 0))
    full = lambda a: pl.BlockSpec(a.shape, lambda bi, ti: (0,) * a.ndim)
    consts = [p["mu_prev"], p["mu_next"], p["w2cat"], p["a2cat"], p["g2"], p["w0"], p["a0"],
              p["k_k"], p["k_a"], p["r_k"], p["bd"]]
    out_w = [c, c, c, c, c, 2 * c, 2 * c, 2 * c]
    return pl.pallas_call(
        _rw_prep_kernel,
        name="rw_prep",
        grid=(b, nt),
        in_specs=[tile(RW_IN), halo, halo] + [full(a) for a in consts],
        out_specs=[tile(w) for w in out_w],
        out_shape=[jax.ShapeDtypeStruct((b, s, w), F32) for w in out_w],
        compiler_params=_cparams(("parallel", "parallel")),
    )(z_rw, halo_prev, halo_next, *consts)


def _rw_scan_kernel(rf, vf, kkf, ldf, af, kdf, rb, vb, kkb, ldb, ab, kdb, yf_o, yb_o, z_scr):
    @pl.when(pl.program_id(1) == 0)
    def _():
        z_scr[...] = jnp.zeros_like(z_scr)

    c = RW_CHUNK
    ti = lax.broadcasted_iota(jnp.int32, (c, c), 0)
    tj = lax.broadcasted_iota(jnp.int32, (c, c), 1)
    lane = lax.broadcasted_iota(jnp.int32, (1, LANES), 1)
    lane_m = (lane < RW_HEAD_DIM, lane >= RW_HEAD_DIM)
    bi = lax.broadcasted_iota(jnp.int32, (LANES, LANES), 0) // RW_HEAD_DIM
    bj = lax.broadcasted_iota(jnp.int32, (LANES, LANES), 1) // RW_HEAD_DIM
    bdmask = bi == bj
    incl = ((ti >= tj), (ti <= tj))
    strict = ((ti > tj), (ti < tj))
    incl_bf = tuple(m.astype(BF16) for m in incl)
    ones = jnp.ones((c, LANES), BF16)
    in_refs = ((rf, vf, kkf, ldf, af, kdf), (rb, vb, kkb, ldb, ab, kdb))
    out_refs = (yf_o, yb_o)

    probs = [(d, p) for d in range(N_DIR) for p in range(RW_PAIRS)]
    chains = [(s, h) for s in probs for h in range(2)]
    sl = lambda p: slice(p * LANES, (p + 1) * LANES)
    val = {s: [ref[0, :, sl(s[1])] for ref in in_refs[s[0]]] for s in probs}
    ld_split = {s: _split_bf16(val[s][3]) for s in probs}
    logp = {s: _dot(incl_bf[s[0]], ld_split[s][0]) + _dot(incl_bf[s[0]], ld_split[s][1])
            for s in probs}
    logpc = {s: _dot_tn(ld_split[s][0], ones) + _dot_tn(ld_split[s][1], ones)
             for s in probs}
    at, rt, bt, kt, vv, z, zb = {}, {}, {}, {}, {}, {}, {}
    for s in probs:
        r, v, kk, ld, a, kd = val[s]
        pinv = jnp.exp(-logp[s])
        at[s] = -(kk * jnp.exp(logp[s] - ld))
        rt[s] = r * jnp.exp(logp[s])
        bt[s] = (kk * a * pinv).astype(BF16)
        kt[s] = (kd * pinv).astype(BF16)
        vv[s] = v
        z[s] = z_scr[s[0], s[1]]
        zb[s] = z[s].astype(BF16)
    lhs, vh = {}, {}
    for s, h in chains:
        m = lane_m[h]
        lhs[s, h] = jnp.concatenate([jnp.where(m, at[s], 0.0), jnp.where(m, rt[s], 0.0)],
                                    axis=0).astype(BF16)
        vh[s, h] = jnp.where(m, vv[s], 0.0).astype(BF16)
    gb = {ch: _dot_nt(lhs[ch], bt[ch[0]]) for ch in chains}
    gk = {ch: _dot_nt(lhs[ch], kt[ch[0]]) for ch in chains}
    a_ak = {ch: jnp.where(strict[ch[0][0]], gk[ch][0:c], 0.0).astype(BF16) for ch in chains}
    a_rb = {ch: jnp.where(incl[ch[0][0]], gb[ch][c:], 0.0).astype(BF16) for ch in chains}
    a_rk = {ch: jnp.where(incl[ch[0][0]], gk[ch][c:], 0.0).astype(BF16) for ch in chains}
    ap = {ch: jnp.where(strict[ch[0][0]], gb[ch][0:c], 0.0).astype(BF16) for ch in chains}
    x = {ch: _dot(lhs[ch][0:c], zb[ch[0]]) + _dot(a_ak[ch], vh[ch]) for ch in chains}
    for i in range(6):
        x = {ch: x[ch] + _dot(ap[ch], x[ch].astype(BF16)) for ch in chains}
        if i < 5:
            ap = {ch: _dot(ap[ch], ap[ch]).astype(BF16) for ch in chains}
    y = {ch: _dot(lhs[ch][c:], zb[ch[0]]) + _dot(a_rb[ch], x[ch].astype(BF16))
             + _dot(a_rk[ch], vh[ch]) for ch in chains}
    for s in probs:
        d, p = s
        out_refs[d][0, :, sl(p)] = y[s, 0] + y[s, 1]
        u = x[s, 0] + x[s, 1]
        upd = _dot_tn(jnp.concatenate([bt[s], kt[s]], axis=0),
                      jnp.concatenate([u.astype(BF16), vv[s].astype(BF16)], axis=0))
        z_scr[d, p] = jnp.where(bdmask, jnp.exp(logpc[s]) * (z[s] + upd), 0.0)


def _rw_scan(r, v, kk, logd, a, kdir):
    b, s, c = r.shape
    nc = s // RW_CHUNK
    fwd = pl.BlockSpec((1, RW_CHUNK, c), lambda bi, ci: (bi, ci, 0))
    bwd = pl.BlockSpec((1, RW_CHUNK, c), lambda bi, ci: (bi, nc - 1 - ci, 0))
    bwd_dir = pl.BlockSpec((1, RW_CHUNK, c), lambda bi, ci: (bi, nc - 1 - ci, 1))
    return pl.pallas_call(
        _rw_scan_kernel,
        name="rw_scan",
        grid=(b, nc),
        in_specs=[fwd, fwd, fwd, fwd, fwd, fwd, bwd, bwd, bwd, bwd_dir, bwd_dir, bwd_dir],
        out_specs=[fwd, bwd],
        out_shape=[jax.ShapeDtypeStruct((b, s, c), F32)] * 2,
        scratch_shapes=[pltpu.VMEM((N_DIR, RW_PAIRS, LANES, LANES), F32)],
        compiler_params=_cparams(("parallel", "arbitrary")),
    )(r, v, kk, logd, a, kdir, r, v, kk, logd, a, kdir)


def _rw_post_kernel(yf_ref, yb_ref, bonus_ref, g_ref, lg_ref, lb_ref, bd_ref, o_ref):
    y = yf_ref[...] + yb_ref[...]
    bd = bd_ref[...]

    def head_mean(x):
        hi, lo = _split_bf16(x)
        return (_dot(hi, bd) + _dot(lo, bd)) * (1.0 / RW_HEAD_DIM)

    mu = head_mean(y)
    yc = y - mu
    var = head_mean(yc * yc)
    yn = yc * lax.rsqrt(var + GN_EPS) * lg_ref[...] + lb_ref[...]
    o_ref[...] = ((yn + bonus_ref[...]) * g_ref[...]).astype(o_ref.dtype)


def _rw_post(yf, yb, bonus, g, p):
    t, c = yf.shape
    tm = _pick_tile(t, (1024, 512, 256))
    tile = pl.BlockSpec((tm, c), lambda i: (i, 0))
    full = lambda a: pl.BlockSpec(a.shape, lambda i: (0,) * a.ndim)
    consts = [p["lnx_g"], p["lnx_b"], p["bd"]]
    return pl.pallas_call(
        _rw_post_kernel,
        name="rw_post",
        grid=(t // tm,),
        in_specs=[tile] * 4 + [full(a) for a in consts],
        out_specs=tile,
        out_shape=jax.ShapeDtypeStruct((t, c), BF16),
        compiler_params=_cparams(("parallel",)),
    )(yf, yb, bonus, g, *consts)


def _rope_block(x, cos, s_up, s_dn, half):
    return x * cos + pltpu.roll(x, half, 1) * s_up + pltpu.roll(x, LANES - half, 1) * s_dn


def _rope_tables(s, dim, lane_of_x1, period):
    half = dim // 2
    inv_freq = jnp.power(ROPE_THETA, -jnp.arange(half, dtype=F32) * (2.0 / dim))
    ang = jnp.arange(s, dtype=F32)[:, None] * inv_freq[None, :]
    lane = jnp.arange(LANES) % period - lane_of_x1
    in_x1 = (lane >= 0) & (lane < half)
    in_x2 = (lane >= half) & (lane < dim)
    j = jnp.clip(jnp.where(in_x2, lane - half, lane), 0, half - 1)
    cos = jnp.cos(ang)[:, j]
    sin = jnp.sin(ang)[:, j]
    cos_t = jnp.where(in_x1 | in_x2, cos, 1.0)
    s_up = jnp.where(in_x2, sin, 0.0)
    s_dn = jnp.where(in_x1, -sin, 0.0)
    return cos_t.astype(F32), s_up.astype(F32), s_dn.astype(F32)


def _mla_prep_kernel(z_ref, qg_ref, kvg_ref, wq_ref, wk_ref, wv_ref,
                     cq_ref, squ_ref, sqd_ref, ck_ref, sku_ref, skd_ref, q_o, k_o, v_o):
    z = z_ref[0]
    c_q = z[:, 0:MLA_Q_LORA]
    c_kv = z[:, MLA_Q_LORA:MLA_Q_LORA + MLA_KV_LORA]
    kr = z[:, MLA_Q_LORA + MLA_KV_LORA:MLA_IN_PAD]
    c_q = c_q * lax.rsqrt(jnp.mean(c_q * c_q, -1, keepdims=True) + RMS_EPS) * qg_ref[...]
    c_kv = c_kv * lax.rsqrt(jnp.mean(c_kv * c_kv, -1, keepdims=True) + RMS_EPS) * kvg_ref[...]
    q = _dot(c_q.astype(BF16), wq_ref[...])
    scale = (MLA_NOPE + MLA_ROPE) ** -0.5 * LOG2E
    cq, squ, sqd = cq_ref[...], squ_ref[...], sqd_ref[...]
    for h in range(MLA_HEADS):
        sl = slice(h * MLA_HEAD_PAD, (h + 1) * MLA_HEAD_PAD)
        q_o[0, :, sl] = (_rope_block(q[:, sl], cq, squ, sqd, MLA_ROPE // 2) * scale).astype(BF16)
    kr = _rope_block(kr, ck_ref[...], sku_ref[...], skd_ref[...], MLA_ROPE // 2)
    ckv_b = c_kv.astype(BF16)
    k_in = jnp.concatenate([ckv_b, kr.astype(BF16)], axis=1)
    k_o[0] = _dot(k_in, wk_ref[...]).astype(BF16)
    v_o[0] = _dot(ckv_b, wv_ref[...]).astype(BF16)


def _mla_prep(z_mla, p, tabs):
    b, s, _ = z_mla.shape
    tr = 256
    tile = lambda w: pl.BlockSpec((1, tr, w), lambda bi, ti: (bi, ti, 0))
    full = lambda a: pl.BlockSpec(a.shape, lambda bi, ti: (0,) * a.ndim)
    tab = pl.BlockSpec((tr, LANES), lambda bi, ti: (ti, 0))
    consts = [p["q_norm"], p["kv_norm"], p["wq"], p["wk"], p["wv"]]
    hq = MLA_HEADS * MLA_HEAD_PAD
    return pl.pallas_call(
        _mla_prep_kernel,
        name="mla_prep",
        grid=(b, s // tr),
        in_specs=[tile(MLA_IN_PAD)] + [full(a) for a in consts] + [tab] * 6,
        out_specs=[tile(hq), tile(hq), tile(MLA_WIDTH)],
        out_shape=[jax.ShapeDtypeStruct((b, s, hq), BF16), jax.ShapeDtypeStruct((b, s, hq), BF16),
                   jax.ShapeDtypeStruct((b, s, MLA_WIDTH), BF16)],
        compiler_params=_cparams(("parallel", "parallel")),
    )(z_mla, *consts, *tabs["mla_q"], *tabs["mla_k"])


def _mla_attn_kernel(q_ref, k_ref, v_ref, o_ref):
    v = v_ref[0]
    outs = []
    for h in range(2):
        sl = slice(h * MLA_HEAD_PAD, (h + 1) * MLA_HEAD_PAD)
        s = _dot_nt(q_ref[0, :, sl], k_ref[0, :, sl])
        pr = jnp.exp2(s - jnp.max(s, -1, keepdims=True))
        l = jnp.sum(pr, -1, keepdims=True)
        outs.append(_dot(pr.astype(BF16), v) * (1.0 / l))
    lane = lax.broadcasted_iota(jnp.int32, (1, LANES), 1)
    o_ref[0] = jnp.where(lane < MLA_V, outs[0], outs[1]).astype(o_ref.dtype)


def _mla_attn(q, k, v):
    b, s, _ = q.shape
    tq = 256
    npair = MLA_HEADS // 2
    return pl.pallas_call(
        _mla_attn_kernel,
        name="mla_attn",
        grid=(b, npair, s // tq),
        in_specs=[pl.BlockSpec((1, tq, 2 * MLA_HEAD_PAD), lambda bi, pi, qi: (bi, qi, pi)),
                  pl.BlockSpec((1, s, 2 * MLA_HEAD_PAD), lambda bi, pi, qi: (bi, 0, pi)),
                  pl.BlockSpec((1, s, LANES), lambda bi, pi, qi: (bi, 0, pi))],
        out_specs=pl.BlockSpec((1, tq, LANES), lambda bi, pi, qi: (bi, qi, pi)),
        out_shape=jax.ShapeDtypeStruct((b, s, MLA_WIDTH), BF16),
        compiler_params=_cparams(("parallel", "parallel", "parallel")),
    )(q, k, v)


def _df_prep_kernel(z_ref, c_ref, su_ref, sd_ref, q_o, k_o, v_o):
    cos, s_up, s_dn = c_ref[...], su_ref[...], sd_ref[...]
    scale = DF_HEAD_DIM ** -0.5 * LOG2E
    for h in range(DF_HEADS):
        sl = slice(h * LANES, (h + 1) * LANES)
        q_o[0, :, sl] = (_rope_block(z_ref[0, :, sl], cos, s_up, s_dn, DF_HEAD_DIM // 2)
                         * scale).astype(BF16)
        slk = slice(DF_WIDTH + h * LANES, DF_WIDTH + (h + 1) * LANES)
        k_o[0, :, sl] = _rope_block(z_ref[0, :, slk], cos, s_up, s_dn,
                                    DF_HEAD_DIM // 2).astype(BF16)
    v_o[0] = z_ref[0, :, 2 * DF_WIDTH:3 * DF_WIDTH].astype(BF16)


def _df_prep(z_df, tabs):
    b, s, _ = z_df.shape
    tr = 256
    tile = lambda w: pl.BlockSpec((1, tr, w), lambda bi, ti: (bi, ti, 0))
    tab = pl.BlockSpec((tr, LANES), lambda bi, ti: (ti, 0))
    return pl.pallas_call(
        _df_prep_kernel,
        name="df_prep",
        grid=(b, s // tr),
        in_specs=[tile(DF_IN)] + [tab] * 3,
        out_specs=[tile(DF_WIDTH)] * 3,
        out_shape=[jax.ShapeDtypeStruct((b, s, DF_WIDTH), BF16)] * 3,
        compiler_params=_cparams(("parallel", "parallel")),
    )(z_df, *tabs["df"])


def _df_attn_kernel(q_ref, k_ref, v_ref, lq1, lk1, lq2, lk2, g_ref, o_ref, *, lambda_init):
    lam = (jnp.exp(jnp.sum(lq1[...] * lk1[...], -1, keepdims=True))
           - jnp.exp(jnp.sum(lq2[...] * lk2[...], -1, keepdims=True)) + lambda_init)
    q = q_ref[0]
    k = k_ref[0]
    v = v_ref[0]
    lane = lax.broadcasted_iota(jnp.int32, (1, LANES), 1)
    zero = jnp.zeros_like(q)
    outs = []
    for h in range(2):
        m = (lane < DF_HEAD_DIM) if h == 0 else (lane >= DF_HEAD_DIM)
        s = _dot_nt(jnp.where(m, q, zero), k)
        pr = jnp.exp2(s - jnp.max(s, -1, keepdims=True))
        l = jnp.sum(pr, -1, keepdims=True)
        outs.append(_dot(pr.astype(BF16), v) * (1.0 / l))
    o = outs[0] - lam * outs[1]
    o = o * lax.rsqrt(jnp.mean(o * o, -1, keepdims=True) + DF_EPS) * g_ref[...]
    o_ref[0] = (o * (1.0 - lambda_init)).astype(o_ref.dtype)


def _df_attn(q, k, v, p, lambda_init):
    b, s, _ = q.shape
    tq = 256
    full = lambda a: pl.BlockSpec(a.shape, lambda bi, hi, qi: (0,) * a.ndim)
    consts = [p["lq1"], p["lk1"], p["lq2"], p["lk2"], p["subln"]]
    return pl.pallas_call(
        functools.partial(_df_attn_kernel, lambda_init=lambda_init),
        name="df_attn",
        grid=(b, DF_HEADS, s // tq),
        in_specs=[pl.BlockSpec((1, tq, LANES), lambda bi, hi, qi: (bi, qi, hi)),
                  pl.BlockSpec((1, s, LANES), lambda bi, hi, qi: (bi, 0, hi)),
                  pl.BlockSpec((1, s, LANES), lambda bi, hi, qi: (bi, 0, hi))]
                 + [full(a) for a in consts],
        out_specs=pl.BlockSpec((1, tq, LANES), lambda bi, hi, qi: (bi, qi, hi)),
        out_shape=jax.ShapeDtypeStruct((b, s, DF_WIDTH), BF16),
        compiler_params=_cparams(("parallel", "parallel", "parallel")),
    )(q, k, v, *consts)


def _merge_kernel(orw, omla, odf, g0, g1, g2, w0, w1, w2, o_ref):
    acc = g0[...].astype(F32) * _dot(orw[...], w0[...])
    acc = acc + g1[...].astype(F32) * _dot(omla[...], w1[...])
    acc = acc + g2[...].astype(F32) * _dot(odf[...], w2[...])
    o_ref[...] = acc.astype(o_ref.dtype)


def _merge(o_rw, o_mla, o_df, gates, p):
    t = o_rw.shape[0]
    tm = _pick_tile(t, (1024, 512, 256))
    tn = 1024
    nj = D_MODEL // tn
    act = lambda w: pl.BlockSpec((tm, w), lambda i, j: (i, 0))
    gate = lambda br: pl.BlockSpec((tm, tn), lambda i, j: (i, j + br * nj))
    wt = lambda w: pl.BlockSpec((w, tn), lambda i, j: (0, j))
    return pl.pallas_call(
        _merge_kernel,
        name="merge",
        grid=(t // tm, nj),
        in_specs=[act(RW_WIDTH), act(MLA_WIDTH), act(DF_WIDTH), gate(0), gate(1), gate(2),
                  wt(RW_WIDTH), wt(MLA_WIDTH), wt(DF_WIDTH)],
        out_specs=pl.BlockSpec((tm, tn), lambda i, j: (i, j)),
        out_shape=jax.ShapeDtypeStruct((t, D_MODEL), BF16),
        compiler_params=_cparams(("parallel", "parallel")),
    )(o_rw, o_mla, o_df, gates, gates, gates, p["w_up_rw"], p["w_up_mla"], p["w_up_df"])


def _layer_norm(y, g, b):
    mu = jnp.mean(y, -1, keepdims=True)
    yc = y - mu
    var = jnp.mean(yc * yc, -1, keepdims=True)
    return yc * lax.rsqrt(var + LN_EPS) * g + b


def _wo_ln_kernel(m_ref, w_ref, x_ref, g_ref, b_ref, o_ref, ob_ref):
    y = ALPHA * x_ref[...] + _dot(m_ref[...], w_ref[...])
    out = _layer_norm(y, g_ref[...], b_ref[...])
    o_ref[...] = out
    ob_ref[...] = out.astype(BF16).reshape(ob_ref.shape)


def _wo_ln(merged, x, p):
    t = x.shape[0]
    tm = 256
    tile = pl.BlockSpec((tm, D_MODEL), lambda i: (i, 0))
    tile3 = pl.BlockSpec((tm, ROW_SUB, LANES), lambda i: (i, 0, 0))
    full = lambda a: pl.BlockSpec(a.shape, lambda i: (0,) * a.ndim)
    return pl.pallas_call(
        _wo_ln_kernel,
        name="wo_ln",
        grid=(t // tm,),
        in_specs=[tile, full(p["w_o"]), tile, full(p["ln1_g"]), full(p["ln1_b"])],
        out_specs=[tile, tile3],
        out_shape=[jax.ShapeDtypeStruct((t, D_MODEL), F32),
                   jax.ShapeDtypeStruct((t, ROW_SUB, LANES), BF16)],
        compiler_params=_cparams(("parallel",)),
    )(merged, p["w_o"], x, p["ln1_g"], p["ln1_b"])


ROUTER_TILE = 1024


def _router_kernel(x_ref, wh_ref, wl_ref, bias_ref, tri_ref, idx_o, wts_o, rank_o, cnt_o, cnt_scr):
    @pl.when(pl.program_id(0) == 0)
    def _():
        cnt_scr[...] = jnp.zeros_like(cnt_scr)

    xh, xl = _split_bf16(x_ref[...])
    wh, wl = wh_ref[...], wl_ref[...]
    logits = _dot_nt(wh, xh) + (_dot_nt(wh, xl) + _dot_nt(wl, xh))
    scores = jax.nn.sigmoid(logits)
    sel = scores + bias_ref[...]
    tm = sel.shape[1]

    def row(a, i):
        return a[i:i + 1, :]

    best = jnp.zeros((1, tm), jnp.int32)
    best_s = None
    for g in range(N_GROUPS):
        a, b, c, d = (row(sel, EXPERTS_PER_GROUP * g + j) for j in range(4))
        hi1, lo1 = jnp.maximum(a, b), jnp.minimum(a, b)
        hi2, lo2 = jnp.maximum(c, d), jnp.minimum(c, d)
        gs = jnp.maximum(hi1, hi2) + jnp.maximum(jnp.minimum(hi1, hi2), jnp.maximum(lo1, lo2))
        if g == 0:
            best_s = gs
        else:
            upd = gs > best_s
            best = jnp.where(upd, g, best)
            best_s = jnp.where(upd, gs, best_s)

    def pick(a, j):
        out = row(a, j)
        for g in range(1, N_GROUPS):
            out = jnp.where(best == g, row(a, EXPERTS_PER_GROUP * g + j), out)
        return out

    cand = [pick(sel, j) for j in range(EXPERTS_PER_GROUP)]
    csc = [pick(scores, j) for j in range(EXPERTS_PER_GROUP)]
    neg = jnp.float32(-jnp.inf)

    def argmax4(vals):
        bi, bv = jnp.zeros((1, tm), jnp.int32), vals[0]
        for j in range(1, EXPERTS_PER_GROUP):
            upd = vals[j] > bv
            bi = jnp.where(upd, j, bi)
            bv = jnp.where(upd, vals[j], bv)
        return bi

    i1 = argmax4(cand)
    i2 = argmax4([jnp.where(i1 == j, neg, cand[j]) for j in range(EXPERTS_PER_GROUP)])

    def take(vals, i):
        out = vals[0]
        for j in range(1, EXPERTS_PER_GROUP):
            out = jnp.where(i == j, vals[j], out)
        return out

    w1, w2 = take(csc, i1), take(csc, i2)
    tot = w1 + w2
    e1 = best * EXPERTS_PER_GROUP + i1
    e2 = best * EXPERTS_PER_GROUP + i2
    eid = lax.broadcasted_iota(jnp.int32, (N_EXPERTS, tm), 0)
    oh1 = eid == e1
    oh2 = eid == e2
    oh = (oh1 | oh2).astype(BF16)
    before = _dot(oh, tri_ref[...]) + cnt_scr[...][:, 0:1]
    r1 = jnp.sum(jnp.where(oh1, before, 0.0), 0, keepdims=True)
    r2 = jnp.sum(jnp.where(oh2, before, 0.0), 0, keepdims=True)
    idx_o[0:1, :] = e1
    idx_o[1:2, :] = e2
    wts_o[0:1, :] = w1 / tot
    wts_o[1:2, :] = w2 / tot
    rank_o[0:1, :] = r1.astype(jnp.int32)
    rank_o[1:2, :] = r2.astype(jnp.int32)
    new_cnt = cnt_scr[...] + jnp.sum(oh.astype(F32), 1, keepdims=True)
    cnt_scr[...] = new_cnt
    cnt_o[...] = new_cnt


def _router(x, p):
    t = x.shape[0]
    tm = ROUTER_TILE
    full = lambda a: pl.BlockSpec(a.shape, lambda i: (0,) * a.ndim)
    tok = pl.BlockSpec((TOP_K, tm), lambda i: (0, i))
    consts = [p["router_wt_hi"], p["router_wt_lo"], p["router_bias"], p["router_tri"]]
    return pl.pallas_call(
        _router_kernel,
        name="router",
        grid=(t // tm,),
        in_specs=[pl.BlockSpec((tm, D_MODEL), lambda i: (i, 0))] + [full(a) for a in consts],
        out_specs=[tok, tok, tok, pl.BlockSpec((N_EXPERTS, LANES), lambda i: (0, 0))],
        out_shape=[jax.ShapeDtypeStruct((TOP_K, t), jnp.int32), jax.ShapeDtypeStruct((TOP_K, t), F32),
                   jax.ShapeDtypeStruct((TOP_K, t), jnp.int32),
                   jax.ShapeDtypeStruct((N_EXPERTS, LANES), F32)],
        scratch_shapes=[pltpu.VMEM((N_EXPERTS, LANES), F32)],
        compiler_params=_cparams(("arbitrary",)),
    )(x, *consts)


DISPATCH_TILE = 512


def _dispatch_kernel(dest_ref, x_ref, init_ref, xs_ref, sem):
    del init_ref
    def copy(r, k):
        return pltpu.make_async_copy(x_ref.at[r], xs_ref.at[dest_ref[0, TOP_K * r + k]], sem)

    def start(r, carry):
        copy(r, 0).start()
        copy(r, 1).start()
        return carry

    def wait(r, carry):
        copy(r, 0).wait()
        copy(r, 1).wait()
        return carry

    lax.fori_loop(0, DISPATCH_TILE, start, 0)
    lax.fori_loop(0, DISPATCH_TILE, wait, 0)


def _dispatch(xb, dest, n_rows):
    t = xb.shape[0]
    nt = t // DISPATCH_TILE
    dest2 = dest.reshape(nt, 1, DISPATCH_TILE * TOP_K)
    init = jnp.zeros((n_rows, ROW_SUB, LANES), BF16)
    return pl.pallas_call(
        _dispatch_kernel,
        name="dispatch",
        grid=(nt,),
        in_specs=[pl.BlockSpec((None, 1, DISPATCH_TILE * TOP_K), lambda i: (i, 0, 0),
                               memory_space=pltpu.SMEM),
                  pl.BlockSpec((DISPATCH_TILE, ROW_SUB, LANES), lambda i: (i, 0, 0)),
                  pl.BlockSpec(memory_space=pl.ANY)],
        out_specs=pl.BlockSpec(memory_space=pl.ANY),
        out_shape=jax.ShapeDtypeStruct((n_rows, ROW_SUB, LANES), BF16),
        scratch_shapes=[pltpu.SemaphoreType.DMA(())],
        input_output_aliases={2: 0},
        compiler_params=_cparams(("arbitrary",)),
    )(dest2, xb, init)


def _ffn_kernel(te_ref, nu_ref, xs_ref, wg_ref, wu_ref, wd_ref, y_ref):
    del te_ref

    @pl.when(pl.program_id(0) < nu_ref[0])
    def _():
        xs = xs_ref[...].reshape(FFN_TILE, D_MODEL)
        h = jax.nn.silu(_dot(xs, wg_ref[0])) * _dot(xs, wu_ref[0])
        y_ref[...] = _dot(h.astype(BF16), wd_ref[0]).reshape(y_ref.shape)

    @pl.when(pl.program_id(0) >= nu_ref[0])
    def _():
        y_ref[...] = jnp.zeros_like(y_ref)


def _ffn(xs, tile_expert, n_used, p):
    n_rows = xs.shape[0]
    tm = FFN_TILE
    grid_spec = pltpu.PrefetchScalarGridSpec(
        num_scalar_prefetch=2,
        grid=(n_rows // tm,),
        in_specs=[pl.BlockSpec((tm, ROW_SUB, LANES), lambda i, te, nu: (i, 0, 0)),
                  pl.BlockSpec((1, D_MODEL, D_EXPERT), lambda i, te, nu: (te[i], 0, 0)),
                  pl.BlockSpec((1, D_MODEL, D_EXPERT), lambda i, te, nu: (te[i], 0, 0)),
                  pl.BlockSpec((1, D_EXPERT, D_MODEL), lambda i, te, nu: (te[i], 0, 0))],
        out_specs=pl.BlockSpec((tm, ROW_SUB, LANES), lambda i, te, nu: (i, 0, 0)),
    )
    return pl.pallas_call(
        _ffn_kernel,
        name="ffn",
        grid_spec=grid_spec,
        out_shape=jax.ShapeDtypeStruct((n_rows, ROW_SUB, LANES), F32),
        compiler_params=_cparams(("arbitrary",)),
    )(tile_expert, n_used, xs, p["ex_w_gate"], p["ex_w_up"], p["ex_w_down"])


COMBINE_TILE = 256


def _combine_ln_kernel(dest_ref, y_ref, x_ref, w_ref, g_ref, b_ref, o_ref, ob_ref, buf, sem):
    def copy(r, k):
        return pltpu.make_async_copy(y_ref.at[dest_ref[0, TOP_K * r + k]], buf.at[k, r], sem)

    def start(r, carry):
        copy(r, 0).start()
        copy(r, 1).start()
        return carry

    def wait(r, carry):
        copy(r, 0).wait()
        copy(r, 1).wait()
        return carry

    lax.fori_loop(0, COMBINE_TILE, start, 0)
    lax.fori_loop(0, COMBINE_TILE, wait, 0)
    w = w_ref[...]
    shape = (COMBINE_TILE, D_MODEL)
    ffn = w[:, 0:1] * buf[0].reshape(shape) + w[:, 1:2] * buf[1].reshape(shape)
    out = _layer_norm(ALPHA * x_ref[...] + ffn, g_ref[...], b_ref[...])
    o_ref[...] = out
    ob_ref[...] = out.astype(BF16)


def _combine_ln(y, dest, x, wts, p):
    t = x.shape[0]
    tm = COMBINE_TILE
    nt = t // tm
    dest2 = dest.reshape(nt, 1, tm * TOP_K)
    tile = pl.BlockSpec((tm, D_MODEL), lambda i: (i, 0))
    full = lambda a: pl.BlockSpec(a.shape, lambda i: (0,) * a.ndim)
    return pl.pallas_call(
        _combine_ln_kernel,
        name="combine_ln",
        grid=(nt,),
        in_specs=[pl.BlockSpec((None, 1, tm * TOP_K), lambda i: (i, 0, 0), memory_space=pltpu.SMEM),
                  pl.BlockSpec(memory_space=pl.ANY), tile,
                  pl.BlockSpec((tm, TOP_K), lambda i: (i, 0)), full(p["ln2_g"]), full(p["ln2_b"])],
        out_specs=[tile, tile],
        out_shape=[jax.ShapeDtypeStruct((t, D_MODEL), F32), jax.ShapeDtypeStruct((t, D_MODEL), BF16)],
        scratch_shapes=[pltpu.VMEM((TOP_K, tm, ROW_SUB, LANES), F32), pltpu.SemaphoreType.DMA(())],
        compiler_params=_cparams(("arbitrary",)),
    )(dest2, y, x, wts, p["ln2_g"], p["ln2_b"])


def _moe_ln(x, xb, p):
    t = x.shape[0]
    idx, wts, rank, counts = _router(x, p)
    counts = counts[:, 0].astype(jnp.int32)
    tiles = (counts + FFN_TILE - 1) // FFN_TILE
    tile_end = jnp.cumsum(tiles)
    row_start = (tile_end - tiles) * FFN_TILE
    n_tiles = (t * TOP_K) // FFN_TILE + N_EXPERTS
    start_of = jnp.sum(jnp.where(idx[..., None] == jnp.arange(N_EXPERTS), row_start, 0), -1)
    dest = (start_of + rank).T.reshape(-1)
    tile_expert = jnp.minimum(jnp.searchsorted(tile_end, jnp.arange(n_tiles), side="right"),
                              N_EXPERTS - 1).astype(jnp.int32)
    n_used = tile_end[-1:].astype(jnp.int32)
    xs = _dispatch(xb, dest, n_tiles * FFN_TILE)
    y = _ffn(xs, tile_expert, n_used, p)
    return _combine_ln(y, dest, x, wts.T, p)


def _block_diag_ones(n, blk):
    i = jnp.arange(n) // blk
    return (i[:, None] == i[None, :]).astype(BF16)


def _prep_layer(l, w):
    row = lambda a: a.reshape(1, -1).astype(F32)
    w_in = w["w_in"][l]
    o1, o2, o3 = RW_IN, RW_IN + MLA_IN, RW_IN + MLA_IN + DF_IN
    p = {
        "w_in_rw": w_in[:, :o1].astype(BF16),
        "w_in_mla": jnp.pad(w_in[:, o1:o2], ((0, 0), (0, MLA_IN_PAD - MLA_IN))).astype(BF16),
        "w_in_df": w_in[:, o2:o3].astype(BF16),
        "w_in_gate": w_in[:, o3:].astype(BF16),
        "mu_prev": row(w["shift_prev"][l]), "mu_next": row(w["shift_next"][l]),
        "w0": row(w["rw_w0"][l]), "a0": row(w["rw_a0"][l]),
        "g2": w["rw_g2"][l].astype(BF16),
        "k_k": row(w["rw_k_k"][l]), "k_a": row(w["rw_k_a"][l]), "r_k": row(w["rw_r_k"][l]),
        "lnx_g": row(w["rw_lnx_g"][l]), "lnx_b": row(w["rw_lnx_b"][l]),
        "bd": _block_diag_ones(RW_WIDTH, RW_HEAD_DIM),
        "q_norm": row(w["mla_q_norm"][l]), "kv_norm": row(w["mla_kv_norm"][l]),
        "lq1": row(w["df_lq1"][l]), "lk1": row(w["df_lk1"][l]),
        "lq2": row(w["df_lq2"][l]), "lk2": row(w["df_lk2"][l]),
        "subln": row(w["df_subln"][l]),
        "w_up_rw": w["w_up_rw"][l].astype(BF16), "w_up_mla": w["w_up_mla"][l].astype(BF16),
        "w_up_df": w["w_up_df"][l].astype(BF16), "w_o": w["w_o"][l].astype(BF16),
        "ln1_g": row(w["ln1_g"][l]), "ln1_b": row(w["ln1_b"][l]),
        "ln2_g": row(w["ln2_g"][l]), "ln2_b": row(w["ln2_b"][l]),
        "ex_w_gate": w["ex_w_gate"][l].astype(BF16), "ex_w_up": w["ex_w_up"][l].astype(BF16),
        "ex_w_down": w["ex_w_down"][l].astype(BF16),
    }
    zc = jnp.zeros((DECAY_LORA, RW_WIDTH), F32)
    w2 = w["rw_w2"][l]
    a2 = w["rw_a2"][l]
    p["w2cat"] = jnp.block([[w2[0], zc], [zc, w2[1]]]).astype(BF16)
    p["a2cat"] = jnp.block([[a2[0], zc], [zc, a2[1]]]).astype(BF16)
    wq = w["mla_w_uq"][l].reshape(MLA_Q_LORA, MLA_HEADS, MLA_NOPE + MLA_ROPE)
    p["wq"] = jnp.pad(wq, ((0, 0), (0, 0), (0, MLA_HEAD_PAD - MLA_NOPE - MLA_ROPE))
                      ).reshape(MLA_Q_LORA, -1).astype(BF16)
    wkv = w["mla_w_ukv"][l].reshape(MLA_KV_LORA, MLA_HEADS, MLA_NOPE + MLA_V)
    wk_nope = jnp.pad(wkv[:, :, :MLA_NOPE], ((0, 0), (0, 0), (0, MLA_HEAD_PAD - MLA_NOPE)))
    place = jnp.zeros((LANES, MLA_HEADS, MLA_HEAD_PAD), F32)
    j = jnp.arange(MLA_ROPE)
    place = place.at[j, :, MLA_NOPE + j].set(1.0)
    p["wk"] = jnp.concatenate([wk_nope, place], axis=0).reshape(MLA_KV_LORA + LANES, -1).astype(BF16)
    p["wv"] = wkv[:, :, MLA_NOPE:].reshape(MLA_KV_LORA, -1).astype(BF16)
    return p


def _trunk(x3, layers, shared):
    b, s, d = x3.shape
    t = b * s
    tabs = {
        "mla_q": _rope_tables(s, MLA_ROPE, MLA_NOPE, LANES),
        "mla_k": _rope_tables(s, MLA_ROPE, 0, LANES),
        "df": _rope_tables(s, DF_HEAD_DIM, 0, DF_HEAD_DIM),
    }
    x = x3.reshape(t, d)
    xb = x.astype(BF16)
    for l, p in enumerate(layers):
        p = dict(p, **shared)
        lambda_init = 0.8 - 0.6 * math.exp(-0.3 * l)
        z_rw = _matmul(xb, p["w_in_rw"], F32).reshape(b, s, -1)
        z_mla = _matmul(xb, p["w_in_mla"], F32).reshape(b, s, -1)
        z_df = _matmul(xb, p["w_in_df"], F32).reshape(b, s, -1)
        gates = _matmul(xb, p["w_in_gate"], BF16, act="sigmoid")
        r, v, kk, g, bonus, logd, a, kdir = _rw_prep(z_rw, p)
        yf, yb = _rw_scan(r, v, kk, logd, a, kdir)
        o_rw = _rw_post(yf.reshape(t, -1), yb.reshape(t, -1), bonus.reshape(t, -1),
                        g.reshape(t, -1), p)
        q, k, v2 = _mla_prep(z_mla, p, tabs)
        o_mla = _mla_attn(q, k, v2).reshape(t, -1)
        q, k, v2 = _df_prep(z_df, tabs)
        o_df = _df_attn(q, k, v2, p, lambda_init).reshape(t, -1)
        merged = _merge(o_rw, o_mla, o_df, gates, p)
        x, xb = _wo_ln(merged, x, p)
        x, xb = _moe_ln(x, xb, p)
    return x.reshape(b, s, d)


def kernel(x_prompt, x_sample, w_in, shift_prev, shift_next, rw_w0, rw_w2, rw_a0, rw_a2, rw_g2,
           rw_k_k, rw_k_a, rw_r_k, rw_lnx_g, rw_lnx_b, mla_q_norm, mla_kv_norm, mla_w_uq,
           mla_w_ukv, df_lq1, df_lk1, df_lq2, df_lk2, df_subln, w_up_rw, w_up_mla, w_up_df, w_o,
           ln1_g, ln1_b, ln2_g, ln2_b, router_w, router_bias, ex_w_gate, ex_w_up, ex_w_down):
    w = dict(w_in=w_in, shift_prev=shift_prev, shift_next=shift_next, rw_w0=rw_w0, rw_w2=rw_w2,
             rw_a0=rw_a0, rw_a2=rw_a2, rw_g2=rw_g2, rw_k_k=rw_k_k, rw_k_a=rw_k_a, rw_r_k=rw_r_k,
             rw_lnx_g=rw_lnx_g, rw_lnx_b=rw_lnx_b, mla_q_norm=mla_q_norm,
             mla_kv_norm=mla_kv_norm, mla_w_uq=mla_w_uq, mla_w_ukv=mla_w_ukv, df_lq1=df_lq1,
             df_lk1=df_lk1, df_lq2=df_lq2, df_lk2=df_lk2, df_subln=df_subln, w_up_rw=w_up_rw,
             w_up_mla=w_up_mla, w_up_df=w_up_df, w_o=w_o, ln1_g=ln1_g, ln1_b=ln1_b, ln2_g=ln2_g,
             ln2_b=ln2_b, ex_w_gate=ex_w_gate, ex_w_up=ex_w_up, ex_w_down=ex_w_down)
    layers = [_prep_layer(l, w) for l in range(DEPTH)]
    rwt = router_w.T.astype(F32)
    rwt_hi = rwt.astype(BF16)
    ti = jnp.arange(ROUTER_TILE)
    shared = {
        "router_wt_hi": rwt_hi,
        "router_wt_lo": (rwt - rwt_hi.astype(F32)).astype(BF16),
        "router_bias": router_bias.reshape(N_EXPERTS, 1).astype(F32),
        "router_tri": (ti[:, None] < ti[None, :]).astype(BF16),
    }
    return (_trunk(x_prompt, layers, shared), _trunk(x_sample, layers, shared))
```

```python
import functools
import math

import jax
import jax.numpy as jnp
from jax import lax
from jax.experimental import pallas as pl
from jax.experimental.pallas import tpu as pltpu

F32 = jnp.float32
BF16 = jnp.bfloat16

D_MODEL = 2048
DEPTH = 2
RW_HEADS, RW_HEAD_DIM = 12, 64
RW_WIDTH = RW_HEADS * RW_HEAD_DIM
DECAY_LORA, ICLR_LORA, GATE_LORA, N_DIR = 64, 64, 128, 2
GN_EPS = 64e-5
MLA_HEADS, MLA_NOPE, MLA_ROPE, MLA_V = 8, 64, 32, 64
MLA_Q_LORA, MLA_KV_LORA = 512, 256
MLA_WIDTH = MLA_HEADS * MLA_V
DF_HEADS, DF_HEAD_DIM = 6, 64
DF_WIDTH = DF_HEADS * 2 * DF_HEAD_DIM
DF_EPS = 1e-5
N_BRANCH = 3
N_EXPERTS, N_GROUPS, TOP_K, D_EXPERT = 16, 4, 2, 1024
EXPERTS_PER_GROUP = N_EXPERTS // N_GROUPS
ROPE_THETA = 10000.0
LN_EPS = 1e-5
RMS_EPS = 1e-6
ALPHA = (2 * DEPTH) ** 0.25
RW_IN = 3 * RW_WIDTH + N_DIR * DECAY_LORA + N_DIR * ICLR_LORA + GATE_LORA
MLA_IN = MLA_Q_LORA + MLA_KV_LORA + MLA_ROPE
DF_IN = 3 * DF_WIDTH
GATE_IN = N_BRANCH * D_MODEL

LANES = 128
MLA_IN_PAD = 896
MLA_HEAD_PAD = 128
RW_CHUNK = 64
RW_PAIRS = RW_WIDTH // LANES
VMEM_LIMIT = 56 * 1024 * 1024
FFN_TILE = 512
LOG2E = math.log2(math.e)
DMA_UNROLL = 8
HALO = 8
ROW_SUB = D_MODEL // LANES


def _cparams(sem):
    return pltpu.CompilerParams(dimension_semantics=sem, vmem_limit_bytes=VMEM_LIMIT)


def _dot(a, b):
    return jnp.dot(a, b, preferred_element_type=F32)


def _dot_nt(a, b):
    return lax.dot_general(a, b, (((1,), (1,)), ((), ())), preferred_element_type=F32)


def _dot_tn(a, b):
    return lax.dot_general(a, b, (((0,), (0,)), ((), ())), preferred_element_type=F32)


def _split_bf16(x):
    hi = x.astype(BF16)
    lo = (x - hi.astype(F32)).astype(BF16)
    return hi, lo


def _pick_tile(n, candidates):
    for c in candidates:
        if n % c == 0:
            return c
    raise ValueError(f"no tile for {n}")


def _mm_kernel(x_ref, w_ref, o_ref, *, act):
    acc = _dot(x_ref[...], w_ref[...])
    if act == "sigmoid":
        acc = jax.nn.sigmoid(acc)
    o_ref[...] = acc.astype(o_ref.dtype)


def _matmul(x, w, out_dtype, act=None):
    m, k = x.shape
    n = w.shape[1]
    tm = _pick_tile(m, (1024, 512, 256, 128))
    tn = _pick_tile(n, (1024, 896, 768, 512, 256, 128))
    return pl.pallas_call(
        functools.partial(_mm_kernel, act=act),
        name="mm",
        grid=(n // tn, m // tm),
        in_specs=[pl.BlockSpec((tm, k), lambda j, i: (i, 0)),
                  pl.BlockSpec((k, tn), lambda j, i: (0, j))],
        out_specs=pl.BlockSpec((tm, tn), lambda j, i: (i, j)),
        out_shape=jax.ShapeDtypeStruct((m, n), out_dtype),
        compiler_params=_cparams(("parallel", "parallel")),
    )(x, w)


def _rw_prep_kernel(z_ref, hp_ref, hn_ref, mup_ref, mun_ref, w2_ref, a2_ref, g2_ref, w0_ref,
                    a0_ref, kk_k_ref, k_a_ref, r_k_ref, bd_ref,
                    r_o, v_o, kk_o, g_o, bonus_o, logd_o, a_o, kdir_o):
    z = z_ref[0]
    tr = z.shape[0]
    row = lax.broadcasted_iota(jnp.int32, (tr, 1), 0)
    ti = pl.program_id(1)
    halo_prev = jnp.where(ti == 0, 0.0, hp_ref[0, HALO - 1:HALO, :])
    halo_next = jnp.where(ti == pl.num_programs(1) - 1, 0.0, hn_ref[0, 0:1, :])
    prev = jnp.where(row == 0, halo_prev, pltpu.roll(z, 1, 0))
    nxt = jnp.where(row == tr - 1, halo_next, pltpu.roll(z, tr - 1, 0))
    zs = z + mup_ref[...] * (prev - z) + mun_ref[...] * (nxt - z)
    c = RW_WIDTH
    r, k, v = zs[:, 0:c], zs[:, c:2 * c], zs[:, 2 * c:3 * c]
    wl = zs[:, 3 * c:3 * c + 128]
    al = zs[:, 3 * c + 128:3 * c + 256]
    gl = zs[:, 3 * c + 256:3 * c + 384]
    w_raw = w0_ref[...] + _dot(jnp.tanh(wl).astype(BF16), w2_ref[...])
    a = jax.nn.sigmoid(a0_ref[...] + _dot(al.astype(BF16), a2_ref[...]))
    g = _dot(jax.nn.sigmoid(gl).astype(BF16), g2_ref[...])
    logd = (-math.exp(-0.5)) * jax.nn.sigmoid(w_raw)
    bd = bd_ref[...]

    def head_sum(x):
        hi, lo = _split_bf16(x)
        return _dot(hi, bd) + _dot(lo, bd)

    kk = k * kk_k_ref[...]
    kk = kk * lax.rsqrt(head_sum(kk * kk) + 1e-12)
    k_a = k_a_ref[...]
    kd0 = k * (1.0 + (a[:, 0:c] - 1.0) * k_a)
    kd1 = k * (1.0 + (a[:, c:2 * c] - 1.0) * k_a)
    bonus = head_sum(r * r_k_ref[...] * (kd0 + kd1)) * v
    r_o[0] = r
    v_o[0] = v
    kk_o[0] = kk
    g_o[0] = g
    bonus_o[0] = bonus
    logd_o[0] = logd
    a_o[0] = a
    kdir_o[0, :, 0:c] = kd0
    kdir_o[0, :, c:2 * c] = kd1


def _rw_prep(z_rw, p):
    b, s, _ = z_rw.shape
    tr = 256
    nt = s // tr
    hpt = tr // HALO
    c = RW_WIDTH
    tile = lambda w: pl.BlockSpec((1, tr, w), lambda bi, ti: (bi, ti, 0))
    halo_prev = pl.BlockSpec((1, HALO, RW_IN), lambda bi, ti: (bi, jnp.maximum(ti * hpt - 1, 0), 0))
    halo_next = pl.BlockSpec((1, HALO, RW_IN),
                             lambda bi, ti: (bi, jnp.minimum((ti + 1) * hpt, s // HALO - 1), 0))
    full = lambda a: pl.BlockSpec(a.shape, lambda bi, ti: (0,) * a.ndim)
    consts = [p["mu_prev"], p["mu_next"], p["w2cat"], p["a2cat"], p["g2"], p["w0"], p["a0"],
              p["k_k"], p["k_a"], p["r_k"], p["bd"]]
    out_w = [c, c, c, c, c, 2 * c, 2 * c, 2 * c]
    return pl.pallas_call(
        _rw_prep_kernel,
        name="rw_prep",
        grid=(b, nt),
        in_specs=[tile(RW_IN), halo_prev, halo_next] + [full(a) for a in consts],
        out_specs=[tile(w) for w in out_w],
        out_shape=[jax.ShapeDtypeStruct((b, s, w), F32) for w in out_w],
        compiler_params=_cparams(("parallel", "parallel")),
    )(z_rw, z_rw, z_rw, *consts)


def _rw_scan_kernel(rf, vf, kkf, ldf, af, kdf, rb, vb, kkb, ldb, ab, kdb, yf_o, yb_o, z_scr):
    @pl.when(pl.program_id(1) == 0)
    def _():
        z_scr[...] = jnp.zeros_like(z_scr)

    c = RW_CHUNK
    ti = lax.broadcasted_iota(jnp.int32, (c, c), 0)
    tj = lax.broadcasted_iota(jnp.int32, (c, c), 1)
    lane = lax.broadcasted_iota(jnp.int32, (1, LANES), 1)
    lane_m = (lane < RW_HEAD_DIM, lane >= RW_HEAD_DIM)
    bi = lax.broadcasted_iota(jnp.int32, (LANES, LANES), 0) // RW_HEAD_DIM
    bj = lax.broadcasted_iota(jnp.int32, (LANES, LANES), 1) // RW_HEAD_DIM
    bdmask = bi == bj
    incl = ((ti >= tj), (ti <= tj))
    strict = ((ti > tj), (ti < tj))
    incl_bf = tuple(m.astype(BF16) for m in incl)
    ones = jnp.ones((c, LANES), BF16)
    in_refs = ((rf, vf, kkf, ldf, af, kdf), (rb, vb, kkb, ldb, ab, kdb))
    out_refs = (yf_o, yb_o)

    probs = [(d, p) for d in range(N_DIR) for p in range(RW_PAIRS)]
    chains = [(s, h) for s in probs for h in range(2)]
    sl = lambda p: slice(p * LANES, (p + 1) * LANES)
    val = {s: [ref[0, :, sl(s[1])] for ref in in_refs[s[0]]] for s in probs}
    ld_split = {s: _split_bf16(val[s][3]) for s in probs}
    logp = {s: _dot(incl_bf[s[0]], ld_split[s][0]) + _dot(incl_bf[s[0]], ld_split[s][1])
            for s in probs}
    logpc = {s: _dot_tn(ld_split[s][0], ones) + _dot_tn(ld_split[s][1], ones)
             for s in probs}
    at, rt, bt, kt, vv, z, zb = {}, {}, {}, {}, {}, {}, {}
    for s in probs:
        r, v, kk, ld, a, kd = val[s]
        pinv = jnp.exp(-logp[s])
        at[s] = -(kk * jnp.exp(logp[s] - ld))
        rt[s] = r * jnp.exp(logp[s])
        bt[s] = (kk * a * pinv).astype(BF16)
        kt[s] = (kd * pinv).astype(BF16)
        vv[s] = v
        z[s] = z_scr[s[0], s[1]]
        zb[s] = z[s].astype(BF16)
    lhs, vh = {}, {}
    for s, h in chains:
        m = lane_m[h]
        lhs[s, h] = jnp.concatenate([jnp.where(m, at[s], 0.0), jnp.where(m, rt[s], 0.0)],
                                    axis=0).astype(BF16)
        vh[s, h] = jnp.where(m, vv[s], 0.0).astype(BF16)
    gb = {ch: _dot_nt(lhs[ch], bt[ch[0]]) for ch in chains}
    gk = {ch: _dot_nt(lhs[ch], kt[ch[0]]) for ch in chains}
    a_ak = {ch: jnp.where(strict[ch[0][0]], gk[ch][0:c], 0.0).astype(BF16) for ch in chains}
    a_rb = {ch: jnp.where(incl[ch[0][0]], gb[ch][c:], 0.0).astype(BF16) for ch in chains}
    a_rk = {ch: jnp.where(incl[ch[0][0]], gk[ch][c:], 0.0).astype(BF16) for ch in chains}
    ap = {ch: jnp.where(strict[ch[0][0]], gb[ch][0:c], 0.0).astype(BF16) for ch in chains}
    x = {ch: _dot(lhs[ch][0:c], zb[ch[0]]) + _dot(a_ak[ch], vh[ch]) for ch in chains}
    for i in range(6):
        x = {ch: x[ch] + _dot(ap[ch], x[ch].astype(BF16)) for ch in chains}
        if i < 5:
            ap = {ch: _dot(ap[ch], ap[ch]).astype(BF16) for ch in chains}
    y = {ch: _dot(lhs[ch][c:], zb[ch[0]]) + _dot(a_rb[ch], x[ch].astype(BF16))
             + _dot(a_rk[ch], vh[ch]) for ch in chains}
    for s in probs:
        d, p = s
        out_refs[d][0, :, sl(p)] = y[s, 0] + y[s, 1]
        u = x[s, 0] + x[s, 1]
        upd = _dot_tn(jnp.concatenate([bt[s], kt[s]], axis=0),
                      jnp.concatenate([u.astype(BF16), vv[s].astype(BF16)], axis=0))
        z_scr[d, p] = jnp.where(bdmask, jnp.exp(logpc[s]) * (z[s] + upd), 0.0)


def _rw_scan(r, v, kk, logd, a, kdir):
    b, s, c = r.shape
    nc = s // RW_CHUNK
    fwd = pl.BlockSpec((1, RW_CHUNK, c), lambda bi, ci: (bi, ci, 0))
    bwd = pl.BlockSpec((1, RW_CHUNK, c), lambda bi, ci: (bi, nc - 1 - ci, 0))
    bwd_dir = pl.BlockSpec((1, RW_CHUNK, c), lambda bi, ci: (bi, nc - 1 - ci, 1))
    return pl.pallas_call(
        _rw_scan_kernel,
        name="rw_scan",
        grid=(b, nc),
        in_specs=[fwd, fwd, fwd, fwd, fwd, fwd, bwd, bwd, bwd, bwd_dir, bwd_dir, bwd_dir],
        out_specs=[fwd, bwd],
        out_shape=[jax.ShapeDtypeStruct((b, s, c), F32)] * 2,
        scratch_shapes=[pltpu.VMEM((N_DIR, RW_PAIRS, LANES, LANES), F32)],
        compiler_params=_cparams(("parallel", "arbitrary")),
    )(r, v, kk, logd, a, kdir, r, v, kk, logd, a, kdir)


def _rw_post_kernel(yf_ref, yb_ref, bonus_ref, g_ref, lg_ref, lb_ref, bd_ref, o_ref):
    y = yf_ref[...] + yb_ref[...]
    bd = bd_ref[...]

    def head_mean(x):
        hi, lo = _split_bf16(x)
        return (_dot(hi, bd) + _dot(lo, bd)) * (1.0 / RW_HEAD_DIM)

    mu = head_mean(y)
    yc = y - mu
    var = head_mean(yc * yc)
    yn = yc * lax.rsqrt(var + GN_EPS) * lg_ref[...] + lb_ref[...]
    o_ref[...] = ((yn + bonus_ref[...]) * g_ref[...]).astype(o_ref.dtype)


def _rw_post(yf, yb, bonus, g, p):
    t, c = yf.shape
    tm = _pick_tile(t, (1024, 512, 256))
    tile = pl.BlockSpec((tm, c), lambda i: (i, 0))
    full = lambda a: pl.BlockSpec(a.shape, lambda i: (0,) * a.ndim)
    consts = [p["lnx_g"], p["lnx_b"], p["bd"]]
    return pl.pallas_call(
        _rw_post_kernel,
        name="rw_post",
        grid=(t // tm,),
        in_specs=[tile] * 4 + [full(a) for a in consts],
        out_specs=tile,
        out_shape=jax.ShapeDtypeStruct((t, c), BF16),
        compiler_params=_cparams(("parallel",)),
    )(yf, yb, bonus, g, *consts)


def _rope_block(x, cos, s_up, s_dn, half):
    return x * cos + pltpu.roll(x, half, 1) * s_up + pltpu.roll(x, LANES - half, 1) * s_dn


def _rope_tables(s, dim, lane_of_x1, period):
    half = dim // 2
    inv_freq = jnp.power(ROPE_THETA, -jnp.arange(half, dtype=F32) * (2.0 / dim))
    ang = jnp.arange(s, dtype=F32)[:, None] * inv_freq[None, :]
    lane = jnp.arange(LANES) % period - lane_of_x1
    in_x1 = (lane >= 0) & (lane < half)
    in_x2 = (lane >= half) & (lane < dim)
    j = jnp.clip(jnp.where(in_x2, lane - half, lane), 0, half - 1)
    cos = jnp.cos(ang)[:, j]
    sin = jnp.sin(ang)[:, j]
    cos_t = jnp.where(in_x1 | in_x2, cos, 1.0)
    s_up = jnp.where(in_x2, sin, 0.0)
    s_dn = jnp.where(in_x1, -sin, 0.0)
    return cos_t.astype(F32), s_up.astype(F32), s_dn.astype(F32)


def _mla_prep_kernel(z_ref, qg_ref, kvg_ref, wq_ref, wk_ref, wv_ref,
                     cq_ref, squ_ref, sqd_ref, ck_ref, sku_ref, skd_ref, q_o, k_o, v_o):
    z = z_ref[0]
    c_q = z[:, 0:MLA_Q_LORA]
    c_kv = z[:, MLA_Q_LORA:MLA_Q_LORA + MLA_KV_LORA]
    kr = z[:, MLA_Q_LORA + MLA_KV_LORA:MLA_IN_PAD]
    c_q = c_q * lax.rsqrt(jnp.mean(c_q * c_q, -1, keepdims=True) + RMS_EPS) * qg_ref[...]
    c_kv = c_kv * lax.rsqrt(jnp.mean(c_kv * c_kv, -1, keepdims=True) + RMS_EPS) * kvg_ref[...]
    q = _dot(c_q.astype(BF16), wq_ref[...])
    scale = (MLA_NOPE + MLA_ROPE) ** -0.5 * LOG2E
    cq, squ, sqd = cq_ref[...], squ_ref[...], sqd_ref[...]
    for h in range(MLA_HEADS):
        sl = slice(h * MLA_HEAD_PAD, (h + 1) * MLA_HEAD_PAD)
        q_o[0, :, sl] = (_rope_block(q[:, sl], cq, squ, sqd, MLA_ROPE // 2) * scale).astype(BF16)
    kr = _rope_block(kr, ck_ref[...], sku_ref[...], skd_ref[...], MLA_ROPE // 2)
    ckv_b = c_kv.astype(BF16)
    k_in = jnp.concatenate([ckv_b, kr.astype(BF16)], axis=1)
    k_o[0] = _dot(k_in, wk_ref[...]).astype(BF16)
    v_o[0] = _dot(ckv_b, wv_ref[...]).astype(BF16)


def _mla_prep(z_mla, p, tabs):
    b, s, _ = z_mla.shape
    tr = 256
    tile = lambda w: pl.BlockSpec((1, tr, w), lambda bi, ti: (bi, ti, 0))
    full = lambda a: pl.BlockSpec(a.shape, lambda bi, ti: (0,) * a.ndim)
    tab = pl.BlockSpec((tr, LANES), lambda bi, ti: (ti, 0))
    consts = [p["q_norm"], p["kv_norm"], p["wq"], p["wk"], p["wv"]]
    hq = MLA_HEADS * MLA_HEAD_PAD
    return pl.pallas_call(
        _mla_prep_kernel,
        name="mla_prep",
        grid=(b, s // tr),
        in_specs=[tile(MLA_IN_PAD)] + [full(a) for a in consts] + [tab] * 6,
        out_specs=[tile(hq), tile(hq), tile(MLA_WIDTH)],
        out_shape=[jax.ShapeDtypeStruct((b, s, hq), BF16), jax.ShapeDtypeStruct((b, s, hq), BF16),
                   jax.ShapeDtypeStruct((b, s, MLA_WIDTH), BF16)],
        compiler_params=_cparams(("parallel", "parallel")),
    )(z_mla, *consts, *tabs["mla_q"], *tabs["mla_k"])


def _mla_attn_kernel(q_ref, k_ref, v_ref, o_ref):
    v = v_ref[0]
    sl = lambda h: slice(h * MLA_HEAD_PAD, (h + 1) * MLA_HEAD_PAD)
    scores = [_dot_nt(q_ref[0, :, sl(h)], k_ref[0, :, sl(h)]) for h in range(2)]
    outs = []
    for s in scores:
        pr = jnp.exp2(s - jnp.max(s, -1, keepdims=True))
        l = jnp.sum(pr, -1, keepdims=True)
        outs.append(_dot(pr.astype(BF16), v) * (1.0 / l))
    lane = lax.broadcasted_iota(jnp.int32, (1, LANES), 1)
    o_ref[0] = jnp.where(lane < MLA_V, outs[0], outs[1]).astype(o_ref.dtype)


def _mla_attn(q, k, v):
    b, s, _ = q.shape
    tq = 256
    npair = MLA_HEADS // 2
    return pl.pallas_call(
        _mla_attn_kernel,
        name="mla_attn",
        grid=(b, npair, s // tq),
        in_specs=[pl.BlockSpec((1, tq, 2 * MLA_HEAD_PAD), lambda bi, pi, qi: (bi, qi, pi)),
                  pl.BlockSpec((1, s, 2 * MLA_HEAD_PAD), lambda bi, pi, qi: (bi, 0, pi)),
                  pl.BlockSpec((1, s, LANES), lambda bi, pi, qi: (bi, 0, pi))],
        out_specs=pl.BlockSpec((1, tq, LANES), lambda bi, pi, qi: (bi, qi, pi)),
        out_shape=jax.ShapeDtypeStruct((b, s, MLA_WIDTH), BF16),
        compiler_params=_cparams(("parallel", "parallel", "parallel")),
    )(q, k, v)


def _df_prep_kernel(z_ref, c_ref, su_ref, sd_ref, q_o, k_o, v_o):
    cos, s_up, s_dn = c_ref[...], su_ref[...], sd_ref[...]
    scale = DF_HEAD_DIM ** -0.5 * LOG2E
    for h in range(DF_HEADS):
        sl = slice(h * LANES, (h + 1) * LANES)
        q_o[0, :, sl] = (_rope_block(z_ref[0, :, sl], cos, s_up, s_dn, DF_HEAD_DIM // 2)
                         * scale).astype(BF16)
        slk = slice(DF_WIDTH + h * LANES, DF_WIDTH + (h + 1) * LANES)
        k_o[0, :, sl] = _rope_block(z_ref[0, :, slk], cos, s_up, s_dn,
                                    DF_HEAD_DIM // 2).astype(BF16)
    v_o[0] = z_ref[0, :, 2 * DF_WIDTH:3 * DF_WIDTH].astype(BF16)


def _df_prep(z_df, tabs):
    b, s, _ = z_df.shape
    tr = 256
    tile = lambda w: pl.BlockSpec((1, tr, w), lambda bi, ti: (bi, ti, 0))
    tab = pl.BlockSpec((tr, LANES), lambda bi, ti: (ti, 0))
    return pl.pallas_call(
        _df_prep_kernel,
        name="df_prep",
        grid=(b, s // tr),
        in_specs=[tile(DF_IN)] + [tab] * 3,
        out_specs=[tile(DF_WIDTH)] * 3,
        out_shape=[jax.ShapeDtypeStruct((b, s, DF_WIDTH), BF16)] * 3,
        compiler_params=_cparams(("parallel", "parallel")),
    )(z_df, *tabs["df"])


def _df_attn_kernel(q_ref, k_ref, v_ref, lq1, lk1, lq2, lk2, g_ref, o_ref, *, lambda_init):
    lam = (jnp.exp(jnp.sum(lq1[...] * lk1[...], -1, keepdims=True))
           - jnp.exp(jnp.sum(lq2[...] * lk2[...], -1, keepdims=True)) + lambda_init)
    q = q_ref[0]
    k = k_ref[0]
    v = v_ref[0]
    lane = lax.broadcasted_iota(jnp.int32, (1, LANES), 1)
    zero = jnp.zeros_like(q)
    outs = []
    masks = ((lane < DF_HEAD_DIM), (lane >= DF_HEAD_DIM))
    scores = [_dot_nt(jnp.where(m, q, zero), k) for m in masks]
    for s in scores:
        pr = jnp.exp2(s - jnp.max(s, -1, keepdims=True))
        l = jnp.sum(pr, -1, keepdims=True)
        outs.append(_dot(pr.astype(BF16), v) * (1.0 / l))
    o = outs[0] - lam * outs[1]
    o = o * lax.rsqrt(jnp.mean(o * o, -1, keepdims=True) + DF_EPS) * g_ref[...]
    o_ref[0] = (o * (1.0 - lambda_init)).astype(o_ref.dtype)


def _df_attn(q, k, v, p, lambda_init):
    b, s, _ = q.shape
    tq = 256
    full = lambda a: pl.BlockSpec(a.shape, lambda bi, hi, qi: (0,) * a.ndim)
    consts = [p["lq1"], p["lk1"], p["lq2"], p["lk2"], p["subln"]]
    return pl.pallas_call(
        functools.partial(_df_attn_kernel, lambda_init=lambda_init),
        name="df_attn",
        grid=(b, DF_HEADS, s // tq),
        in_specs=[pl.BlockSpec((1, tq, LANES), lambda bi, hi, qi: (bi, qi, hi)),
                  pl.BlockSpec((1, s, LANES), lambda bi, hi, qi: (bi, 0, hi)),
                  pl.BlockSpec((1, s, LANES), lambda bi, hi, qi: (bi, 0, hi))]
                 + [full(a) for a in consts],
        out_specs=pl.BlockSpec((1, tq, LANES), lambda bi, hi, qi: (bi, qi, hi)),
        out_shape=jax.ShapeDtypeStruct((b, s, DF_WIDTH), BF16),
        compiler_params=_cparams(("parallel", "parallel", "parallel")),
    )(q, k, v, *consts)


def _merge_kernel(orw, omla, odf, g0, g1, g2, w0, w1, w2, o_ref):
    acc = g0[...].astype(F32) * _dot(orw[...], w0[...])
    acc = acc + g1[...].astype(F32) * _dot(omla[...], w1[...])
    acc = acc + g2[...].astype(F32) * _dot(odf[...], w2[...])
    o_ref[...] = acc.astype(o_ref.dtype)


def _merge(o_rw, o_mla, o_df, gates, p):
    t = o_rw.shape[0]
    tm = _pick_tile(t, (1024, 512, 256))
    tn = 1024
    nj = D_MODEL // tn
    act = lambda w: pl.BlockSpec((tm, w), lambda i, j: (i, 0))
    gate = lambda br: pl.BlockSpec((tm, tn), lambda i, j: (i, j + br * nj))
    wt = lambda w: pl.BlockSpec((w, tn), lambda i, j: (0, j))
    return pl.pallas_call(
        _merge_kernel,
        name="merge",
        grid=(t // tm, nj),
        in_specs=[act(RW_WIDTH), act(MLA_WIDTH), act(DF_WIDTH), gate(0), gate(1), gate(2),
                  wt(RW_WIDTH), wt(MLA_WIDTH), wt(DF_WIDTH)],
        out_specs=pl.BlockSpec((tm, tn), lambda i, j: (i, j)),
        out_shape=jax.ShapeDtypeStruct((t, D_MODEL), BF16),
        compiler_params=_cparams(("parallel", "parallel")),
    )(o_rw, o_mla, o_df, gates, gates, gates, p["w_up_rw"], p["w_up_mla"], p["w_up_df"])


def _layer_norm(y, g, b):
    mu = jnp.mean(y, -1, keepdims=True)
    yc = y - mu
    var = jnp.mean(yc * yc, -1, keepdims=True)
    return yc * lax.rsqrt(var + LN_EPS) * g + b


def _wo_ln_kernel(m_ref, w_ref, x_ref, g_ref, b_ref, o_ref, ob_ref):
    y = ALPHA * x_ref[...] + _dot(m_ref[...], w_ref[...])
    out = _layer_norm(y, g_ref[...], b_ref[...])
    o_ref[...] = out
    ob_ref[...] = out.astype(BF16).reshape(ob_ref.shape)


def _wo_ln(merged, x, p):
    t = x.shape[0]
    tm = 256
    tile = pl.BlockSpec((tm, D_MODEL), lambda i: (i, 0))
    tile3 = pl.BlockSpec((tm, ROW_SUB, LANES), lambda i: (i, 0, 0))
    full = lambda a: pl.BlockSpec(a.shape, lambda i: (0,) * a.ndim)
    return pl.pallas_call(
        _wo_ln_kernel,
        name="wo_ln",
        grid=(t // tm,),
        in_specs=[tile, full(p["w_o"]), tile, full(p["ln1_g"]), full(p["ln1_b"])],
        out_specs=[tile, tile3],
        out_shape=[jax.ShapeDtypeStruct((t, D_MODEL), F32),
                   jax.ShapeDtypeStruct((t, ROW_SUB, LANES), BF16)],
        compiler_params=_cparams(("parallel",)),
    )(merged, p["w_o"], x, p["ln1_g"], p["ln1_b"])


ROUTER_TILE = 1024


def _router_kernel(x_ref, wh_ref, wl_ref, bias_ref, tri_ref, idx_o, wts_o, rank_o, cnt_o, cnt_scr):
    @pl.when(pl.program_id(0) == 0)
    def _():
        cnt_scr[...] = jnp.zeros_like(cnt_scr)

    xh, xl = _split_bf16(x_ref[...])
    wh, wl = wh_ref[...], wl_ref[...]
    logits = _dot_nt(wh, xh) + (_dot_nt(wh, xl) + _dot_nt(wl, xh))
    scores = jax.nn.sigmoid(logits)
    sel = scores + bias_ref[...]
    tm = sel.shape[1]

    def row(a, i):
        return a[i:i + 1, :]

    best = jnp.zeros((1, tm), jnp.int32)
    best_s = None
    for g in range(N_GROUPS):
        a, b, c, d = (row(sel, EXPERTS_PER_GROUP * g + j) for j in range(4))
        hi1, lo1 = jnp.maximum(a, b), jnp.minimum(a, b)
        hi2, lo2 = jnp.maximum(c, d), jnp.minimum(c, d)
        gs = jnp.maximum(hi1, hi2) + jnp.maximum(jnp.minimum(hi1, hi2), jnp.maximum(lo1, lo2))
        if g == 0:
            best_s = gs
        else:
            upd = gs > best_s
            best = jnp.where(upd, g, best)
            best_s = jnp.where(upd, gs, best_s)

    def pick(a, j):
        out = row(a, j)
        for g in range(1, N_GROUPS):
            out = jnp.where(best == g, row(a, EXPERTS_PER_GROUP * g + j), out)
        return out

    cand = [pick(sel, j) for j in range(EXPERTS_PER_GROUP)]
    csc = [pick(scores, j) for j in range(EXPERTS_PER_GROUP)]
    neg = jnp.float32(-jnp.inf)

    def argmax4(vals):
        bi, bv = jnp.zeros((1, tm), jnp.int32), vals[0]
        for j in range(1, EXPERTS_PER_GROUP):
            upd = vals[j] > bv
            bi = jnp.where(upd, j, bi)
            bv = jnp.where(upd, vals[j], bv)
        return bi

    i1 = argmax4(cand)
    i2 = argmax4([jnp.where(i1 == j, neg, cand[j]) for j in range(EXPERTS_PER_GROUP)])

    def take(vals, i):
        out = vals[0]
        for j in range(1, EXPERTS_PER_GROUP):
            out = jnp.where(i == j, vals[j], out)
        return out

    w1, w2 = take(csc, i1), take(csc, i2)
    tot = w1 + w2
    e1 = best * EXPERTS_PER_GROUP + i1
    e2 = best * EXPERTS_PER_GROUP + i2
    eid = lax.broadcasted_iota(jnp.int32, (N_EXPERTS, tm), 0)
    oh1 = eid == e1
    oh2 = eid == e2
    oh = (oh1 | oh2).astype(BF16)
    before = _dot(oh, tri_ref[...]) + cnt_scr[...][:, 0:1]
    r1 = jnp.sum(jnp.where(oh1, before, 0.0), 0, keepdims=True)
    r2 = jnp.sum(jnp.where(oh2, before, 0.0), 0, keepdims=True)
    idx_o[0:1, :] = e1
    idx_o[1:2, :] = e2
    wts_o[0:1, :] = w1 / tot
    wts_o[1:2, :] = w2 / tot
    rank_o[0:1, :] = r1.astype(jnp.int32)
    rank_o[1:2, :] = r2.astype(jnp.int32)
    new_cnt = cnt_scr[...] + jnp.sum(oh.astype(F32), 1, keepdims=True)
    cnt_scr[...] = new_cnt
    cnt_o[...] = new_cnt


def _router(x, p):
    t = x.shape[0]
    tm = ROUTER_TILE
    full = lambda a: pl.BlockSpec(a.shape, lambda i: (0,) * a.ndim)
    tok = pl.BlockSpec((TOP_K, tm), lambda i: (0, i))
    consts = [p["router_wt_hi"], p["router_wt_lo"], p["router_bias"], p["router_tri"]]
    return pl.pallas_call(
        _router_kernel,
        name="router",
        grid=(t // tm,),
        in_specs=[pl.BlockSpec((tm, D_MODEL), lambda i: (i, 0))] + [full(a) for a in consts],
        out_specs=[tok, tok, tok, pl.BlockSpec((N_EXPERTS, LANES), lambda i: (0, 0))],
        out_shape=[jax.ShapeDtypeStruct((TOP_K, t), jnp.int32), jax.ShapeDtypeStruct((TOP_K, t), F32),
                   jax.ShapeDtypeStruct((TOP_K, t), jnp.int32),
                   jax.ShapeDtypeStruct((N_EXPERTS, LANES), F32)],
        scratch_shapes=[pltpu.VMEM((N_EXPERTS, LANES), F32)],
        compiler_params=_cparams(("arbitrary",)),
    )(x, *consts)


DISPATCH_TILE = 512


def _dispatch_kernel(dest_ref, x_ref, init_ref, xs_ref, sem):
    del init_ref
    def copy(r, k):
        return pltpu.make_async_copy(x_ref.at[r], xs_ref.at[dest_ref[0, TOP_K * r + k]], sem)

    def start(r, carry):
        copy(r, 0).start()
        copy(r, 1).start()
        return carry

    def wait(r, carry):
        copy(r, 0).wait()
        copy(r, 1).wait()
        return carry

    lax.fori_loop(0, DISPATCH_TILE, start, 0, unroll=DMA_UNROLL)
    lax.fori_loop(0, DISPATCH_TILE, wait, 0, unroll=DMA_UNROLL)


def _dispatch(xb, dest, n_rows):
    t = xb.shape[0]
    nt = t // DISPATCH_TILE
    dest2 = dest.reshape(nt, 1, DISPATCH_TILE * TOP_K)
    init = jnp.zeros((n_rows, ROW_SUB, LANES), BF16)
    return pl.pallas_call(
        _dispatch_kernel,
        name="dispatch",
        grid=(nt,),
        in_specs=[pl.BlockSpec((None, 1, DISPATCH_TILE * TOP_K), lambda i: (i, 0, 0),
                               memory_space=pltpu.SMEM),
                  pl.BlockSpec((DISPATCH_TILE, ROW_SUB, LANES), lambda i: (i, 0, 0)),
                  pl.BlockSpec(memory_space=pl.ANY)],
        out_specs=pl.BlockSpec(memory_space=pl.ANY),
        out_shape=jax.ShapeDtypeStruct((n_rows, ROW_SUB, LANES), BF16),
        scratch_shapes=[pltpu.SemaphoreType.DMA(())],
        input_output_aliases={2: 0},
        compiler_params=_cparams(("arbitrary",)),
    )(dest2, xb, init)


def _ffn_kernel(te_ref, nu_ref, xs_ref, wg_ref, wu_ref, wd_ref, y_ref):
    del te_ref

    @pl.when(pl.program_id(0) < nu_ref[0])
    def _():
        xs = xs_ref[...].reshape(FFN_TILE, D_MODEL)
        h = jax.nn.silu(_dot(xs, wg_ref[0])) * _dot(xs, wu_ref[0])
        y_ref[...] = _dot(h.astype(BF16), wd_ref[0]).reshape(y_ref.shape)

    @pl.when(pl.program_id(0) >= nu_ref[0])
    def _():
        y_ref[...] = jnp.zeros_like(y_ref)


def _ffn(xs, tile_expert, n_used, p):
    n_rows = xs.shape[0]
    tm = FFN_TILE
    grid_spec = pltpu.PrefetchScalarGridSpec(
        num_scalar_prefetch=2,
        grid=(n_rows // tm,),
        in_specs=[pl.BlockSpec((tm, ROW_SUB, LANES), lambda i, te, nu: (i, 0, 0)),
                  pl.BlockSpec((1, D_MODEL, D_EXPERT), lambda i, te, nu: (te[i], 0, 0)),
                  pl.BlockSpec((1, D_MODEL, D_EXPERT), lambda i, te, nu: (te[i], 0, 0)),
                  pl.BlockSpec((1, D_EXPERT, D_MODEL), lambda i, te, nu: (te[i], 0, 0))],
        out_specs=pl.BlockSpec((tm, ROW_SUB, LANES), lambda i, te, nu: (i, 0, 0)),
    )
    return pl.pallas_call(
        _ffn_kernel,
        name="ffn",
        grid_spec=grid_spec,
        out_shape=jax.ShapeDtypeStruct((n_rows, ROW_SUB, LANES), F32),
        compiler_params=_cparams(("arbitrary",)),
    )(tile_expert, n_used, xs, p["ex_w_gate"], p["ex_w_up"], p["ex_w_down"])


COMBINE_TILE = 256


def _combine_ln_kernel(dest_ref, dest_next_ref, y_ref, x_ref, w_ref, g_ref, b_ref, o_ref, ob_ref,
                       buf, sem):
    i = pl.program_id(0)
    slot = i % 2

    def copy(idx_ref, s, r, k):
        return pltpu.make_async_copy(y_ref.at[idx_ref[0, TOP_K * r + k]], buf.at[s, k, r],
                                     sem.at[s])

    def issue(idx_ref, s):
        def body(r, carry):
            copy(idx_ref, s, r, 0).start()
            copy(idx_ref, s, r, 1).start()
            return carry
        lax.fori_loop(0, COMBINE_TILE, body, 0, unroll=DMA_UNROLL)

    @pl.when(i == 0)
    def _():
        issue(dest_ref, 0)

    @pl.when(i + 1 < pl.num_programs(0))
    def _():
        issue(dest_next_ref, 1 - slot)

    def wait(r, carry):
        copy(dest_ref, slot, r, 0).wait()
        copy(dest_ref, slot, r, 1).wait()
        return carry

    lax.fori_loop(0, COMBINE_TILE, wait, 0, unroll=DMA_UNROLL)
    w = w_ref[...]
    shape = (COMBINE_TILE, D_MODEL)
    ffn = w[:, 0:1] * buf[slot, 0].reshape(shape) + w[:, 1:2] * buf[slot, 1].reshape(shape)
    out = _layer_norm(ALPHA * x_ref[...] + ffn, g_ref[...], b_ref[...])
    o_ref[...] = out
    ob_ref[...] = out.astype(BF16)


def _combine_ln(y, dest, x, wts, p):
    t = x.shape[0]
    tm = COMBINE_TILE
    nt = t // tm
    dest2 = dest.reshape(nt, 1, tm * TOP_K)
    tile = pl.BlockSpec((tm, D_MODEL), lambda i: (i, 0))
    full = lambda a: pl.BlockSpec(a.shape, lambda i: (0,) * a.ndim)
    return pl.pallas_call(
        _combine_ln_kernel,
        name="combine_ln",
        grid=(nt,),
        in_specs=[pl.BlockSpec((None, 1, tm * TOP_K), lambda i: (i, 0, 0), memory_space=pltpu.SMEM),
                  pl.BlockSpec((None, 1, tm * TOP_K), lambda i: (jnp.minimum(i + 1, nt - 1), 0, 0),
                               memory_space=pltpu.SMEM),
                  pl.BlockSpec(memory_space=pl.ANY), tile,
                  pl.BlockSpec((tm, TOP_K), lambda i: (i, 0)), full(p["ln2_g"]), full(p["ln2_b"])],
        out_specs=[tile, tile],
        out_shape=[jax.ShapeDtypeStruct((t, D_MODEL), F32), jax.ShapeDtypeStruct((t, D_MODEL), BF16)],
        scratch_shapes=[pltpu.VMEM((2, TOP_K, tm, ROW_SUB, LANES), F32),
                        pltpu.SemaphoreType.DMA((2,))],
        compiler_params=_cparams(("arbitrary",)),
    )(dest2, dest2, y, x, wts, p["ln2_g"], p["ln2_b"])


def _moe_ln(x, xb, p):
    t = x.shape[0]
    idx, wts, rank, counts = _router(x, p)
    counts = counts[:, 0].astype(jnp.int32)
    tiles = (counts + FFN_TILE - 1) // FFN_TILE
    tile_end = jnp.cumsum(tiles)
    row_start = (tile_end - tiles) * FFN_TILE
    n_tiles = (t * TOP_K) // FFN_TILE + N_EXPERTS
    start_of = jnp.sum(jnp.where(idx[..., None] == jnp.arange(N_EXPERTS), row_start, 0), -1)
    dest = (start_of + rank).T.reshape(-1)
    tile_expert = jnp.minimum(jnp.sum(jnp.arange(n_tiles)[:, None] >= tile_end[None, :], axis=1),
                              N_EXPERTS - 1).astype(jnp.int32)
    n_used = tile_end[-1:].astype(jnp.int32)
    xs = _dispatch(xb, dest, n_tiles * FFN_TILE)
    y = _ffn(xs, tile_expert, n_used, p)
    return _combine_ln(y, dest, x, wts.T, p)


def _block_diag_ones(n, blk):
    i = jnp.arange(n) // blk
    return (i[:, None] == i[None, :]).astype(BF16)


def _prep_layer(l, w):
    row = lambda a: a.reshape(1, -1).astype(F32)
    w_in = w["w_in"][l]
    o1, o2, o3 = RW_IN, RW_IN + MLA_IN, RW_IN + MLA_IN + DF_IN
    p = {
        "w_in_rw": w_in[:, :o1].astype(BF16),
        "w_in_mla": jnp.pad(w_in[:, o1:o2], ((0, 0), (0, MLA_IN_PAD - MLA_IN))).astype(BF16),
        "w_in_df": w_in[:, o2:o3].astype(BF16),
        "w_in_gate": w_in[:, o3:].astype(BF16),
        "mu_prev": row(w["shift_prev"][l]), "mu_next": row(w["shift_next"][l]),
        "w0": row(w["rw_w0"][l]), "a0": row(w["rw_a0"][l]),
        "g2": w["rw_g2"][l].astype(BF16),
        "k_k": row(w["rw_k_k"][l]), "k_a": row(w["rw_k_a"][l]), "r_k": row(w["rw_r_k"][l]),
        "lnx_g": row(w["rw_lnx_g"][l]), "lnx_b": row(w["rw_lnx_b"][l]),
        "bd": _block_diag_ones(RW_WIDTH, RW_HEAD_DIM),
        "q_norm": row(w["mla_q_norm"][l]), "kv_norm": row(w["mla_kv_norm"][l]),
        "lq1": row(w["df_lq1"][l]), "lk1": row(w["df_lk1"][l]),
        "lq2": row(w["df_lq2"][l]), "lk2": row(w["df_lk2"][l]),
        "subln": row(w["df_subln"][l]),
        "w_up_rw": w["w_up_rw"][l].astype(BF16), "w_up_mla": w["w_up_mla"][l].astype(BF16),
        "w_up_df": w["w_up_df"][l].astype(BF16), "w_o": w["w_o"][l].astype(BF16),
        "ln1_g": row(w["ln1_g"][l]), "ln1_b": row(w["ln1_b"][l]),
        "ln2_g": row(w["ln2_g"][l]), "ln2_b": row(w["ln2_b"][l]),
        "ex_w_gate": w["ex_w_gate"][l].astype(BF16), "ex_w_up": w["ex_w_up"][l].astype(BF16),
        "ex_w_down": w["ex_w_down"][l].astype(BF16),
    }
    zc = jnp.zeros((DECAY_LORA, RW_WIDTH), F32)
    w2 = w["rw_w2"][l]
    a2 = w["rw_a2"][l]
    p["w2cat"] = jnp.block([[w2[0], zc], [zc, w2[1]]]).astype(BF16)
    p["a2cat"] = jnp.block([[a2[0], zc], [zc, a2[1]]]).astype(BF16)
    wq = w["mla_w_uq"][l].reshape(MLA_Q_LORA, MLA_HEADS, MLA_NOPE + MLA_ROPE)
    p["wq"] = jnp.pad(wq, ((0, 0), (0, 0), (0, MLA_HEAD_PAD - MLA_NOPE - MLA_ROPE))
                      ).reshape(MLA_Q_LORA, -1).astype(BF16)
    wkv = w["mla_w_ukv"][l].reshape(MLA_KV_LORA, MLA_HEADS, MLA_NOPE + MLA_V)
    wk_nope = jnp.pad(wkv[:, :, :MLA_NOPE], ((0, 0), (0, 0), (0, MLA_HEAD_PAD - MLA_NOPE)))
    place = jnp.zeros((LANES, MLA_HEADS, MLA_HEAD_PAD), F32)
    j = jnp.arange(MLA_ROPE)
    place = place.at[j, :, MLA_NOPE + j].set(1.0)
    p["wk"] = jnp.concatenate([wk_nope, place], axis=0).reshape(MLA_KV_LORA + LANES, -1).astype(BF16)
    p["wv"] = wkv[:, :, MLA_NOPE:].reshape(MLA_KV_LORA, -1).astype(BF16)
    return p


def _trunk(x3, layers, shared):
    b, s, d = x3.shape
    t = b * s
    tabs = {
        "mla_q": _rope_tables(s, MLA_ROPE, MLA_NOPE, LANES),
        "mla_k": _rope_tables(s, MLA_ROPE, 0, LANES),
        "df": _rope_tables(s, DF_HEAD_DIM, 0, DF_HEAD_DIM),
    }
    x = x3.reshape(t, d)
    xb = x.astype(BF16)
    for l, p in enumerate(layers):
        p = dict(p, **shared)
        lambda_init = 0.8 - 0.6 * math.exp(-0.3 * l)
        z_rw = _matmul(xb, p["w_in_rw"], F32).reshape(b, s, -1)
        z_mla = _matmul(xb, p["w_in_mla"], F32).reshape(b, s, -1)
        z_df = _matmul(xb, p["w_in_df"], F32).reshape(b, s, -1)
        gates = _matmul(xb, p["w_in_gate"], BF16, act="sigmoid")
        r, v, kk, g, bonus, logd, a, kdir = _rw_prep(z_rw, p)
        yf, yb = _rw_scan(r, v, kk, logd, a, kdir)
        o_rw = _rw_post(yf.reshape(t, -1), yb.reshape(t, -1), bonus.reshape(t, -1),
                        g.reshape(t, -1), p)
        q, k, v2 = _mla_prep(z_mla, p, tabs)
        o_mla = _mla_attn(q, k, v2).reshape(t, -1)
        q, k, v2 = _df_prep(z_df, tabs)
        o_df = _df_attn(q, k, v2, p, lambda_init).reshape(t, -1)
        merged = _merge(o_rw, o_mla, o_df, gates, p)
        x, xb = _wo_ln(merged, x, p)
        x, xb = _moe_ln(x, xb, p)
    return x.reshape(b, s, d)


def kernel(x_prompt, x_sample, w_in, shift_prev, shift_next, rw_w0, rw_w2, rw_a0, rw_a2, rw_g2,
           rw_k_k, rw_k_a, rw_r_k, rw_lnx_g, rw_lnx_b, mla_q_norm, mla_kv_norm, mla_w_uq,
           mla_w_ukv, df_lq1, df_lk1, df_lq2, df_lk2, df_subln, w_up_rw, w_up_mla, w_up_df, w_o,
           ln1_g, ln1_b, ln2_g, ln2_b, router_w, router_bias, ex_w_gate, ex_w_up, ex_w_down):
    w = dict(w_in=w_in, shift_prev=shift_prev, shift_next=shift_next, rw_w0=rw_w0, rw_w2=rw_w2,
             rw_a0=rw_a0, rw_a2=rw_a2, rw_g2=rw_g2, rw_k_k=rw_k_k, rw_k_a=rw_k_a, rw_r_k=rw_r_k,
             rw_lnx_g=rw_lnx_g, rw_lnx_b=rw_lnx_b, mla_q_norm=mla_q_norm,
             mla_kv_norm=mla_kv_norm, mla_w_uq=mla_w_uq, mla_w_ukv=mla_w_ukv, df_lq1=df_lq1,
             df_lk1=df_lk1, df_lq2=df_lq2, df_lk2=df_lk2, df_subln=df_subln, w_up_rw=w_up_rw,
             w_up_mla=w_up_mla, w_up_df=w_up_df, w_o=w_o, ln1_g=ln1_g, ln1_b=ln1_b, ln2_g=ln2_g,
             ln2_b=ln2_b, ex_w_gate=ex_w_gate, ex_w_up=ex_w_up, ex_w_down=ex_w_down)
    layers = [_prep_layer(l, w) for l in range(DEPTH)]
    rwt = router_w.T.astype(F32)
    rwt_hi = rwt.astype(BF16)
    ti = jnp.arange(ROUTER_TILE)
    shared = {
        "router_wt_hi": rwt_hi,
        "router_wt_lo": (rwt - rwt_hi.astype(F32)).astype(BF16),
        "router_bias": router_bias.reshape(N_EXPERTS, 1).astype(F32),
        "router_tri": (ti[:, None] < ti[None, :]).astype(BF16),
    }
    return (_trunk(x_prompt, layers, shared), _trunk(x_sample, layers, shared))
```

```python
import functools
import math

import jax
import jax.numpy as jnp
from jax import lax
from jax.experimental import pallas as pl
from jax.experimental.pallas import tpu as pltpu

F32 = jnp.float32
BF16 = jnp.bfloat16

D_MODEL = 2048
DEPTH = 2
RW_HEADS, RW_HEAD_DIM = 12, 64
RW_WIDTH = RW_HEADS * RW_HEAD_DIM
DECAY_LORA, ICLR_LORA, GATE_LORA, N_DIR = 64, 64, 128, 2
GN_EPS = 64e-5
MLA_HEADS, MLA_NOPE, MLA_ROPE, MLA_V = 8, 64, 32, 64
MLA_Q_LORA, MLA_KV_LORA = 512, 256
MLA_WIDTH = MLA_HEADS * MLA_V
DF_HEADS, DF_HEAD_DIM = 6, 64
DF_WIDTH = DF_HEADS * 2 * DF_HEAD_DIM
DF_EPS = 1e-5
N_BRANCH = 3
N_EXPERTS, N_GROUPS, TOP_K, D_EXPERT = 16, 4, 2, 1024
EXPERTS_PER_GROUP = N_EXPERTS // N_GROUPS
ROPE_THETA = 10000.0
LN_EPS = 1e-5
RMS_EPS = 1e-6
ALPHA = (2 * DEPTH) ** 0.25
RW_IN = 3 * RW_WIDTH + N_DIR * DECAY_LORA + N_DIR * ICLR_LORA + GATE_LORA
MLA_IN = MLA_Q_LORA + MLA_KV_LORA + MLA_ROPE
DF_IN = 3 * DF_WIDTH
GATE_IN = N_BRANCH * D_MODEL

LANES = 128
MLA_IN_PAD = 896
MLA_HEAD_PAD = 128
RW_CHUNK = 64
RW_PAIRS = RW_WIDTH // LANES
VMEM_LIMIT = 56 * 1024 * 1024
FFN_TILE = 512
LOG2E = math.log2(math.e)
MLA_HEADS_PER_STEP = 4
DF_HEADS_PER_STEP = 2
DMA_UNROLL = 8
HALO = 8
ROW_SUB = D_MODEL // LANES


def _cparams(sem):
    return pltpu.CompilerParams(dimension_semantics=sem, vmem_limit_bytes=VMEM_LIMIT)


def _dot(a, b):
    return jnp.dot(a, b, preferred_element_type=F32)


def _dot_nt(a, b):
    return lax.dot_general(a, b, (((1,), (1,)), ((), ())), preferred_element_type=F32)


def _dot_tn(a, b):
    return lax.dot_general(a, b, (((0,), (0,)), ((), ())), preferred_element_type=F32)


def _split_bf16(x):
    hi = x.astype(BF16)
    lo = (x - hi.astype(F32)).astype(BF16)
    return hi, lo


def _pick_tile(n, candidates):
    for c in candidates:
        if n % c == 0:
            return c
    raise ValueError(f"no tile for {n}")


def _mm_kernel(x_ref, w_ref, o_ref, *, act):
    acc = _dot(x_ref[...], w_ref[...])
    if act == "sigmoid":
        acc = jax.nn.sigmoid(acc)
    o_ref[...] = acc.astype(o_ref.dtype)


def _matmul(x, w, out_dtype, act=None):
    m, k = x.shape
    n = w.shape[1]
    tm = _pick_tile(m, (1024, 512, 256, 128))
    tn = _pick_tile(n, (1024, 896, 768, 512, 256, 128))
    return pl.pallas_call(
        functools.partial(_mm_kernel, act=act),
        name="mm",
        grid=(n // tn, m // tm),
        in_specs=[pl.BlockSpec((tm, k), lambda j, i: (i, 0)),
                  pl.BlockSpec((k, tn), lambda j, i: (0, j))],
        out_specs=pl.BlockSpec((tm, tn), lambda j, i: (i, j)),
        out_shape=jax.ShapeDtypeStruct((m, n), out_dtype),
        compiler_params=_cparams(("parallel", "parallel")),
    )(x, w)


def _rw_prep_kernel(z_ref, hp_ref, hn_ref, mup_ref, mun_ref, w2_ref, a2_ref, g2_ref, w0_ref,
                    a0_ref, kk_k_ref, k_a_ref, r_k_ref, bd_ref,
                    r_o, v_o, kk_o, g_o, bonus_o, logd_o, a_o, kdir_o):
    z = z_ref[0]
    tr = z.shape[0]
    row = lax.broadcasted_iota(jnp.int32, (tr, 1), 0)
    ti = pl.program_id(1)
    halo_prev = jnp.where(ti == 0, 0.0, hp_ref[0, HALO - 1:HALO, :])
    halo_next = jnp.where(ti == pl.num_programs(1) - 1, 0.0, hn_ref[0, 0:1, :])
    prev = jnp.where(row == 0, halo_prev, pltpu.roll(z, 1, 0))
    nxt = jnp.where(row == tr - 1, halo_next, pltpu.roll(z, tr - 1, 0))
    zs = z + mup_ref[...] * (prev - z) + mun_ref[...] * (nxt - z)
    c = RW_WIDTH
    r, k, v = zs[:, 0:c], zs[:, c:2 * c], zs[:, 2 * c:3 * c]
    wl = zs[:, 3 * c:3 * c + 128]
    al = zs[:, 3 * c + 128:3 * c + 256]
    gl = zs[:, 3 * c + 256:3 * c + 384]
    w_raw = w0_ref[...] + _dot(jnp.tanh(wl).astype(BF16), w2_ref[...])
    a = jax.nn.sigmoid(a0_ref[...] + _dot(al.astype(BF16), a2_ref[...]))
    g = _dot(jax.nn.sigmoid(gl).astype(BF16), g2_ref[...])
    logd = (-math.exp(-0.5)) * jax.nn.sigmoid(w_raw)
    bd = bd_ref[...]

    def head_sum(x):
        hi, lo = _split_bf16(x)
        return _dot(hi, bd) + _dot(lo, bd)

    kk = k * kk_k_ref[...]
    kk = kk * lax.rsqrt(head_sum(kk * kk) + 1e-12)
    k_a = k_a_ref[...]
    kd0 = k * (1.0 + (a[:, 0:c] - 1.0) * k_a)
    kd1 = k * (1.0 + (a[:, c:2 * c] - 1.0) * k_a)
    bonus = head_sum(r * r_k_ref[...] * (kd0 + kd1)) * v
    r_o[0] = r
    v_o[0] = v
    kk_o[0] = kk
    g_o[0] = g
    bonus_o[0] = bonus
    logd_o[0] = logd
    a_o[0] = a
    kdir_o[0, :, 0:c] = kd0
    kdir_o[0, :, c:2 * c] = kd1


def _rw_prep(z_rw, p):
    b, s, _ = z_rw.shape
    tr = 256
    nt = s // tr
    hpt = tr // HALO
    c = RW_WIDTH
    tile = lambda w: pl.BlockSpec((1, tr, w), lambda bi, ti: (bi, ti, 0))
    halo_prev = pl.BlockSpec((1, HALO, RW_IN), lambda bi, ti: (bi, jnp.maximum(ti * hpt - 1, 0), 0))
    halo_next = pl.BlockSpec((1, HALO, RW_IN),
                             lambda bi, ti: (bi, jnp.minimum((ti + 1) * hpt, s // HALO - 1), 0))
    full = lambda a: pl.BlockSpec(a.shape, lambda bi, ti: (0,) * a.ndim)
    consts = [p["mu_prev"], p["mu_next"], p["w2cat"], p["a2cat"], p["g2"], p["w0"], p["a0"],
              p["k_k"], p["k_a"], p["r_k"], p["bd"]]
    out_w = [c, c, c, c, c, 2 * c, 2 * c, 2 * c]
    return pl.pallas_call(
        _rw_prep_kernel,
        name="rw_prep",
        grid=(b, nt),
        in_specs=[tile(RW_IN), halo_prev, halo_next] + [full(a) for a in consts],
        out_specs=[tile(w) for w in out_w],
        out_shape=[jax.ShapeDtypeStruct((b, s, w), F32) for w in out_w],
        compiler_params=_cparams(("parallel", "parallel")),
    )(z_rw, z_rw, z_rw, *consts)


def _rw_scan_kernel(rf, vf, kkf, ldf, af, kdf, rb, vb, kkb, ldb, ab, kdb, yf_o, yb_o, z_scr):
    @pl.when(pl.program_id(1) == 0)
    def _():
        z_scr[...] = jnp.zeros_like(z_scr)

    c = RW_CHUNK
    ti = lax.broadcasted_iota(jnp.int32, (c, c), 0)
    tj = lax.broadcasted_iota(jnp.int32, (c, c), 1)
    lane = lax.broadcasted_iota(jnp.int32, (1, LANES), 1)
    lane_m = (lane < RW_HEAD_DIM, lane >= RW_HEAD_DIM)
    bi = lax.broadcasted_iota(jnp.int32, (LANES, LANES), 0) // RW_HEAD_DIM
    bj = lax.broadcasted_iota(jnp.int32, (LANES, LANES), 1) // RW_HEAD_DIM
    bdmask = bi == bj
    incl = ((ti >= tj), (ti <= tj))
    strict = ((ti > tj), (ti < tj))
    incl_bf = tuple(m.astype(BF16) for m in incl)
    ones = jnp.ones((c, LANES), BF16)
    in_refs = ((rf, vf, kkf, ldf, af, kdf), (rb, vb, kkb, ldb, ab, kdb))
    out_refs = (yf_o, yb_o)

    probs = [(d, p) for d in range(N_DIR) for p in range(RW_PAIRS)]
    chains = [(s, h) for s in probs for h in range(2)]
    sl = lambda p: slice(p * LANES, (p + 1) * LANES)
    val = {s: [ref[0, :, sl(s[1])] for ref in in_refs[s[0]]] for s in probs}
    ld_split = {s: _split_bf16(val[s][3]) for s in probs}
    logp = {s: _dot(incl_bf[s[0]], ld_split[s][0]) + _dot(incl_bf[s[0]], ld_split[s][1])
            for s in probs}
    logpc = {s: _dot_tn(ld_split[s][0], ones) + _dot_tn(ld_split[s][1], ones)
             for s in probs}
    at, rt, bt, kt, vv, z, zb = {}, {}, {}, {}, {}, {}, {}
    for s in probs:
        r, v, kk, ld, a, kd = val[s]
        pinv = jnp.exp(-logp[s])
        at[s] = -(kk * jnp.exp(logp[s] - ld))
        rt[s] = r * jnp.exp(logp[s])
        bt[s] = (kk * a * pinv).astype(BF16)
        kt[s] = (kd * pinv).astype(BF16)
        vv[s] = v
        z[s] = z_scr[s[0], s[1]]
        zb[s] = z[s].astype(BF16)
    lhs, vh = {}, {}
    for s, h in chains:
        m = lane_m[h]
        lhs[s, h] = jnp.concatenate([jnp.where(m, at[s], 0.0), jnp.where(m, rt[s], 0.0)],
                                    axis=0).astype(BF16)
        vh[s, h] = jnp.where(m, vv[s], 0.0).astype(BF16)
    gb = {ch: _dot_nt(lhs[ch], bt[ch[0]]) for ch in chains}
    gk = {ch: _dot_nt(lhs[ch], kt[ch[0]]) for ch in chains}
    a_ak = {ch: jnp.where(strict[ch[0][0]], gk[ch][0:c], 0.0).astype(BF16) for ch in chains}
    a_rb = {ch: jnp.where(incl[ch[0][0]], gb[ch][c:], 0.0).astype(BF16) for ch in chains}
    a_rk = {ch: jnp.where(incl[ch[0][0]], gk[ch][c:], 0.0).astype(BF16) for ch in chains}
    ap = {ch: jnp.where(strict[ch[0][0]], gb[ch][0:c], 0.0).astype(BF16) for ch in chains}
    x = {ch: _dot(lhs[ch][0:c], zb[ch[0]]) + _dot(a_ak[ch], vh[ch]) for ch in chains}
    for i in range(6):
        x = {ch: x[ch] + _dot(ap[ch], x[ch].astype(BF16)) for ch in chains}
        if i < 5:
            ap = {ch: _dot(ap[ch], ap[ch]).astype(BF16) for ch in chains}
    y = {ch: _dot(lhs[ch][c:], zb[ch[0]]) + _dot(a_rb[ch], x[ch].astype(BF16))
             + _dot(a_rk[ch], vh[ch]) for ch in chains}
    for s in probs:
        d, p = s
        out_refs[d][0, :, sl(p)] = y[s, 0] + y[s, 1]
        u = x[s, 0] + x[s, 1]
        upd = _dot_tn(jnp.concatenate([bt[s], kt[s]], axis=0),
                      jnp.concatenate([u.astype(BF16), vv[s].astype(BF16)], axis=0))
        z_scr[d, p] = jnp.where(bdmask, jnp.exp(logpc[s]) * (z[s] + upd), 0.0)


def _rw_scan(r, v, kk, logd, a, kdir):
    b, s, c = r.shape
    nc = s // RW_CHUNK
    fwd = pl.BlockSpec((1, RW_CHUNK, c), lambda bi, ci: (bi, ci, 0))
    bwd = pl.BlockSpec((1, RW_CHUNK, c), lambda bi, ci: (bi, nc - 1 - ci, 0))
    bwd_dir = pl.BlockSpec((1, RW_CHUNK, c), lambda bi, ci: (bi, nc - 1 - ci, 1))
    return pl.pallas_call(
        _rw_scan_kernel,
        name="rw_scan",
        grid=(b, nc),
        in_specs=[fwd, fwd, fwd, fwd, fwd, fwd, bwd, bwd, bwd, bwd_dir, bwd_dir, bwd_dir],
        out_specs=[fwd, bwd],
        out_shape=[jax.ShapeDtypeStruct((b, s, c), F32)] * 2,
        scratch_shapes=[pltpu.VMEM((N_DIR, RW_PAIRS, LANES, LANES), F32)],
        compiler_params=_cparams(("parallel", "arbitrary")),
    )(r, v, kk, logd, a, kdir, r, v, kk, logd, a, kdir)


def _rw_post_kernel(yf_ref, yb_ref, bonus_ref, g_ref, lg_ref, lb_ref, bd_ref, o_ref):
    y = yf_ref[...] + yb_ref[...]
    bd = bd_ref[...]

    def head_mean(x):
        hi, lo = _split_bf16(x)
        return (_dot(hi, bd) + _dot(lo, bd)) * (1.0 / RW_HEAD_DIM)

    mu = head_mean(y)
    yc = y - mu
    var = head_mean(yc * yc)
    yn = yc * lax.rsqrt(var + GN_EPS) * lg_ref[...] + lb_ref[...]
    o_ref[...] = ((yn + bonus_ref[...]) * g_ref[...]).astype(o_ref.dtype)


def _rw_post(yf, yb, bonus, g, p):
    t, c = yf.shape
    tm = _pick_tile(t, (1024, 512, 256))
    tile = pl.BlockSpec((tm, c), lambda i: (i, 0))
    full = lambda a: pl.BlockSpec(a.shape, lambda i: (0,) * a.ndim)
    consts = [p["lnx_g"], p["lnx_b"], p["bd"]]
    return pl.pallas_call(
        _rw_post_kernel,
        name="rw_post",
        grid=(t // tm,),
        in_specs=[tile] * 4 + [full(a) for a in consts],
        out_specs=tile,
        out_shape=jax.ShapeDtypeStruct((t, c), BF16),
        compiler_params=_cparams(("parallel",)),
    )(yf, yb, bonus, g, *consts)


def _rope_block(x, cos, s_up, s_dn, half):
    return x * cos + pltpu.roll(x, half, 1) * s_up + pltpu.roll(x, LANES - half, 1) * s_dn


def _rope_tables(s, dim, lane_of_x1, period):
    half = dim // 2
    inv_freq = jnp.power(ROPE_THETA, -jnp.arange(half, dtype=F32) * (2.0 / dim))
    ang = jnp.arange(s, dtype=F32)[:, None] * inv_freq[None, :]
    lane = jnp.arange(LANES) % period - lane_of_x1
    in_x1 = (lane >= 0) & (lane < half)
    in_x2 = (lane >= half) & (lane < dim)
    j = jnp.clip(jnp.where(in_x2, lane - half, lane), 0, half - 1)
    cos = jnp.cos(ang)[:, j]
    sin = jnp.sin(ang)[:, j]
    cos_t = jnp.where(in_x1 | in_x2, cos, 1.0)
    s_up = jnp.where(in_x2, sin, 0.0)
    s_dn = jnp.where(in_x1, -sin, 0.0)
    return cos_t.astype(F32), s_up.astype(F32), s_dn.astype(F32)


def _mla_prep_kernel(z_ref, qg_ref, kvg_ref, wq_ref, wk_ref, wv_ref,
                     cq_ref, squ_ref, sqd_ref, ck_ref, sku_ref, skd_ref, q_o, k_o, v_o):
    z = z_ref[0]
    c_q = z[:, 0:MLA_Q_LORA]
    c_kv = z[:, MLA_Q_LORA:MLA_Q_LORA + MLA_KV_LORA]
    kr = z[:, MLA_Q_LORA + MLA_KV_LORA:MLA_IN_PAD]
    c_q = c_q * lax.rsqrt(jnp.mean(c_q * c_q, -1, keepdims=True) + RMS_EPS) * qg_ref[...]
    c_kv = c_kv * lax.rsqrt(jnp.mean(c_kv * c_kv, -1, keepdims=True) + RMS_EPS) * kvg_ref[...]
    q = _dot(c_q.astype(BF16), wq_ref[...])
    scale = (MLA_NOPE + MLA_ROPE) ** -0.5 * LOG2E
    cq, squ, sqd = cq_ref[...], squ_ref[...], sqd_ref[...]
    for h in range(MLA_HEADS):
        sl = slice(h * MLA_HEAD_PAD, (h + 1) * MLA_HEAD_PAD)
        q_o[0, :, sl] = (_rope_block(q[:, sl], cq, squ, sqd, MLA_ROPE // 2) * scale).astype(BF16)
    kr = _rope_block(kr, ck_ref[...], sku_ref[...], skd_ref[...], MLA_ROPE // 2)
    ckv_b = c_kv.astype(BF16)
    k_in = jnp.concatenate([ckv_b, kr.astype(BF16)], axis=1)
    k_o[0] = _dot(k_in, wk_ref[...]).astype(BF16)
    v_o[0] = _dot_nt(wv_ref[...], ckv_b).astype(BF16)


def _mla_prep(z_mla, p, tabs):
    b, s, _ = z_mla.shape
    tr = 256
    tile = lambda w: pl.BlockSpec((1, tr, w), lambda bi, ti: (bi, ti, 0))
    full = lambda a: pl.BlockSpec(a.shape, lambda bi, ti: (0,) * a.ndim)
    tab = pl.BlockSpec((tr, LANES), lambda bi, ti: (ti, 0))
    consts = [p["q_norm"], p["kv_norm"], p["wq"], p["wk"], p["wv"]]
    hq = MLA_HEADS * MLA_HEAD_PAD
    return pl.pallas_call(
        _mla_prep_kernel,
        name="mla_prep",
        grid=(b, s // tr),
        in_specs=[tile(MLA_IN_PAD)] + [full(a) for a in consts] + [tab] * 6,
        out_specs=[tile(hq), tile(hq),
                   pl.BlockSpec((1, MLA_WIDTH, tr), lambda bi, ti: (bi, 0, ti))],
        out_shape=[jax.ShapeDtypeStruct((b, s, hq), BF16), jax.ShapeDtypeStruct((b, s, hq), BF16),
                   jax.ShapeDtypeStruct((b, MLA_WIDTH, s), BF16)],
        compiler_params=_cparams(("parallel", "parallel")),
    )(z_mla, *consts, *tabs["mla_q"], *tabs["mla_k"])


def _softmax_pv_t(s_t, v_t):
    pr = jnp.exp2(s_t - jnp.max(s_t, 0, keepdims=True))
    l = jnp.sum(pr, 0, keepdims=True)
    return _dot(v_t, pr.astype(BF16)) * (1.0 / l)


def _mla_attn_kernel(q_ref, k_ref, v_ref, o_ref):
    nh = MLA_HEADS_PER_STEP
    sl = lambda h: slice(h * MLA_HEAD_PAD, (h + 1) * MLA_HEAD_PAD)
    scores = [_dot_nt(k_ref[0, :, sl(h)], q_ref[0, :, sl(h)]) for h in range(nh)]
    outs = []
    for h, s_t in enumerate(scores):
        pair = h // 2
        o_t = _softmax_pv_t(s_t, v_ref[0, pair * LANES:(pair + 1) * LANES, :])
        outs.append(o_t[(h % 2) * MLA_V:(h % 2 + 1) * MLA_V])
    o_ref[0] = jnp.concatenate(outs, axis=0).T.astype(o_ref.dtype)


def _mla_attn(q, k, v):
    b, s, _ = q.shape
    tq = 256
    nh = MLA_HEADS_PER_STEP
    return pl.pallas_call(
        _mla_attn_kernel,
        name="mla_attn",
        grid=(b, MLA_HEADS // nh, s // tq),
        in_specs=[pl.BlockSpec((1, tq, nh * MLA_HEAD_PAD), lambda bi, pi, qi: (bi, qi, pi)),
                  pl.BlockSpec((1, s, nh * MLA_HEAD_PAD), lambda bi, pi, qi: (bi, 0, pi)),
                  pl.BlockSpec((1, nh * MLA_V, s), lambda bi, pi, qi: (bi, pi, 0))],
        out_specs=pl.BlockSpec((1, tq, nh * MLA_V), lambda bi, pi, qi: (bi, qi, pi)),
        out_shape=jax.ShapeDtypeStruct((b, s, MLA_WIDTH), BF16),
        compiler_params=_cparams(("parallel", "parallel", "parallel")),
    )(q, k, v)


def _df_prep_kernel(z_ref, c_ref, su_ref, sd_ref, q_o, k_o, v_o):
    cos, s_up, s_dn = c_ref[...], su_ref[...], sd_ref[...]
    scale = DF_HEAD_DIM ** -0.5 * LOG2E
    for h in range(DF_HEADS):
        sl = slice(h * LANES, (h + 1) * LANES)
        q_o[0, :, sl] = (_rope_block(z_ref[0, :, sl], cos, s_up, s_dn, DF_HEAD_DIM // 2)
                         * scale).astype(BF16)
        slk = slice(DF_WIDTH + h * LANES, DF_WIDTH + (h + 1) * LANES)
        k_o[0, :, sl] = _rope_block(z_ref[0, :, slk], cos, s_up, s_dn,
                                    DF_HEAD_DIM // 2).astype(BF16)
    v_o[0] = z_ref[0, :, 2 * DF_WIDTH:3 * DF_WIDTH].T.astype(BF16)


def _df_prep(z_df, tabs):
    b, s, _ = z_df.shape
    tr = 256
    tile = lambda w: pl.BlockSpec((1, tr, w), lambda bi, ti: (bi, ti, 0))
    tab = pl.BlockSpec((tr, LANES), lambda bi, ti: (ti, 0))
    return pl.pallas_call(
        _df_prep_kernel,
        name="df_prep",
        grid=(b, s // tr),
        in_specs=[tile(DF_IN)] + [tab] * 3,
        out_specs=[tile(DF_WIDTH), tile(DF_WIDTH),
                   pl.BlockSpec((1, DF_WIDTH, tr), lambda bi, ti: (bi, 0, ti))],
        out_shape=[jax.ShapeDtypeStruct((b, s, DF_WIDTH), BF16)] * 2
                  + [jax.ShapeDtypeStruct((b, DF_WIDTH, s), BF16)],
        compiler_params=_cparams(("parallel", "parallel")),
    )(z_df, *tabs["df"])


def _df_attn_kernel(q_ref, k_ref, v_ref, lq1, lk1, lq2, lk2, g_ref, o_ref, *, lambda_init):
    lam = (jnp.exp(jnp.sum(lq1[...] * lk1[...], -1, keepdims=True))
           - jnp.exp(jnp.sum(lq2[...] * lk2[...], -1, keepdims=True)) + lambda_init)
    nh = DF_HEADS_PER_STEP
    lane = lax.broadcasted_iota(jnp.int32, (1, LANES), 1)
    masks = ((lane < DF_HEAD_DIM), (lane >= DF_HEAD_DIM))
    sl = lambda h: slice(h * LANES, (h + 1) * LANES)
    scores = [_dot_nt(k_ref[0, :, sl(h)], jnp.where(m, q_ref[0, :, sl(h)], 0.0).astype(BF16))
              for h in range(nh) for m in masks]
    outs = []
    for h in range(nh):
        v_t = v_ref[0, sl(h), :]
        o = _softmax_pv_t(scores[2 * h], v_t) - lam * _softmax_pv_t(scores[2 * h + 1], v_t)
        o = o * lax.rsqrt(jnp.mean(o * o, 0, keepdims=True) + DF_EPS) * g_ref[...]
        outs.append(o * (1.0 - lambda_init))
    o_ref[0] = jnp.concatenate(outs, axis=0).T.astype(o_ref.dtype)


def _df_attn(q, k, v, p, lambda_init):
    b, s, _ = q.shape
    tq = 256
    nh = DF_HEADS_PER_STEP
    full = lambda a: pl.BlockSpec(a.shape, lambda bi, hi, qi: (0,) * a.ndim)
    consts = [p["lq1"], p["lk1"], p["lq2"], p["lk2"], p["subln"].reshape(-1, 1)]
    return pl.pallas_call(
        functools.partial(_df_attn_kernel, lambda_init=lambda_init),
        name="df_attn",
        grid=(b, DF_HEADS // nh, s // tq),
        in_specs=[pl.BlockSpec((1, tq, nh * LANES), lambda bi, hi, qi: (bi, qi, hi)),
                  pl.BlockSpec((1, s, nh * LANES), lambda bi, hi, qi: (bi, 0, hi)),
                  pl.BlockSpec((1, nh * LANES, s), lambda bi, hi, qi: (bi, hi, 0))]
                 + [full(a) for a in consts],
        out_specs=pl.BlockSpec((1, tq, nh * LANES), lambda bi, hi, qi: (bi, qi, hi)),
        out_shape=jax.ShapeDtypeStruct((b, s, DF_WIDTH), BF16),
        compiler_params=_cparams(("parallel", "parallel", "parallel")),
    )(q, k, v, *consts)


def _merge_kernel(orw, omla, odf, g0, g1, g2, w0, w1, w2, o_ref):
    acc = g0[...].astype(F32) * _dot(orw[...], w0[...])
    acc = acc + g1[...].astype(F32) * _dot(omla[...], w1[...])
    acc = acc + g2[...].astype(F32) * _dot(odf[...], w2[...])
    o_ref[...] = acc.astype(o_ref.dtype)


def _merge(o_rw, o_mla, o_df, gates, p):
    t = o_rw.shape[0]
    tm = _pick_tile(t, (1024, 512, 256))
    tn = 1024
    nj = D_MODEL // tn
    act = lambda w: pl.BlockSpec((tm, w), lambda i, j: (i, 0))
    gate = lambda br: pl.BlockSpec((tm, tn), lambda i, j: (i, j + br * nj))
    wt = lambda w: pl.BlockSpec((w, tn), lambda i, j: (0, j))
    return pl.pallas_call(
        _merge_kernel,
        name="merge",
        grid=(t // tm, nj),
        in_specs=[act(RW_WIDTH), act(MLA_WIDTH), act(DF_WIDTH), gate(0), gate(1), gate(2),
                  wt(RW_WIDTH), wt(MLA_WIDTH), wt(DF_WIDTH)],
        out_specs=pl.BlockSpec((tm, tn), lambda i, j: (i, j)),
        out_shape=jax.ShapeDtypeStruct((t, D_MODEL), BF16),
        compiler_params=_cparams(("parallel", "parallel")),
    )(o_rw, o_mla, o_df, gates, gates, gates, p["w_up_rw"], p["w_up_mla"], p["w_up_df"])


def _layer_norm(y, g, b):
    mu = jnp.mean(y, -1, keepdims=True)
    yc = y - mu
    var = jnp.mean(yc * yc, -1, keepdims=True)
    return yc * lax.rsqrt(var + LN_EPS) * g + b


def _wo_ln_kernel(m_ref, w_ref, x_ref, g_ref, b_ref, o_ref, ob_ref):
    y = ALPHA * x_ref[...] + _dot(m_ref[...], w_ref[...])
    out = _layer_norm(y, g_ref[...], b_ref[...])
    o_ref[...] = out
    ob_ref[...] = out.astype(BF16).reshape(ob_ref.shape)


def _wo_ln(merged, x, p):
    t = x.shape[0]
    tm = 256
    tile = pl.BlockSpec((tm, D_MODEL), lambda i: (i, 0))
    tile3 = pl.BlockSpec((tm, ROW_SUB, LANES), lambda i: (i, 0, 0))
    full = lambda a: pl.BlockSpec(a.shape, lambda i: (0,) * a.ndim)
    return pl.pallas_call(
        _wo_ln_kernel,
        name="wo_ln",
        grid=(t // tm,),
        in_specs=[tile, full(p["w_o"]), tile, full(p["ln1_g"]), full(p["ln1_b"])],
        out_specs=[tile, tile3],
        out_shape=[jax.ShapeDtypeStruct((t, D_MODEL), F32),
                   jax.ShapeDtypeStruct((t, ROW_SUB, LANES), BF16)],
        compiler_params=_cparams(("parallel",)),
    )(merged, p["w_o"], x, p["ln1_g"], p["ln1_b"])


ROUTER_TILE = 1024


def _router_kernel(x_ref, wh_ref, wl_ref, bias_ref, tri_ref, idx_o, wts_o, rank_o, cnt_o, cnt_scr):
    @pl.when(pl.program_id(0) == 0)
    def _():
        cnt_scr[...] = jnp.zeros_like(cnt_scr)

    xh, xl = _split_bf16(x_ref[...])
    wh, wl = wh_ref[...], wl_ref[...]
    logits = _dot_nt(wh, xh) + (_dot_nt(wh, xl) + _dot_nt(wl, xh))
    scores = jax.nn.sigmoid(logits)
    sel = scores + bias_ref[...]
    tm = sel.shape[1]

    def row(a, i):
        return a[i:i + 1, :]

    best = jnp.zeros((1, tm), jnp.int32)
    best_s = None
    for g in range(N_GROUPS):
        a, b, c, d = (row(sel, EXPERTS_PER_GROUP * g + j) for j in range(4))
        hi1, lo1 = jnp.maximum(a, b), jnp.minimum(a, b)
        hi2, lo2 = jnp.maximum(c, d), jnp.minimum(c, d)
        gs = jnp.maximum(hi1, hi2) + jnp.maximum(jnp.minimum(hi1, hi2), jnp.maximum(lo1, lo2))
        if g == 0:
            best_s = gs
        else:
            upd = gs > best_s
            best = jnp.where(upd, g, best)
            best_s = jnp.where(upd, gs, best_s)

    def pick(a, j):
        out = row(a, j)
        for g in range(1, N_GROUPS):
            out = jnp.where(best == g, row(a, EXPERTS_PER_GROUP * g + j), out)
        return out

    cand = [pick(sel, j) for j in range(EXPERTS_PER_GROUP)]
    csc = [pick(scores, j) for j in range(EXPERTS_PER_GROUP)]
    neg = jnp.float32(-jnp.inf)

    def argmax4(vals):
        bi, bv = jnp.zeros((1, tm), jnp.int32), vals[0]
        for j in range(1, EXPERTS_PER_GROUP):
            upd = vals[j] > bv
            bi = jnp.where(upd, j, bi)
            bv = jnp.where(upd, vals[j], bv)
        return bi

    i1 = argmax4(cand)
    i2 = argmax4([jnp.where(i1 == j, neg, cand[j]) for j in range(EXPERTS_PER_GROUP)])

    def take(vals, i):
        out = vals[0]
        for j in range(1, EXPERTS_PER_GROUP):
            out = jnp.where(i == j, vals[j], out)
        return out

    w1, w2 = take(csc, i1), take(csc, i2)
    tot = w1 + w2
    e1 = best * EXPERTS_PER_GROUP + i1
    e2 = best * EXPERTS_PER_GROUP + i2
    eid = lax.broadcasted_iota(jnp.int32, (N_EXPERTS, tm), 0)
    oh1 = eid == e1
    oh2 = eid == e2
    oh = (oh1 | oh2).astype(BF16)
    before = _dot(oh, tri_ref[...]) + cnt_scr[...][:, 0:1]
    r1 = jnp.sum(jnp.where(oh1, before, 0.0), 0, keepdims=True)
    r2 = jnp.sum(jnp.where(oh2, before, 0.0), 0, keepdims=True)
    idx_o[0:1, :] = e1
    idx_o[1:2, :] = e2
    wts_o[0:1, :] = w1 / tot
    wts_o[1:2, :] = w2 / tot
    rank_o[0:1, :] = r1.astype(jnp.int32)
    rank_o[1:2, :] = r2.astype(jnp.int32)
    new_cnt = cnt_scr[...] + jnp.sum(oh.astype(F32), 1, keepdims=True)
    cnt_scr[...] = new_cnt
    cnt_o[...] = new_cnt


def _router(x, p):
    t = x.shape[0]
    tm = ROUTER_TILE
    full = lambda a: pl.BlockSpec(a.shape, lambda i: (0,) * a.ndim)
    tok = pl.BlockSpec((TOP_K, tm), lambda i: (0, i))
    consts = [p["router_wt_hi"], p["router_wt_lo"], p["router_bias"], p["router_tri"]]
    return pl.pallas_call(
        _router_kernel,
        name="router",
        grid=(t // tm,),
        in_specs=[pl.BlockSpec((tm, D_MODEL), lambda i: (i, 0))] + [full(a) for a in consts],
        out_specs=[tok, tok, tok, pl.BlockSpec((N_EXPERTS, LANES), lambda i: (0, 0))],
        out_shape=[jax.ShapeDtypeStruct((TOP_K, t), jnp.int32), jax.ShapeDtypeStruct((TOP_K, t), F32),
                   jax.ShapeDtypeStruct((TOP_K, t), jnp.int32),
                   jax.ShapeDtypeStruct((N_EXPERTS, LANES), F32)],
        scratch_shapes=[pltpu.VMEM((N_EXPERTS, LANES), F32)],
        compiler_params=_cparams(("arbitrary",)),
    )(x, *consts)


DISPATCH_TILE = 512


def _dispatch_kernel(dest_ref, x_ref, init_ref, xs_ref, sem):
    del init_ref
    def copy(r, k):
        return pltpu.make_async_copy(x_ref.at[r], xs_ref.at[dest_ref[0, TOP_K * r + k]], sem)

    def start(r, carry):
        copy(r, 0).start()
        copy(r, 1).start()
        return carry

    def wait(r, carry):
        copy(r, 0).wait()
        copy(r, 1).wait()
        return carry

    lax.fori_loop(0, DISPATCH_TILE, start, 0, unroll=DMA_UNROLL)
    lax.fori_loop(0, DISPATCH_TILE, wait, 0, unroll=DMA_UNROLL)


def _dispatch(xb, dest, n_rows):
    t = xb.shape[0]
    nt = t // DISPATCH_TILE
    dest2 = dest.reshape(nt, 1, DISPATCH_TILE * TOP_K)
    init = jnp.zeros((n_rows, ROW_SUB, LANES), BF16)
    return pl.pallas_call(
        _dispatch_kernel,
        name="dispatch",
        grid=(nt,),
        in_specs=[pl.BlockSpec((None, 1, DISPATCH_TILE * TOP_K), lambda i: (i, 0, 0),
                               memory_space=pltpu.SMEM),
                  pl.BlockSpec((DISPATCH_TILE, ROW_SUB, LANES), lambda i: (i, 0, 0)),
                  pl.BlockSpec(memory_space=pl.ANY)],
        out_specs=pl.BlockSpec(memory_space=pl.ANY),
        out_shape=jax.ShapeDtypeStruct((n_rows, ROW_SUB, LANES), BF16),
        scratch_shapes=[pltpu.SemaphoreType.DMA(())],
        input_output_aliases={2: 0},
        compiler_params=_cparams(("arbitrary",)),
    )(dest2, xb, init)


def _ffn_kernel(te_ref, nu_ref, xs_ref, wg_ref, wu_ref, wd_ref, y_ref):
    del te_ref

    @pl.when(pl.program_id(0) < nu_ref[0])
    def _():
        xs = xs_ref[...].reshape(FFN_TILE, D_MODEL)
        h = jax.nn.silu(_dot(xs, wg_ref[0])) * _dot(xs, wu_ref[0])
        y_ref[...] = _dot(h.astype(BF16), wd_ref[0]).reshape(y_ref.shape)

    @pl.when(pl.program_id(0) >= nu_ref[0])
    def _():
        y_ref[...] = jnp.zeros_like(y_ref)


def _ffn(xs, tile_expert, n_used, p):
    n_rows = xs.shape[0]
    tm = FFN_TILE
    grid_spec = pltpu.PrefetchScalarGridSpec(
        num_scalar_prefetch=2,
        grid=(n_rows // tm,),
        in_specs=[pl.BlockSpec((tm, ROW_SUB, LANES), lambda i, te, nu: (i, 0, 0)),
                  pl.BlockSpec((1, D_MODEL, D_EXPERT), lambda i, te, nu: (te[i], 0, 0)),
                  pl.BlockSpec((1, D_MODEL, D_EXPERT), lambda i, te, nu: (te[i], 0, 0)),
                  pl.BlockSpec((1, D_EXPERT, D_MODEL), lambda i, te, nu: (te[i], 0, 0))],
        out_specs=pl.BlockSpec((tm, ROW_SUB, LANES), lambda i, te, nu: (i, 0, 0)),
    )
    return pl.pallas_call(
        _ffn_kernel,
        name="ffn",
        grid_spec=grid_spec,
        out_shape=jax.ShapeDtypeStruct((n_rows, ROW_SUB, LANES), F32),
        compiler_params=_cparams(("arbitrary",)),
    )(tile_expert, n_used, xs, p["ex_w_gate"], p["ex_w_up"], p["ex_w_down"])


COMBINE_TILE = 256


def _combine_ln_kernel(dest_ref, dest_next_ref, y_ref, x_ref, w_ref, g_ref, b_ref, o_ref, ob_ref,
                       buf, sem):
    i = pl.program_id(0)
    slot = i % 2

    def copy(idx_ref, s, r, k):
        return pltpu.make_async_copy(y_ref.at[idx_ref[0, TOP_K * r + k]], buf.at[s, k, r],
                                     sem.at[s])

    def issue(idx_ref, s):
        def body(r, carry):
            copy(idx_ref, s, r, 0).start()
            copy(idx_ref, s, r, 1).start()
            return carry
        lax.fori_loop(0, COMBINE_TILE, body, 0, unroll=DMA_UNROLL)

    @pl.when(i == 0)
    def _():
        issue(dest_ref, 0)

    @pl.when(i + 1 < pl.num_programs(0))
    def _():
        issue(dest_next_ref, 1 - slot)

    def wait(r, carry):
        copy(dest_ref, slot, r, 0).wait()
        copy(dest_ref, slot, r, 1).wait()
        return carry

    lax.fori_loop(0, COMBINE_TILE, wait, 0, unroll=DMA_UNROLL)
    w = w_ref[...]
    shape = (COMBINE_TILE, D_MODEL)
    ffn = w[:, 0:1] * buf[slot, 0].reshape(shape) + w[:, 1:2] * buf[slot, 1].reshape(shape)
    out = _layer_norm(ALPHA * x_ref[...] + ffn, g_ref[...], b_ref[...])
    o_ref[...] = out
    ob_ref[...] = out.astype(BF16)


def _combine_ln(y, dest, x, wts, p):
    t = x.shape[0]
    tm = COMBINE_TILE
    nt = t // tm
    dest2 = dest.reshape(nt, 1, tm * TOP_K)
    tile = pl.BlockSpec((tm, D_MODEL), lambda i: (i, 0))
    full = lambda a: pl.BlockSpec(a.shape, lambda i: (0,) * a.ndim)
    return pl.pallas_call(
        _combine_ln_kernel,
        name="combine_ln",
        grid=(nt,),
        in_specs=[pl.BlockSpec((None, 1, tm * TOP_K), lambda i: (i, 0, 0), memory_space=pltpu.SMEM),
                  pl.BlockSpec((None, 1, tm * TOP_K), lambda i: (jnp.minimum(i + 1, nt - 1), 0, 0),
                               memory_space=pltpu.SMEM),
                  pl.BlockSpec(memory_space=pl.ANY), tile,
                  pl.BlockSpec((tm, TOP_K), lambda i: (i, 0)), full(p["ln2_g"]), full(p["ln2_b"])],
        out_specs=[tile, tile],
        out_shape=[jax.ShapeDtypeStruct((t, D_MODEL), F32), jax.ShapeDtypeStruct((t, D_MODEL), BF16)],
        scratch_shapes=[pltpu.VMEM((2, TOP_K, tm, ROW_SUB, LANES), F32),
                        pltpu.SemaphoreType.DMA((2,))],
        compiler_params=_cparams(("arbitrary",)),
    )(dest2, dest2, y, x, wts, p["ln2_g"], p["ln2_b"])


def _moe_ln(x, xb, p):
    t = x.shape[0]
    idx, wts, rank, counts = _router(x, p)
    counts = counts[:, 0].astype(jnp.int32)
    tiles = (counts + FFN_TILE - 1) // FFN_TILE
    tile_end = jnp.cumsum(tiles)
    row_start = (tile_end - tiles) * FFN_TILE
    n_tiles = (t * TOP_K) // FFN_TILE + N_EXPERTS
    start_of = jnp.sum(jnp.where(idx[..., None] == jnp.arange(N_EXPERTS), row_start, 0), -1)
    dest = (start_of + rank).T.reshape(-1)
    tile_expert = jnp.minimum(jnp.sum(jnp.arange(n_tiles)[:, None] >= tile_end[None, :], axis=1),
                              N_EXPERTS - 1).astype(jnp.int32)
    n_used = tile_end[-1:].astype(jnp.int32)
    xs = _dispatch(xb, dest, n_tiles * FFN_TILE)
    y = _ffn(xs, tile_expert, n_used, p)
    return _combine_ln(y, dest, x, wts.T, p)


def _block_diag_ones(n, blk):
    i = jnp.arange(n) // blk
    return (i[:, None] == i[None, :]).astype(BF16)


def _prep_layer(l, w):
    row = lambda a: a.reshape(1, -1).astype(F32)
    w_in = w["w_in"][l]
    o1, o2, o3 = RW_IN, RW_IN + MLA_IN, RW_IN + MLA_IN + DF_IN
    p = {
        "w_in_rw": w_in[:, :o1].astype(BF16),
        "w_in_mla": jnp.pad(w_in[:, o1:o2], ((0, 0), (0, MLA_IN_PAD - MLA_IN))).astype(BF16),
        "w_in_df": w_in[:, o2:o3].astype(BF16),
        "w_in_gate": w_in[:, o3:].astype(BF16),
        "mu_prev": row(w["shift_prev"][l]), "mu_next": row(w["shift_next"][l]),
        "w0": row(w["rw_w0"][l]), "a0": row(w["rw_a0"][l]),
        "g2": w["rw_g2"][l].astype(BF16),
        "k_k": row(w["rw_k_k"][l]), "k_a": row(w["rw_k_a"][l]), "r_k": row(w["rw_r_k"][l]),
        "lnx_g": row(w["rw_lnx_g"][l]), "lnx_b": row(w["rw_lnx_b"][l]),
        "bd": _block_diag_ones(RW_WIDTH, RW_HEAD_DIM),
        "q_norm": row(w["mla_q_norm"][l]), "kv_norm": row(w["mla_kv_norm"][l]),
        "lq1": row(w["df_lq1"][l]), "lk1": row(w["df_lk1"][l]),
        "lq2": row(w["df_lq2"][l]), "lk2": row(w["df_lk2"][l]),
        "subln": row(w["df_subln"][l]),
        "w_up_rw": w["w_up_rw"][l].astype(BF16), "w_up_mla": w["w_up_mla"][l].astype(BF16),
        "w_up_df": w["w_up_df"][l].astype(BF16), "w_o": w["w_o"][l].astype(BF16),
        "ln1_g": row(w["ln1_g"][l]), "ln1_b": row(w["ln1_b"][l]),
        "ln2_g": row(w["ln2_g"][l]), "ln2_b": row(w["ln2_b"][l]),
        "ex_w_gate": w["ex_w_gate"][l].astype(BF16), "ex_w_up": w["ex_w_up"][l].astype(BF16),
        "ex_w_down": w["ex_w_down"][l].astype(BF16),
    }
    zc = jnp.zeros((DECAY_LORA, RW_WIDTH), F32)
    w2 = w["rw_w2"][l]
    a2 = w["rw_a2"][l]
    p["w2cat"] = jnp.block([[w2[0], zc], [zc, w2[1]]]).astype(BF16)
    p["a2cat"] = jnp.block([[a2[0], zc], [zc, a2[1]]]).astype(BF16)
    wq = w["mla_w_uq"][l].reshape(MLA_Q_LORA, MLA_HEADS, MLA_NOPE + MLA_ROPE)
    p["wq"] = jnp.pad(wq, ((0, 0), (0, 0), (0, MLA_HEAD_PAD - MLA_NOPE - MLA_ROPE))
                      ).reshape(MLA_Q_LORA, -1).astype(BF16)
    wkv = w["mla_w_ukv"][l].reshape(MLA_KV_LORA, MLA_HEADS, MLA_NOPE + MLA_V)
    wk_nope = jnp.pad(wkv[:, :, :MLA_NOPE], ((0, 0), (0, 0), (0, MLA_HEAD_PAD - MLA_NOPE)))
    place = jnp.zeros((LANES, MLA_HEADS, MLA_HEAD_PAD), F32)
    j = jnp.arange(MLA_ROPE)
    place = place.at[j, :, MLA_NOPE + j].set(1.0)
    p["wk"] = jnp.concatenate([wk_nope, place], axis=0).reshape(MLA_KV_LORA + LANES, -1).astype(BF16)
    p["wv"] = wkv[:, :, MLA_NOPE:].reshape(MLA_KV_LORA, -1).T.astype(BF16)
    return p


def _trunk(x3, layers, shared):
    b, s, d = x3.shape
    t = b * s
    tabs = {
        "mla_q": _rope_tables(s, MLA_ROPE, MLA_NOPE, LANES),
        "mla_k": _rope_tables(s, MLA_ROPE, 0, LANES),
        "df": _rope_tables(s, DF_HEAD_DIM, 0, DF_HEAD_DIM),
    }
    x = x3.reshape(t, d)
    xb = x.astype(BF16)
    for l, p in enumerate(layers):
        p = dict(p, **shared)
        lambda_init = 0.8 - 0.6 * math.exp(-0.3 * l)
        z_rw = _matmul(xb, p["w_in_rw"], F32).reshape(b, s, -1)
        z_mla = _matmul(xb, p["w_in_mla"], F32).reshape(b, s, -1)
        z_df = _matmul(xb, p["w_in_df"], F32).reshape(b, s, -1)
        gates = _matmul(xb, p["w_in_gate"], BF16, act="sigmoid")
        r, v, kk, g, bonus, logd, a, kdir = _rw_prep(z_rw, p)
        yf, yb = _rw_scan(r, v, kk, logd, a, kdir)
        o_rw = _rw_post(yf.reshape(t, -1), yb.reshape(t, -1), bonus.reshape(t, -1),
                        g.reshape(t, -1), p)
        q, k, v2 = _mla_prep(z_mla, p, tabs)
        o_mla = _mla_attn(q, k, v2).reshape(t, -1)
        q, k, v2 = _df_prep(z_df, tabs)
        o_df = _df_attn(q, k, v2, p, lambda_init).reshape(t, -1)
        merged = _merge(o_rw, o_mla, o_df, gates, p)
        x, xb = _wo_ln(merged, x, p)
        x, xb = _moe_ln(x, xb, p)
    return x.reshape(b, s, d)


def kernel(x_prompt, x_sample, w_in, shift_prev, shift_next, rw_w0, rw_w2, rw_a0, rw_a2, rw_g2,
           rw_k_k, rw_k_a, rw_r_k, rw_lnx_g, rw_lnx_b, mla_q_norm, mla_kv_norm, mla_w_uq,
           mla_w_ukv, df_lq1, df_lk1, df_lq2, df_lk2, df_subln, w_up_rw, w_up_mla, w_up_df, w_o,
           ln1_g, ln1_b, ln2_g, ln2_b, router_w, router_bias, ex_w_gate, ex_w_up, ex_w_down):
    w = dict(w_in=w_in, shift_prev=shift_prev, shift_next=shift_next, rw_w0=rw_w0, rw_w2=rw_w2,
             rw_a0=rw_a0, rw_a2=rw_a2, rw_g2=rw_g2, rw_k_k=rw_k_k, rw_k_a=rw_k_a, rw_r_k=rw_r_k,
             rw_lnx_g=rw_lnx_g, rw_lnx_b=rw_lnx_b, mla_q_norm=mla_q_norm,
             mla_kv_norm=mla_kv_norm, mla_w_uq=mla_w_uq, mla_w_ukv=mla_w_ukv, df_lq1=df_lq1,
             df_lk1=df_lk1, df_lq2=df_lq2, df_lk2=df_lk2, df_subln=df_subln, w_up_rw=w_up_rw,
             w_up_mla=w_up_mla, w_up_df=w_up_df, w_o=w_o, ln1_g=ln1_g, ln1_b=ln1_b, ln2_g=ln2_g,
             ln2_b=ln2_b, ex_w_gate=ex_w_gate, ex_w_up=ex_w_up, ex_w_down=ex_w_down)
    layers = [_prep_layer(l, w) for l in range(DEPTH)]
    rwt = router_w.T.astype(F32)
    rwt_hi = rwt.astype(BF16)
    ti = jnp.arange(ROUTER_TILE)
    shared = {
        "router_wt_hi": rwt_hi,
        "router_wt_lo": (rwt - rwt_hi.astype(F32)).astype(BF16),
        "router_bias": router_bias.reshape(N_EXPERTS, 1).astype(F32),
        "router_tri": (ti[:, None] < ti[None, :]).astype(BF16),
    }
    return (_trunk(x_prompt, layers, shared), _trunk(x_sample, layers, shared))
```

```python
import functools
import math

import jax
import jax.numpy as jnp
from jax import lax
from jax.experimental import pallas as pl
from jax.experimental.pallas import tpu as pltpu

F32 = jnp.float32
BF16 = jnp.bfloat16

D_MODEL = 2048
DEPTH = 2
RW_HEADS, RW_HEAD_DIM = 12, 64
RW_WIDTH = RW_HEADS * RW_HEAD_DIM
DECAY_LORA, ICLR_LORA, GATE_LORA, N_DIR = 64, 64, 128, 2
GN_EPS = 64e-5
MLA_HEADS, MLA_NOPE, MLA_ROPE, MLA_V = 8, 64, 32, 64
MLA_Q_LORA, MLA_KV_LORA = 512, 256
MLA_WIDTH = MLA_HEADS * MLA_V
DF_HEADS, DF_HEAD_DIM = 6, 64
DF_WIDTH = DF_HEADS * 2 * DF_HEAD_DIM
DF_EPS = 1e-5
N_BRANCH = 3
N_EXPERTS, N_GROUPS, TOP_K, D_EXPERT = 16, 4, 2, 1024
EXPERTS_PER_GROUP = N_EXPERTS // N_GROUPS
ROPE_THETA = 10000.0
LN_EPS = 1e-5
RMS_EPS = 1e-6
ALPHA = (2 * DEPTH) ** 0.25
RW_IN = 3 * RW_WIDTH + N_DIR * DECAY_LORA + N_DIR * ICLR_LORA + GATE_LORA
MLA_IN = MLA_Q_LORA + MLA_KV_LORA + MLA_ROPE
DF_IN = 3 * DF_WIDTH
GATE_IN = N_BRANCH * D_MODEL

LANES = 128
MLA_IN_PAD = 896
MLA_HEAD_PAD = 128
RW_CHUNK = 64
RW_PAIRS = RW_WIDTH // LANES
VMEM_LIMIT = 56 * 1024 * 1024
FFN_TILE = 512
LOG2E = math.log2(math.e)
MLA_HEADS_PER_STEP = 4
DF_HEADS_PER_STEP = 2
DMA_UNROLL = 8
HALO = 8
ROW_SUB = D_MODEL // LANES


def _cparams(sem):
    return pltpu.CompilerParams(dimension_semantics=sem, vmem_limit_bytes=VMEM_LIMIT)


def _dot(a, b):
    return jnp.dot(a, b, preferred_element_type=F32)


def _dot_nt(a, b):
    return lax.dot_general(a, b, (((1,), (1,)), ((), ())), preferred_element_type=F32)


def _dot_tn(a, b):
    return lax.dot_general(a, b, (((0,), (0,)), ((), ())), preferred_element_type=F32)


def _split_bf16(x):
    hi = x.astype(BF16)
    lo = (x - hi.astype(F32)).astype(BF16)
    return hi, lo


def _pick_tile(n, candidates):
    for c in candidates:
        if n % c == 0:
            return c
    raise ValueError(f"no tile for {n}")


def _mm_kernel(x_ref, w_ref, o_ref, *, act):
    acc = _dot(x_ref[...], w_ref[...])
    if act == "sigmoid":
        acc = jax.nn.sigmoid(acc)
    o_ref[...] = acc.astype(o_ref.dtype)


def _matmul(x, w, out_dtype, act=None):
    m, k = x.shape
    n = w.shape[1]
    tm = _pick_tile(m, (1024, 512, 256, 128))
    tn = _pick_tile(n, (1024, 896, 768, 512, 256, 128))
    return pl.pallas_call(
        functools.partial(_mm_kernel, act=act),
        name="mm",
        grid=(n // tn, m // tm),
        in_specs=[pl.BlockSpec((tm, k), lambda j, i: (i, 0)),
                  pl.BlockSpec((k, tn), lambda j, i: (0, j))],
        out_specs=pl.BlockSpec((tm, tn), lambda j, i: (i, j)),
        out_shape=jax.ShapeDtypeStruct((m, n), out_dtype),
        compiler_params=_cparams(("parallel", "parallel")),
    )(x, w)


def _rw_prep_kernel(z_ref, hp_ref, hn_ref, mup_ref, mun_ref, w2_ref, a2_ref, g2_ref, w0_ref,
                    a0_ref, kk_k_ref, k_a_ref, r_k_ref, bd_ref,
                    r_o, v_o, kk_o, g_o, bonus_o, logd_o, a_o, kdir_o):
    z = z_ref[0]
    tr = z.shape[0]
    row = lax.broadcasted_iota(jnp.int32, (tr, 1), 0)
    ti = pl.program_id(1)
    halo_prev = jnp.where(ti == 0, 0.0, hp_ref[0, HALO - 1:HALO, :])
    halo_next = jnp.where(ti == pl.num_programs(1) - 1, 0.0, hn_ref[0, 0:1, :])
    prev = jnp.where(row == 0, halo_prev, pltpu.roll(z, 1, 0))
    nxt = jnp.where(row == tr - 1, halo_next, pltpu.roll(z, tr - 1, 0))
    zs = z + mup_ref[...] * (prev - z) + mun_ref[...] * (nxt - z)
    c = RW_WIDTH
    r, k, v = zs[:, 0:c], zs[:, c:2 * c], zs[:, 2 * c:3 * c]
    wl = zs[:, 3 * c:3 * c + 128]
    al = zs[:, 3 * c + 128:3 * c + 256]
    gl = zs[:, 3 * c + 256:3 * c + 384]
    w_raw = w0_ref[...] + _dot(jnp.tanh(wl).astype(BF16), w2_ref[...])
    a = jax.nn.sigmoid(a0_ref[...] + _dot(al.astype(BF16), a2_ref[...]))
    g = _dot(jax.nn.sigmoid(gl).astype(BF16), g2_ref[...])
    logd = (-math.exp(-0.5)) * jax.nn.sigmoid(w_raw)
    bd = bd_ref[...]

    def head_sum(x):
        hi, lo = _split_bf16(x)
        return _dot(hi, bd) + _dot(lo, bd)

    kk = k * kk_k_ref[...]
    kk = kk * lax.rsqrt(head_sum(kk * kk) + 1e-12)
    k_a = k_a_ref[...]
    kd0 = k * (1.0 + (a[:, 0:c] - 1.0) * k_a)
    kd1 = k * (1.0 + (a[:, c:2 * c] - 1.0) * k_a)
    bonus = head_sum(r * r_k_ref[...] * (kd0 + kd1)) * v
    r_o[0] = r
    v_o[0] = v
    kk_o[0] = kk
    g_o[0] = g
    bonus_o[0] = bonus
    logd_o[0] = logd
    a_o[0] = a
    kdir_o[0, :, 0:c] = kd0
    kdir_o[0, :, c:2 * c] = kd1


def _rw_prep(z_rw, p):
    b, s, _ = z_rw.shape
    tr = 256
    nt = s // tr
    hpt = tr // HALO
    c = RW_WIDTH
    tile = lambda w: pl.BlockSpec((1, tr, w), lambda bi, ti: (bi, ti, 0))
    halo_prev = pl.BlockSpec((1, HALO, RW_IN), lambda bi, ti: (bi, jnp.maximum(ti * hpt - 1, 0), 0))
    halo_next = pl.BlockSpec((1, HALO, RW_IN),
                             lambda bi, ti: (bi, jnp.minimum((ti + 1) * hpt, s // HALO - 1), 0))
    full = lambda a: pl.BlockSpec(a.shape, lambda bi, ti: (0,) * a.ndim)
    consts = [p["mu_prev"], p["mu_next"], p["w2cat"], p["a2cat"], p["g2"], p["w0"], p["a0"],
              p["k_k"], p["k_a"], p["r_k"], p["bd"]]
    out_w = [c, c, c, c, c, 2 * c, 2 * c, 2 * c]
    return pl.pallas_call(
        _rw_prep_kernel,
        name="rw_prep",
        grid=(b, nt),
        in_specs=[tile(RW_IN), halo_prev, halo_next] + [full(a) for a in consts],
        out_specs=[tile(w) for w in out_w],
        out_shape=[jax.ShapeDtypeStruct((b, s, w), F32) for w in out_w],
        compiler_params=_cparams(("parallel", "parallel")),
    )(z_rw, z_rw, z_rw, *consts)


def _rw_scan_kernel(rf, vf, kkf, ldf, af, kdf, rb, vb, kkb, ldb, ab, kdb, yf_o, yb_o, z_scr):
    @pl.when(pl.program_id(1) == 0)
    def _():
        z_scr[...] = jnp.zeros_like(z_scr)

    c = RW_CHUNK
    ti = lax.broadcasted_iota(jnp.int32, (c, c), 0)
    tj = lax.broadcasted_iota(jnp.int32, (c, c), 1)
    lane = lax.broadcasted_iota(jnp.int32, (1, LANES), 1)
    lane_m = (lane < RW_HEAD_DIM, lane >= RW_HEAD_DIM)
    bi = lax.broadcasted_iota(jnp.int32, (LANES, LANES), 0) // RW_HEAD_DIM
    bj = lax.broadcasted_iota(jnp.int32, (LANES, LANES), 1) // RW_HEAD_DIM
    bdmask = bi == bj
    incl = ((ti >= tj), (ti <= tj))
    strict = ((ti > tj), (ti < tj))
    incl_bf = tuple(m.astype(BF16) for m in incl)
    t2 = lax.broadcasted_iota(jnp.int32, (2 * c, 2 * c), 0)
    j2 = lax.broadcasted_iota(jnp.int32, (2 * c, 2 * c), 1) & (c - 1)
    diag_ok = (t2 >= c) & ((t2 & (c - 1)) == j2)
    t2 = t2 & (c - 1)
    mask_ar = ((t2 > j2) | diag_ok, (t2 < j2) | diag_ok)
    ones = jnp.ones((c, LANES), BF16)
    in_refs = ((rf, vf, kkf, ldf, af, kdf), (rb, vb, kkb, ldb, ab, kdb))
    out_refs = (yf_o, yb_o)

    def bd_rows(x):
        return jnp.concatenate([jnp.where(lane_m[0], x, 0.0), jnp.where(lane_m[1], x, 0.0)],
                               axis=0).astype(BF16)

    probs = [(d, p) for d in range(N_DIR) for p in range(RW_PAIRS)]
    sl = lambda p: slice(p * LANES, (p + 1) * LANES)
    val = {s: [ref[0, :, sl(s[1])] for ref in in_refs[s[0]]] for s in probs}
    ld_split = {s: _split_bf16(val[s][3]) for s in probs}
    logp = {s: _dot(incl_bf[s[0]], ld_split[s][0]) + _dot(incl_bf[s[0]], ld_split[s][1])
            for s in probs}
    logpc = {s: _dot_tn(ld_split[s][0], ones) + _dot_tn(ld_split[s][1], ones)
             for s in probs}
    at, rt, bt, kt, vv, z, zb = {}, {}, {}, {}, {}, {}, {}
    for s in probs:
        r, v, kk, ld, a, kd = val[s]
        pinv = jnp.exp(-logp[s])
        at[s] = -(kk * jnp.exp(logp[s] - ld))
        rt[s] = r * jnp.exp(logp[s])
        bt[s] = kk * a * pinv
        kt[s] = kd * pinv
        vv[s] = v
        z[s] = z_scr[s[0], s[1]]
        zb[s] = z[s].astype(BF16)
    lhs = {s: jnp.concatenate([at[s], rt[s]], axis=0).astype(BF16) for s in probs}
    gb = {s: _dot_nt(lhs[s], bd_rows(bt[s])) for s in probs}
    gk = {s: _dot_nt(lhs[s], bd_rows(kt[s])) for s in probs}
    a_kk = {s: jnp.where(mask_ar[s[0]], gk[s], 0.0).astype(BF16) for s in probs}
    a_bb = {s: jnp.where(mask_ar[s[0]], gb[s], 0.0).astype(BF16) for s in probs}
    zv = {s: _dot(lhs[s], zb[s]) + _dot(a_kk[s], bd_rows(vv[s])) for s in probs}
    x = {s: zv[s][0:c] for s in probs}
    ap = {s: a_bb[s][0:c] for s in probs}
    for i in range(6):
        if i < 5:
            prod = {s: _dot(ap[s], jnp.concatenate([bd_rows(x[s]), bd_rows(ap[s])], axis=1))
                    for s in probs}
            x = {s: x[s] + prod[s][:, 0:LANES] for s in probs}
            ap = {s: prod[s][:, LANES:].astype(BF16) for s in probs}
        else:
            x = {s: x[s] + _dot(ap[s], bd_rows(x[s])) for s in probs}
    for s in probs:
        d, p = s
        out_refs[d][0, :, sl(p)] = zv[s][c:] + _dot(a_bb[s][c:], bd_rows(x[s]))
        upd = _dot_tn(jnp.concatenate([bt[s], kt[s]], axis=0).astype(BF16),
                      jnp.concatenate([x[s], vv[s]], axis=0).astype(BF16))
        z_scr[d, p] = jnp.where(bdmask, jnp.exp(logpc[s]) * (z[s] + upd), 0.0)


def _rw_scan(r, v, kk, logd, a, kdir):
    b, s, c = r.shape
    nc = s // RW_CHUNK
    fwd = pl.BlockSpec((1, RW_CHUNK, c), lambda bi, ci: (bi, ci, 0))
    bwd = pl.BlockSpec((1, RW_CHUNK, c), lambda bi, ci: (bi, nc - 1 - ci, 0))
    bwd_dir = pl.BlockSpec((1, RW_CHUNK, c), lambda bi, ci: (bi, nc - 1 - ci, 1))
    return pl.pallas_call(
        _rw_scan_kernel,
        name="rw_scan",
        grid=(b, nc),
        in_specs=[fwd, fwd, fwd, fwd, fwd, fwd, bwd, bwd, bwd, bwd_dir, bwd_dir, bwd_dir],
        out_specs=[fwd, bwd],
        out_shape=[jax.ShapeDtypeStruct((b, s, c), F32)] * 2,
        scratch_shapes=[pltpu.VMEM((N_DIR, RW_PAIRS, LANES, LANES), F32)],
        compiler_params=_cparams(("parallel", "arbitrary")),
    )(r, v, kk, logd, a, kdir, r, v, kk, logd, a, kdir)


def _rw_post_kernel(yf_ref, yb_ref, bonus_ref, g_ref, lg_ref, lb_ref, bd_ref, o_ref):
    y = yf_ref[...] + yb_ref[...]
    bd = bd_ref[...]

    def head_mean(x):
        hi, lo = _split_bf16(x)
        return (_dot(hi, bd) + _dot(lo, bd)) * (1.0 / RW_HEAD_DIM)

    mu = head_mean(y)
    yc = y - mu
    var = head_mean(yc * yc)
    yn = yc * lax.rsqrt(var + GN_EPS) * lg_ref[...] + lb_ref[...]
    o_ref[...] = ((yn + bonus_ref[...]) * g_ref[...]).astype(o_ref.dtype)


def _rw_post(yf, yb, bonus, g, p):
    t, c = yf.shape
    tm = _pick_tile(t, (1024, 512, 256))
    tile = pl.BlockSpec((tm, c), lambda i: (i, 0))
    full = lambda a: pl.BlockSpec(a.shape, lambda i: (0,) * a.ndim)
    consts = [p["lnx_g"], p["lnx_b"], p["bd"]]
    return pl.pallas_call(
        _rw_post_kernel,
        name="rw_post",
        grid=(t // tm,),
        in_specs=[tile] * 4 + [full(a) for a in consts],
        out_specs=tile,
        out_shape=jax.ShapeDtypeStruct((t, c), BF16),
        compiler_params=_cparams(("parallel",)),
    )(yf, yb, bonus, g, *consts)


def _rope_block(x, cos, s_up, s_dn, half):
    return x * cos + pltpu.roll(x, half, 1) * s_up + pltpu.roll(x, LANES - half, 1) * s_dn


def _rope_tables(s, dim, lane_of_x1, period):
    half = dim // 2
    inv_freq = jnp.power(ROPE_THETA, -jnp.arange(half, dtype=F32) * (2.0 / dim))
    ang = jnp.arange(s, dtype=F32)[:, None] * inv_freq[None, :]
    lane = jnp.arange(LANES) % period - lane_of_x1
    in_x1 = (lane >= 0) & (lane < half)
    in_x2 = (lane >= half) & (lane < dim)
    j = jnp.clip(jnp.where(in_x2, lane - half, lane), 0, half - 1)
    cos = jnp.cos(ang)[:, j]
    sin = jnp.sin(ang)[:, j]
    cos_t = jnp.where(in_x1 | in_x2, cos, 1.0)
    s_up = jnp.where(in_x2, sin, 0.0)
    s_dn = jnp.where(in_x1, -sin, 0.0)
    return cos_t.astype(F32), s_up.astype(F32), s_dn.astype(F32)


def _mla_prep_kernel(z_ref, qg_ref, kvg_ref, wq_ref, wk_ref, wv_ref,
                     cq_ref, squ_ref, sqd_ref, ck_ref, sku_ref, skd_ref, q_o, k_o, v_o):
    z = z_ref[0]
    c_q = z[:, 0:MLA_Q_LORA]
    c_kv = z[:, MLA_Q_LORA:MLA_Q_LORA + MLA_KV_LORA]
    kr = z[:, MLA_Q_LORA + MLA_KV_LORA:MLA_IN_PAD]
    c_q = c_q * lax.rsqrt(jnp.mean(c_q * c_q, -1, keepdims=True) + RMS_EPS) * qg_ref[...]
    c_kv = c_kv * lax.rsqrt(jnp.mean(c_kv * c_kv, -1, keepdims=True) + RMS_EPS) * kvg_ref[...]
    q = _dot(c_q.astype(BF16), wq_ref[...])
    scale = (MLA_NOPE + MLA_ROPE) ** -0.5 * LOG2E
    cq, squ, sqd = cq_ref[...], squ_ref[...], sqd_ref[...]
    for h in range(MLA_HEADS):
        sl = slice(h * MLA_HEAD_PAD, (h + 1) * MLA_HEAD_PAD)
        q_o[0, :, sl] = (_rope_block(q[:, sl], cq, squ, sqd, MLA_ROPE // 2) * scale).astype(BF16)
    kr = _rope_block(kr, ck_ref[...], sku_ref[...], skd_ref[...], MLA_ROPE // 2)
    ckv_b = c_kv.astype(BF16)
    k_in = jnp.concatenate([ckv_b, kr.astype(BF16)], axis=1)
    k_o[0] = _dot(k_in, wk_ref[...]).astype(BF16)
    v_o[0] = _dot_nt(wv_ref[...], ckv_b).astype(BF16)


def _mla_prep(z_mla, p, tabs):
    b, s, _ = z_mla.shape
    tr = 256
    tile = lambda w: pl.BlockSpec((1, tr, w), lambda bi, ti: (bi, ti, 0))
    full = lambda a: pl.BlockSpec(a.shape, lambda bi, ti: (0,) * a.ndim)
    tab = pl.BlockSpec((tr, LANES), lambda bi, ti: (ti, 0))
    consts = [p["q_norm"], p["kv_norm"], p["wq"], p["wk"], p["wv"]]
    hq = MLA_HEADS * MLA_HEAD_PAD
    return pl.pallas_call(
        _mla_prep_kernel,
        name="mla_prep",
        grid=(b, s // tr),
        in_specs=[tile(MLA_IN_PAD)] + [full(a) for a in consts] + [tab] * 6,
        out_specs=[tile(hq), tile(hq),
                   pl.BlockSpec((1, MLA_WIDTH, tr), lambda bi, ti: (bi, 0, ti))],
        out_shape=[jax.ShapeDtypeStruct((b, s, hq), BF16), jax.ShapeDtypeStruct((b, s, hq), BF16),
                   jax.ShapeDtypeStruct((b, MLA_WIDTH, s), BF16)],
        compiler_params=_cparams(("parallel", "parallel")),
    )(z_mla, *consts, *tabs["mla_q"], *tabs["mla_k"])


def _softmax_pv_t(s_t, v_t):
    pr = jnp.exp2(s_t - jnp.max(s_t, 0, keepdims=True))
    l = jnp.sum(pr, 0, keepdims=True)
    return _dot(v_t, pr.astype(BF16)) * (1.0 / l)


def _mla_attn_kernel(q_ref, k_ref, v_ref, o_ref):
    nh = MLA_HEADS_PER_STEP
    sl = lambda h: slice(h * MLA_HEAD_PAD, (h + 1) * MLA_HEAD_PAD)
    scores = [_dot_nt(k_ref[0, :, sl(h)], q_ref[0, :, sl(h)]) for h in range(nh)]
    outs = []
    for h, s_t in enumerate(scores):
        pair = h // 2
        o_t = _softmax_pv_t(s_t, v_ref[0, pair * LANES:(pair + 1) * LANES, :])
        outs.append(o_t[(h % 2) * MLA_V:(h % 2 + 1) * MLA_V])
    o_ref[0] = jnp.concatenate(outs, axis=0).T.astype(o_ref.dtype)


def _mla_attn(q, k, v):
    b, s, _ = q.shape
    tq = 256
    nh = MLA_HEADS_PER_STEP
    return pl.pallas_call(
        _mla_attn_kernel,
        name="mla_attn",
        grid=(b, MLA_HEADS // nh, s // tq),
        in_specs=[pl.BlockSpec((1, tq, nh * MLA_HEAD_PAD), lambda bi, pi, qi: (bi, qi, pi)),
                  pl.BlockSpec((1, s, nh * MLA_HEAD_PAD), lambda bi, pi, qi: (bi, 0, pi)),
                  pl.BlockSpec((1, nh * MLA_V, s), lambda bi, pi, qi: (bi, pi, 0))],
        out_specs=pl.BlockSpec((1, tq, nh * MLA_V), lambda bi, pi, qi: (bi, qi, pi)),
        out_shape=jax.ShapeDtypeStruct((b, s, MLA_WIDTH), BF16),
        compiler_params=_cparams(("parallel", "parallel", "parallel")),
    )(q, k, v)


def _df_prep_kernel(z_ref, c_ref, su_ref, sd_ref, q_o, k_o, v_o):
    cos, s_up, s_dn = c_ref[...], su_ref[...], sd_ref[...]
    scale = DF_HEAD_DIM ** -0.5 * LOG2E
    for h in range(DF_HEADS):
        sl = slice(h * LANES, (h + 1) * LANES)
        q_o[0, :, sl] = (_rope_block(z_ref[0, :, sl], cos, s_up, s_dn, DF_HEAD_DIM // 2)
                         * scale).astype(BF16)
        slk = slice(DF_WIDTH + h * LANES, DF_WIDTH + (h + 1) * LANES)
        k_o[0, :, sl] = _rope_block(z_ref[0, :, slk], cos, s_up, s_dn,
                                    DF_HEAD_DIM // 2).astype(BF16)
    v_o[0] = z_ref[0, :, 2 * DF_WIDTH:3 * DF_WIDTH].T.astype(BF16)


def _df_prep(z_df, tabs):
    b, s, _ = z_df.shape
    tr = 256
    tile = lambda w: pl.BlockSpec((1, tr, w), lambda bi, ti: (bi, ti, 0))
    tab = pl.BlockSpec((tr, LANES), lambda bi, ti: (ti, 0))
    return pl.pallas_call(
        _df_prep_kernel,
        name="df_prep",
        grid=(b, s // tr),
        in_specs=[tile(DF_IN)] + [tab] * 3,
        out_specs=[tile(DF_WIDTH), tile(DF_WIDTH),
                   pl.BlockSpec((1, DF_WIDTH, tr), lambda bi, ti: (bi, 0, ti))],
        out_shape=[jax.ShapeDtypeStruct((b, s, DF_WIDTH), BF16)] * 2
                  + [jax.ShapeDtypeStruct((b, DF_WIDTH, s), BF16)],
        compiler_params=_cparams(("parallel", "parallel")),
    )(z_df, *tabs["df"])


def _df_attn_kernel(q_ref, k_ref, v_ref, lq1, lk1, lq2, lk2, g_ref, o_ref, *, lambda_init):
    lam = (jnp.exp(jnp.sum(lq1[...] * lk1[...], -1, keepdims=True))
           - jnp.exp(jnp.sum(lq2[...] * lk2[...], -1, keepdims=True)) + lambda_init)
    nh = DF_HEADS_PER_STEP
    lane = lax.broadcasted_iota(jnp.int32, (1, LANES), 1)
    masks = ((lane < DF_HEAD_DIM), (lane >= DF_HEAD_DIM))
    sl = lambda h: slice(h * LANES, (h + 1) * LANES)
    scores = [_dot_nt(k_ref[0, :, sl(h)], jnp.where(m, q_ref[0, :, sl(h)], 0.0).astype(BF16))
              for h in range(nh) for m in masks]
    outs = []
    for h in range(nh):
        v_t = v_ref[0, sl(h), :]
        o = _softmax_pv_t(scores[2 * h], v_t) - lam * _softmax_pv_t(scores[2 * h + 1], v_t)
        o = o * lax.rsqrt(jnp.mean(o * o, 0, keepdims=True) + DF_EPS) * g_ref[...]
        outs.append(o * (1.0 - lambda_init))
    o_ref[0] = jnp.concatenate(outs, axis=0).T.astype(o_ref.dtype)


def _df_attn(q, k, v, p, lambda_init):
    b, s, _ = q.shape
    tq = 256
    nh = DF_HEADS_PER_STEP
    full = lambda a: pl.BlockSpec(a.shape, lambda bi, hi, qi: (0,) * a.ndim)
    consts = [p["lq1"], p["lk1"], p["lq2"], p["lk2"], p["subln"].reshape(-1, 1)]
    return pl.pallas_call(
        functools.partial(_df_attn_kernel, lambda_init=lambda_init),
        name="df_attn",
        grid=(b, DF_HEADS // nh, s // tq),
        in_specs=[pl.BlockSpec((1, tq, nh * LANES), lambda bi, hi, qi: (bi, qi, hi)),
                  pl.BlockSpec((1, s, nh * LANES), lambda bi, hi, qi: (bi, 0, hi)),
                  pl.BlockSpec((1, nh * LANES, s), lambda bi, hi, qi: (bi, hi, 0))]
                 + [full(a) for a in consts],
        out_specs=pl.BlockSpec((1, tq, nh * LANES), lambda bi, hi, qi: (bi, qi, hi)),
        out_shape=jax.ShapeDtypeStruct((b, s, DF_WIDTH), BF16),
        compiler_params=_cparams(("parallel", "parallel", "parallel")),
    )(q, k, v, *consts)


def _merge_kernel(orw, omla, odf, g0, g1, g2, w0, w1, w2, o_ref):
    acc = g0[...].astype(F32) * _dot(orw[...], w0[...])
    acc = acc + g1[...].astype(F32) * _dot(omla[...], w1[...])
    acc = acc + g2[...].astype(F32) * _dot(odf[...], w2[...])
    o_ref[...] = acc.astype(o_ref.dtype)


def _merge(o_rw, o_mla, o_df, gates, p):
    t = o_rw.shape[0]
    tm = _pick_tile(t, (1024, 512, 256))
    tn = 1024
    nj = D_MODEL // tn
    act = lambda w: pl.BlockSpec((tm, w), lambda i, j: (i, 0))
    gate = lambda br: pl.BlockSpec((tm, tn), lambda i, j: (i, j + br * nj))
    wt = lambda w: pl.BlockSpec((w, tn), lambda i, j: (0, j))
    return pl.pallas_call(
        _merge_kernel,
        name="merge",
        grid=(t // tm, nj),
        in_specs=[act(RW_WIDTH), act(MLA_WIDTH), act(DF_WIDTH), gate(0), gate(1), gate(2),
                  wt(RW_WIDTH), wt(MLA_WIDTH), wt(DF_WIDTH)],
        out_specs=pl.BlockSpec((tm, tn), lambda i, j: (i, j)),
        out_shape=jax.ShapeDtypeStruct((t, D_MODEL), BF16),
        compiler_params=_cparams(("parallel", "parallel")),
    )(o_rw, o_mla, o_df, gates, gates, gates, p["w_up_rw"], p["w_up_mla"], p["w_up_df"])


def _layer_norm(y, g, b):
    mu = jnp.mean(y, -1, keepdims=True)
    yc = y - mu
    var = jnp.mean(yc * yc, -1, keepdims=True)
    return yc * lax.rsqrt(var + LN_EPS) * g + b


def _wo_ln_kernel(m_ref, w_ref, x_ref, g_ref, b_ref, o_ref, ob_ref):
    y = ALPHA * x_ref[...] + _dot(m_ref[...], w_ref[...])
    out = _layer_norm(y, g_ref[...], b_ref[...])
    o_ref[...] = out
    ob_ref[...] = out.astype(BF16).reshape(ob_ref.shape)


def _wo_ln(merged, x, p):
    t = x.shape[0]
    tm = 256
    tile = pl.BlockSpec((tm, D_MODEL), lambda i: (i, 0))
    tile3 = pl.BlockSpec((tm, ROW_SUB, LANES), lambda i: (i, 0, 0))
    full = lambda a: pl.BlockSpec(a.shape, lambda i: (0,) * a.ndim)
    return pl.pallas_call(
        _wo_ln_kernel,
        name="wo_ln",
        grid=(t // tm,),
        in_specs=[tile, full(p["w_o"]), tile, full(p["ln1_g"]), full(p["ln1_b"])],
        out_specs=[tile, tile3],
        out_shape=[jax.ShapeDtypeStruct((t, D_MODEL), F32),
                   jax.ShapeDtypeStruct((t, ROW_SUB, LANES), BF16)],
        compiler_params=_cparams(("parallel",)),
    )(merged, p["w_o"], x, p["ln1_g"], p["ln1_b"])


ROUTER_TILE = 1024


def _router_kernel(x_ref, wh_ref, wl_ref, bias_ref, tri_ref, idx_o, wts_o, rank_o, cnt_o, cnt_scr):
    @pl.when(pl.program_id(0) == 0)
    def _():
        cnt_scr[...] = jnp.zeros_like(cnt_scr)

    xh, xl = _split_bf16(x_ref[...])
    wh, wl = wh_ref[...], wl_ref[...]
    logits = _dot_nt(wh, xh) + (_dot_nt(wh, xl) + _dot_nt(wl, xh))
    scores = jax.nn.sigmoid(logits)
    sel = scores + bias_ref[...]
    tm = sel.shape[1]

    def row(a, i):
        return a[i:i + 1, :]

    best = jnp.zeros((1, tm), jnp.int32)
    best_s = None
    for g in range(N_GROUPS):
        a, b, c, d = (row(sel, EXPERTS_PER_GROUP * g + j) for j in range(4))
        hi1, lo1 = jnp.maximum(a, b), jnp.minimum(a, b)
        hi2, lo2 = jnp.maximum(c, d), jnp.minimum(c, d)
        gs = jnp.maximum(hi1, hi2) + jnp.maximum(jnp.minimum(hi1, hi2), jnp.maximum(lo1, lo2))
        if g == 0:
            best_s = gs
        else:
            upd = gs > best_s
            best = jnp.where(upd, g, best)
            best_s = jnp.where(upd, gs, best_s)

    def pick(a, j):
        out = row(a, j)
        for g in range(1, N_GROUPS):
            out = jnp.where(best == g, row(a, EXPERTS_PER_GROUP * g + j), out)
        return out

    cand = [pick(sel, j) for j in range(EXPERTS_PER_GROUP)]
    csc = [pick(scores, j) for j in range(EXPERTS_PER_GROUP)]
    neg = jnp.float32(-jnp.inf)

    def argmax4(vals):
        bi, bv = jnp.zeros((1, tm), jnp.int32), vals[0]
        for j in range(1, EXPERTS_PER_GROUP):
            upd = vals[j] > bv
            bi = jnp.where(upd, j, bi)
            bv = jnp.where(upd, vals[j], bv)
        return bi

    i1 = argmax4(cand)
    i2 = argmax4([jnp.where(i1 == j, neg, cand[j]) for j in range(EXPERTS_PER_GROUP)])

    def take(vals, i):
        out = vals[0]
        for j in range(1, EXPERTS_PER_GROUP):
            out = jnp.where(i == j, vals[j], out)
        return out

    w1, w2 = take(csc, i1), take(csc, i2)
    tot = w1 + w2
    e1 = best * EXPERTS_PER_GROUP + i1
    e2 = best * EXPERTS_PER_GROUP + i2
    eid = lax.broadcasted_iota(jnp.int32, (N_EXPERTS, tm), 0)
    oh1 = eid == e1
    oh2 = eid == e2
    oh = (oh1 | oh2).astype(BF16)
    before = _dot(oh, tri_ref[...]) + cnt_scr[...][:, 0:1]
    r1 = jnp.sum(jnp.where(oh1, before, 0.0), 0, keepdims=True)
    r2 = jnp.sum(jnp.where(oh2, before, 0.0), 0, keepdims=True)
    idx_o[0:1, :] = e1
    idx_o[1:2, :] = e2
    wts_o[0:1, :] = w1 / tot
    wts_o[1:2, :] = w2 / tot
    rank_o[0:1, :] = r1.astype(jnp.int32)
    rank_o[1:2, :] = r2.astype(jnp.int32)
    new_cnt = cnt_scr[...] + jnp.sum(oh.astype(F32), 1, keepdims=True)
    cnt_scr[...] = new_cnt
    cnt_o[...] = new_cnt


def _router(x, p):
    t = x.shape[0]
    tm = ROUTER_TILE
    full = lambda a: pl.BlockSpec(a.shape, lambda i: (0,) * a.ndim)
    tok = pl.BlockSpec((TOP_K, tm), lambda i: (0, i))
    consts = [p["router_wt_hi"], p["router_wt_lo"], p["router_bias"], p["router_tri"]]
    return pl.pallas_call(
        _router_kernel,
        name="router",
        grid=(t // tm,),
        in_specs=[pl.BlockSpec((tm, D_MODEL), lambda i: (i, 0))] + [full(a) for a in consts],
        out_specs=[tok, tok, tok, pl.BlockSpec((N_EXPERTS, LANES), lambda i: (0, 0))],
        out_shape=[jax.ShapeDtypeStruct((TOP_K, t), jnp.int32), jax.ShapeDtypeStruct((TOP_K, t), F32),
                   jax.ShapeDtypeStruct((TOP_K, t), jnp.int32),
                   jax.ShapeDtypeStruct((N_EXPERTS, LANES), F32)],
        scratch_shapes=[pltpu.VMEM((N_EXPERTS, LANES), F32)],
        compiler_params=_cparams(("arbitrary",)),
    )(x, *consts)


DISPATCH_TILE = 512


def _dispatch_kernel(dest_ref, x_ref, init_ref, xs_ref, sem):
    del init_ref
    def copy(r, k):
        return pltpu.make_async_copy(x_ref.at[r], xs_ref.at[dest_ref[0, TOP_K * r + k]], sem)

    def start(r, carry):
        copy(r, 0).start()
        copy(r, 1).start()
        return carry

    def wait(r, carry):
        copy(r, 0).wait()
        copy(r, 1).wait()
        return carry

    lax.fori_loop(0, DISPATCH_TILE, start, 0, unroll=DMA_UNROLL)
    lax.fori_loop(0, DISPATCH_TILE, wait, 0, unroll=DMA_UNROLL)


def _dispatch(xb, dest, n_rows):
    t = xb.shape[0]
    nt = t // DISPATCH_TILE
    dest2 = dest.reshape(nt, 1, DISPATCH_TILE * TOP_K)
    init = jnp.zeros((n_rows, ROW_SUB, LANES), BF16)
    return pl.pallas_call(
        _dispatch_kernel,
        name="dispatch",
        grid=(nt,),
        in_specs=[pl.BlockSpec((None, 1, DISPATCH_TILE * TOP_K), lambda i: (i, 0, 0),
                               memory_space=pltpu.SMEM),
                  pl.BlockSpec((DISPATCH_TILE, ROW_SUB, LANES), lambda i: (i, 0, 0)),
                  pl.BlockSpec(memory_space=pl.ANY)],
        out_specs=pl.BlockSpec(memory_space=pl.ANY),
        out_shape=jax.ShapeDtypeStruct((n_rows, ROW_SUB, LANES), BF16),
        scratch_shapes=[pltpu.SemaphoreType.DMA(())],
        input_output_aliases={2: 0},
        compiler_params=_cparams(("arbitrary",)),
    )(dest2, xb, init)


def _ffn_kernel(te_ref, nu_ref, xs_ref, wg_ref, wu_ref, wd_ref, y_ref):
    del te_ref

    @pl.when(pl.program_id(0) < nu_ref[0])
    def _():
        xs = xs_ref[...].reshape(FFN_TILE, D_MODEL)
        h = jax.nn.silu(_dot(xs, wg_ref[0])) * _dot(xs, wu_ref[0])
        y_ref[...] = _dot(h.astype(BF16), wd_ref[0]).reshape(y_ref.shape)

    @pl.when(pl.program_id(0) >= nu_ref[0])
    def _():
        y_ref[...] = jnp.zeros_like(y_ref)


def _ffn(xs, tile_expert, n_used, p):
    n_rows = xs.shape[0]
    tm = FFN_TILE
    grid_spec = pltpu.PrefetchScalarGridSpec(
        num_scalar_prefetch=2,
        grid=(n_rows // tm,),
        in_specs=[pl.BlockSpec((tm, ROW_SUB, LANES), lambda i, te, nu: (i, 0, 0)),
                  pl.BlockSpec((1, D_MODEL, D_EXPERT), lambda i, te, nu: (te[i], 0, 0)),
                  pl.BlockSpec((1, D_MODEL, D_EXPERT), lambda i, te, nu: (te[i], 0, 0)),
                  pl.BlockSpec((1, D_EXPERT, D_MODEL), lambda i, te, nu: (te[i], 0, 0))],
        out_specs=pl.BlockSpec((tm, ROW_SUB, LANES), lambda i, te, nu: (i, 0, 0)),
    )
    return pl.pallas_call(
        _ffn_kernel,
        name="ffn",
        grid_spec=grid_spec,
        out_shape=jax.ShapeDtypeStruct((n_rows, ROW_SUB, LANES), F32),
        compiler_params=_cparams(("arbitrary",)),
    )(tile_expert, n_used, xs, p["ex_w_gate"], p["ex_w_up"], p["ex_w_down"])


COMBINE_TILE = 256


def _combine_ln_kernel(dest_ref, dest_next_ref, y_ref, x_ref, w_ref, g_ref, b_ref, o_ref, ob_ref,
                       buf, sem):
    i = pl.program_id(0)
    slot = i % 2

    def copy(idx_ref, s, r, k):
        return pltpu.make_async_copy(y_ref.at[idx_ref[0, TOP_K * r + k]], buf.at[s, k, r],
                                     sem.at[s])

    def issue(idx_ref, s):
        def body(r, carry):
            copy(idx_ref, s, r, 0).start()
            copy(idx_ref, s, r, 1).start()
            return carry
        lax.fori_loop(0, COMBINE_TILE, body, 0, unroll=DMA_UNROLL)

    @pl.when(i == 0)
    def _():
        issue(dest_ref, 0)

    @pl.when(i + 1 < pl.num_programs(0))
    def _():
        issue(dest_next_ref, 1 - slot)

    def wait(r, carry):
        copy(dest_ref, slot, r, 0).wait()
        copy(dest_ref, slot, r, 1).wait()
        return carry

    lax.fori_loop(0, COMBINE_TILE, wait, 0, unroll=DMA_UNROLL)
    w = w_ref[...]
    shape = (COMBINE_TILE, D_MODEL)
    ffn = w[:, 0:1] * buf[slot, 0].reshape(shape) + w[:, 1:2] * buf[slot, 1].reshape(shape)
    out = _layer_norm(ALPHA * x_ref[...] + ffn, g_ref[...], b_ref[...])
    o_ref[...] = out
    ob_ref[...] = out.astype(BF16)


def _combine_ln(y, dest, x, wts, p):
    t = x.shape[0]
    tm = COMBINE_TILE
    nt = t // tm
    dest2 = dest.reshape(nt, 1, tm * TOP_K)
    tile = pl.BlockSpec((tm, D_MODEL), lambda i: (i, 0))
    full = lambda a: pl.BlockSpec(a.shape, lambda i: (0,) * a.ndim)
    return pl.pallas_call(
        _combine_ln_kernel,
        name="combine_ln",
        grid=(nt,),
        in_specs=[pl.BlockSpec((None, 1, tm * TOP_K), lambda i: (i, 0, 0), memory_space=pltpu.SMEM),
                  pl.BlockSpec((None, 1, tm * TOP_K), lambda i: (jnp.minimum(i + 1, nt - 1), 0, 0),
                               memory_space=pltpu.SMEM),
                  pl.BlockSpec(memory_space=pl.ANY), tile,
                  pl.BlockSpec((tm, TOP_K), lambda i: (i, 0)), full(p["ln2_g"]), full(p["ln2_b"])],
        out_specs=[tile, tile],
        out_shape=[jax.ShapeDtypeStruct((t, D_MODEL), F32), jax.ShapeDtypeStruct((t, D_MODEL), BF16)],
        scratch_shapes=[pltpu.VMEM((2, TOP_K, tm, ROW_SUB, LANES), F32),
                        pltpu.SemaphoreType.DMA((2,))],
        compiler_params=_cparams(("arbitrary",)),
    )(dest2, dest2, y, x, wts, p["ln2_g"], p["ln2_b"])


def _moe_ln(x, xb, p):
    t = x.shape[0]
    idx, wts, rank, counts = _router(x, p)
    counts = counts[:, 0].astype(jnp.int32)
    tiles = (counts + FFN_TILE - 1) // FFN_TILE
    tile_end = jnp.cumsum(tiles)
    row_start = (tile_end - tiles) * FFN_TILE
    n_tiles = (t * TOP_K) // FFN_TILE + N_EXPERTS
    start_of = jnp.sum(jnp.where(idx[..., None] == jnp.arange(N_EXPERTS), row_start, 0), -1)
    dest = (start_of + rank).T.reshape(-1)
    tile_expert = jnp.minimum(jnp.sum(jnp.arange(n_tiles)[:, None] >= tile_end[None, :], axis=1),
                              N_EXPERTS - 1).astype(jnp.int32)
    n_used = tile_end[-1:].astype(jnp.int32)
    xs = _dispatch(xb, dest, n_tiles * FFN_TILE)
    y = _ffn(xs, tile_expert, n_used, p)
    return _combine_ln(y, dest, x, wts.T, p)


def _block_diag_ones(n, blk):
    i = jnp.arange(n) // blk
    return (i[:, None] == i[None, :]).astype(BF16)


def _prep_layer(l, w):
    row = lambda a: a.reshape(1, -1).astype(F32)
    w_in = w["w_in"][l]
    o1, o2, o3 = RW_IN, RW_IN + MLA_IN, RW_IN + MLA_IN + DF_IN
    p = {
        "w_in_rw": w_in[:, :o1].astype(BF16),
        "w_in_mla": jnp.pad(w_in[:, o1:o2], ((0, 0), (0, MLA_IN_PAD - MLA_IN))).astype(BF16),
        "w_in_df": w_in[:, o2:o3].astype(BF16),
        "w_in_gate": w_in[:, o3:].astype(BF16),
        "mu_prev": row(w["shift_prev"][l]), "mu_next": row(w["shift_next"][l]),
        "w0": row(w["rw_w0"][l]), "a0": row(w["rw_a0"][l]),
        "g2": w["rw_g2"][l].astype(BF16),
        "k_k": row(w["rw_k_k"][l]), "k_a": row(w["rw_k_a"][l]), "r_k": row(w["rw_r_k"][l]),
        "lnx_g": row(w["rw_lnx_g"][l]), "lnx_b": row(w["rw_lnx_b"][l]),
        "bd": _block_diag_ones(RW_WIDTH, RW_HEAD_DIM),
        "q_norm": row(w["mla_q_norm"][l]), "kv_norm": row(w["mla_kv_norm"][l]),
        "lq1": row(w["df_lq1"][l]), "lk1": row(w["df_lk1"][l]),
        "lq2": row(w["df_lq2"][l]), "lk2": row(w["df_lk2"][l]),
        "subln": row(w["df_subln"][l]),
        "w_up_rw": w["w_up_rw"][l].astype(BF16), "w_up_mla": w["w_up_mla"][l].astype(BF16),
        "w_up_df": w["w_up_df"][l].astype(BF16), "w_o": w["w_o"][l].astype(BF16),
        "ln1_g": row(w["ln1_g"][l]), "ln1_b": row(w["ln1_b"][l]),
        "ln2_g": row(w["ln2_g"][l]), "ln2_b": row(w["ln2_b"][l]),
        "ex_w_gate": w["ex_w_gate"][l].astype(BF16), "ex_w_up": w["ex_w_up"][l].astype(BF16),
        "ex_w_down": w["ex_w_down"][l].astype(BF16),
    }
    zc = jnp.zeros((DECAY_LORA, RW_WIDTH), F32)
    w2 = w["rw_w2"][l]
    a2 = w["rw_a2"][l]
    p["w2cat"] = jnp.block([[w2[0], zc], [zc, w2[1]]]).astype(BF16)
    p["a2cat"] = jnp.block([[a2[0], zc], [zc, a2[1]]]).astype(BF16)
    wq = w["mla_w_uq"][l].reshape(MLA_Q_LORA, MLA_HEADS, MLA_NOPE + MLA_ROPE)
    p["wq"] = jnp.pad(wq, ((0, 0), (0, 0), (0, MLA_HEAD_PAD - MLA_NOPE - MLA_ROPE))
                      ).reshape(MLA_Q_LORA, -1).astype(BF16)
    wkv = w["mla_w_ukv"][l].reshape(MLA_KV_LORA, MLA_HEADS, MLA_NOPE + MLA_V)
    wk_nope = jnp.pad(wkv[:, :, :MLA_NOPE], ((0, 0), (0, 0), (0, MLA_HEAD_PAD - MLA_NOPE)))
    place = jnp.zeros((LANES, MLA_HEADS, MLA_HEAD_PAD), F32)
    j = jnp.arange(MLA_ROPE)
    place = place.at[j, :, MLA_NOPE + j].set(1.0)
    p["wk"] = jnp.concatenate([wk_nope, place], axis=0).reshape(MLA_KV_LORA + LANES, -1).astype(BF16)
    p["wv"] = wkv[:, :, MLA_NOPE:].reshape(MLA_KV_LORA, -1).T.astype(BF16)
    return p


def _trunk(x3, layers, shared):
    b, s, d = x3.shape
    t = b * s
    tabs = {
        "mla_q": _rope_tables(s, MLA_ROPE, MLA_NOPE, LANES),
        "mla_k": _rope_tables(s, MLA_ROPE, 0, LANES),
        "df": _rope_tables(s, DF_HEAD_DIM, 0, DF_HEAD_DIM),
    }
    x = x3.reshape(t, d)
    xb = x.astype(BF16)
    for l, p in enumerate(layers):
        p = dict(p, **shared)
        lambda_init = 0.8 - 0.6 * math.exp(-0.3 * l)
        z_rw = _matmul(xb, p["w_in_rw"], F32).reshape(b, s, -1)
        z_mla = _matmul(xb, p["w_in_mla"], F32).reshape(b, s, -1)
        z_df = _matmul(xb, p["w_in_df"], F32).reshape(b, s, -1)
        gates = _matmul(xb, p["w_in_gate"], BF16, act="sigmoid")
        r, v, kk, g, bonus, logd, a, kdir = _rw_prep(z_rw, p)
        yf, yb = _rw_scan(r, v, kk, logd, a, kdir)
        o_rw = _rw_post(yf.reshape(t, -1), yb.reshape(t, -1), bonus.reshape(t, -1),
                        g.reshape(t, -1), p)
        q, k, v2 = _mla_prep(z_mla, p, tabs)
        o_mla = _mla_attn(q, k, v2).reshape(t, -1)
        q, k, v2 = _df_prep(z_df, tabs)
        o_df = _df_attn(q, k, v2, p, lambda_init).reshape(t, -1)
        merged = _merge(o_rw, o_mla, o_df, gates, p)
        x, xb = _wo_ln(merged, x, p)
        x, xb = _moe_ln(x, xb, p)
    return x.reshape(b, s, d)


def kernel(x_prompt, x_sample, w_in, shift_prev, shift_next, rw_w0, rw_w2, rw_a0, rw_a2, rw_g2,
           rw_k_k, rw_k_a, rw_r_k, rw_lnx_g, rw_lnx_b, mla_q_norm, mla_kv_norm, mla_w_uq,
           mla_w_ukv, df_lq1, df_lk1, df_lq2, df_lk2, df_subln, w_up_rw, w_up_mla, w_up_df, w_o,
           ln1_g, ln1_b, ln2_g, ln2_b, router_w, router_bias, ex_w_gate, ex_w_up, ex_w_down):
    w = dict(w_in=w_in, shift_prev=shift_prev, shift_next=shift_next, rw_w0=rw_w0, rw_w2=rw_w2,
             rw_a0=rw_a0, rw_a2=rw_a2, rw_g2=rw_g2, rw_k_k=rw_k_k, rw_k_a=rw_k_a, rw_r_k=rw_r_k,
             rw_lnx_g=rw_lnx_g, rw_lnx_b=rw_lnx_b, mla_q_norm=mla_q_norm,
             mla_kv_norm=mla_kv_norm, mla_w_uq=mla_w_uq, mla_w_ukv=mla_w_ukv, df_lq1=df_lq1,
             df_lk1=df_lk1, df_lq2=df_lq2, df_lk2=df_lk2, df_subln=df_subln, w_up_rw=w_up_rw,
             w_up_mla=w_up_mla, w_up_df=w_up_df, w_o=w_o, ln1_g=ln1_g, ln1_b=ln1_b, ln2_g=ln2_g,
             ln2_b=ln2_b, ex_w_gate=ex_w_gate, ex_w_up=ex_w_up, ex_w_down=ex_w_down)
    layers = [_prep_layer(l, w) for l in range(DEPTH)]
    rwt = router_w.T.astype(F32)
    rwt_hi = rwt.astype(BF16)
    ti = jnp.arange(ROUTER_TILE)
    shared = {
        "router_wt_hi": rwt_hi,
        "router_wt_lo": (rwt - rwt_hi.astype(F32)).astype(BF16),
        "router_bias": router_bias.reshape(N_EXPERTS, 1).astype(F32),
        "router_tri": (ti[:, None] < ti[None, :]).astype(BF16),
    }
    return (_trunk(x_prompt, layers, shared), _trunk(x_sample, layers, shared))
```

```python
import functools
import math

import jax
import jax.numpy as jnp
from jax import lax
from jax.experimental import pallas as pl
from jax.experimental.pallas import tpu as pltpu

F32 = jnp.float32
BF16 = jnp.bfloat16

D_MODEL = 2048
DEPTH = 2
RW_HEADS, RW_HEAD_DIM = 12, 64
RW_WIDTH = RW_HEADS * RW_HEAD_DIM
DECAY_LORA, ICLR_LORA, GATE_LORA, N_DIR = 64, 64, 128, 2
GN_EPS = 64e-5
MLA_HEADS, MLA_NOPE, MLA_ROPE, MLA_V = 8, 64, 32, 64
MLA_Q_LORA, MLA_KV_LORA = 512, 256
MLA_WIDTH = MLA_HEADS * MLA_V
DF_HEADS, DF_HEAD_DIM = 6, 64
DF_WIDTH = DF_HEADS * 2 * DF_HEAD_DIM
DF_EPS = 1e-5
N_BRANCH = 3
N_EXPERTS, N_GROUPS, TOP_K, D_EXPERT = 16, 4, 2, 1024
EXPERTS_PER_GROUP = N_EXPERTS // N_GROUPS
ROPE_THETA = 10000.0
LN_EPS = 1e-5
RMS_EPS = 1e-6
ALPHA = (2 * DEPTH) ** 0.25
RW_IN = 3 * RW_WIDTH + N_DIR * DECAY_LORA + N_DIR * ICLR_LORA + GATE_LORA
MLA_IN = MLA_Q_LORA + MLA_KV_LORA + MLA_ROPE
DF_IN = 3 * DF_WIDTH
GATE_IN = N_BRANCH * D_MODEL

LANES = 128
MLA_IN_PAD = 896
MLA_HEAD_PAD = 128
RW_CHUNK = 64
RW_PAIRS = RW_WIDTH // LANES
RW_BATCH = 4
VMEM_LIMIT = 56 * 1024 * 1024
FFN_TILE = 512
LOG2E = math.log2(math.e)
MLA_HEADS_PER_STEP = 4
DF_HEADS_PER_STEP = 3
DMA_UNROLL = 8
HALO = 8
ROW_SUB = D_MODEL // LANES


def _cparams(sem):
    return pltpu.CompilerParams(dimension_semantics=sem, vmem_limit_bytes=VMEM_LIMIT)


def _dot(a, b):
    return jnp.dot(a, b, preferred_element_type=F32)


def _dot_nt(a, b):
    return lax.dot_general(a, b, (((1,), (1,)), ((), ())), preferred_element_type=F32)


def _dot_tn(a, b):
    return lax.dot_general(a, b, (((0,), (0,)), ((), ())), preferred_element_type=F32)


def _split_bf16(x):
    hi = x.astype(BF16)
    lo = (x - hi.astype(F32)).astype(BF16)
    return hi, lo


def _pick_tile(n, candidates):
    for c in candidates:
        if n % c == 0:
            return c
    raise ValueError(f"no tile for {n}")


def _mm_kernel(x_ref, w_ref, o_ref, *, act):
    acc = _dot(x_ref[...], w_ref[...])
    if act == "sigmoid":
        acc = jax.nn.sigmoid(acc)
    o_ref[...] = acc.astype(o_ref.dtype)


def _matmul(x, w, out_dtype, act=None):
    m, k = x.shape
    n = w.shape[1]
    tm = _pick_tile(m, (1024, 512, 256, 128))
    tn = _pick_tile(n, (1024, 896, 768, 512, 256, 128))
    return pl.pallas_call(
        functools.partial(_mm_kernel, act=act),
        name="mm",
        grid=(n // tn, m // tm),
        in_specs=[pl.BlockSpec((tm, k), lambda j, i: (i, 0)),
                  pl.BlockSpec((k, tn), lambda j, i: (0, j))],
        out_specs=pl.BlockSpec((tm, tn), lambda j, i: (i, j)),
        out_shape=jax.ShapeDtypeStruct((m, n), out_dtype),
        compiler_params=_cparams(("parallel", "parallel")),
    )(x, w)


def _rw_prep_kernel(z_ref, hp_ref, hn_ref, mup_ref, mun_ref, w2_ref, a2_ref, g2_ref, w0_ref,
                    a0_ref, kk_k_ref, k_a_ref, r_k_ref, bd_ref,
                    r_o, v_o, kk_o, g_o, bonus_o, logd_o, a_o, kdir_o):
    z = z_ref[0]
    tr = z.shape[0]
    row = lax.broadcasted_iota(jnp.int32, (tr, 1), 0)
    ti = pl.program_id(1)
    halo_prev = jnp.where(ti == 0, 0.0, hp_ref[0, HALO - 1:HALO, :])
    halo_next = jnp.where(ti == pl.num_programs(1) - 1, 0.0, hn_ref[0, 0:1, :])
    prev = jnp.where(row == 0, halo_prev, pltpu.roll(z, 1, 0))
    nxt = jnp.where(row == tr - 1, halo_next, pltpu.roll(z, tr - 1, 0))
    zs = z + mup_ref[...] * (prev - z) + mun_ref[...] * (nxt - z)
    c = RW_WIDTH
    r, k, v = zs[:, 0:c], zs[:, c:2 * c], zs[:, 2 * c:3 * c]
    wl = zs[:, 3 * c:3 * c + 128]
    al = zs[:, 3 * c + 128:3 * c + 256]
    gl = zs[:, 3 * c + 256:3 * c + 384]
    w_raw = w0_ref[...] + _dot(jnp.tanh(wl).astype(BF16), w2_ref[...])
    a = jax.nn.sigmoid(a0_ref[...] + _dot(al.astype(BF16), a2_ref[...]))
    g = _dot(jax.nn.sigmoid(gl).astype(BF16), g2_ref[...])
    logd = (-math.exp(-0.5)) * jax.nn.sigmoid(w_raw)
    bd = bd_ref[...]

    def head_sum(x):
        hi, lo = _split_bf16(x)
        return _dot(hi, bd) + _dot(lo, bd)

    kk = k * kk_k_ref[...]
    kk = kk * lax.rsqrt(head_sum(kk * kk) + 1e-12)
    k_a = k_a_ref[...]
    kd0 = k * (1.0 + (a[:, 0:c] - 1.0) * k_a)
    kd1 = k * (1.0 + (a[:, c:2 * c] - 1.0) * k_a)
    bonus = head_sum(r * r_k_ref[...] * (kd0 + kd1)) * v
    r_o[0] = r
    v_o[0] = v
    kk_o[0] = kk
    g_o[0] = g
    bonus_o[0] = bonus
    logd_o[0] = logd
    a_o[0] = a
    kdir_o[0, :, 0:c] = kd0
    kdir_o[0, :, c:2 * c] = kd1


def _rw_prep(z_rw, p):
    b, s, _ = z_rw.shape
    tr = 256
    nt = s // tr
    hpt = tr // HALO
    c = RW_WIDTH
    tile = lambda w: pl.BlockSpec((1, tr, w), lambda bi, ti: (bi, ti, 0))
    halo_prev = pl.BlockSpec((1, HALO, RW_IN), lambda bi, ti: (bi, jnp.maximum(ti * hpt - 1, 0), 0))
    halo_next = pl.BlockSpec((1, HALO, RW_IN),
                             lambda bi, ti: (bi, jnp.minimum((ti + 1) * hpt, s // HALO - 1), 0))
    full = lambda a: pl.BlockSpec(a.shape, lambda bi, ti: (0,) * a.ndim)
    consts = [p["mu_prev"], p["mu_next"], p["w2cat"], p["a2cat"], p["g2"], p["w0"], p["a0"],
              p["k_k"], p["k_a"], p["r_k"], p["bd"]]
    out_w = [c, c, c, c, c, 2 * c, 2 * c, 2 * c]
    return pl.pallas_call(
        _rw_prep_kernel,
        name="rw_prep",
        grid=(b, nt),
        in_specs=[tile(RW_IN), halo_prev, halo_next] + [full(a) for a in consts],
        out_specs=[tile(w) for w in out_w],
        out_shape=[jax.ShapeDtypeStruct((b, s, w), F32) for w in out_w],
        compiler_params=_cparams(("parallel", "parallel")),
    )(z_rw, z_rw, z_rw, *consts)


def _rw_scan_kernel(rf, vf, kkf, ldf, af, kdf, rb, vb, kkb, ldb, ab, kdb, yf_o, yb_o, z_scr):
    @pl.when(pl.program_id(1) == 0)
    def _():
        z_scr[...] = jnp.zeros_like(z_scr)

    c = RW_CHUNK
    ti = lax.broadcasted_iota(jnp.int32, (c, c), 0)
    tj = lax.broadcasted_iota(jnp.int32, (c, c), 1)
    lane = lax.broadcasted_iota(jnp.int32, (1, LANES), 1)
    lane_m = (lane < RW_HEAD_DIM, lane >= RW_HEAD_DIM)
    bi = lax.broadcasted_iota(jnp.int32, (LANES, LANES), 0) // RW_HEAD_DIM
    bj = lax.broadcasted_iota(jnp.int32, (LANES, LANES), 1) // RW_HEAD_DIM
    bdmask = bi == bj
    incl = ((ti >= tj), (ti <= tj))
    strict = ((ti > tj), (ti < tj))
    incl_bf = tuple(m.astype(BF16) for m in incl)
    t2 = lax.broadcasted_iota(jnp.int32, (2 * c, 2 * c), 0)
    j2 = lax.broadcasted_iota(jnp.int32, (2 * c, 2 * c), 1) & (c - 1)
    diag_ok = (t2 >= c) & ((t2 & (c - 1)) == j2)
    t2 = t2 & (c - 1)
    mask_ar = ((t2 > j2) | diag_ok, (t2 < j2) | diag_ok)
    ones = jnp.ones((c, LANES), BF16)
    in_refs = ((rf, vf, kkf, ldf, af, kdf), (rb, vb, kkb, ldb, ab, kdb))
    out_refs = (yf_o, yb_o)

    def bd_rows(x):
        return jnp.concatenate([jnp.where(lane_m[0], x, 0.0), jnp.where(lane_m[1], x, 0.0)],
                               axis=0).astype(BF16)

    probs = [(d, p, bb) for d in range(N_DIR) for p in range(RW_PAIRS) for bb in range(RW_BATCH)]
    sl = lambda p: slice(p * LANES, (p + 1) * LANES)
    val = {s: [ref[s[2], :, sl(s[1])] for ref in in_refs[s[0]]] for s in probs}
    ld_split = {s: _split_bf16(val[s][3]) for s in probs}
    logp = {s: _dot(incl_bf[s[0]], ld_split[s][0]) + _dot(incl_bf[s[0]], ld_split[s][1])
            for s in probs}
    logpc = {s: _dot_tn(ld_split[s][0], ones) + _dot_tn(ld_split[s][1], ones)
             for s in probs}
    at, rt, bt, kt, vv, z, zb = {}, {}, {}, {}, {}, {}, {}
    for s in probs:
        r, v, kk, ld, a, kd = val[s]
        pinv = jnp.exp(-logp[s])
        at[s] = -(kk * jnp.exp(logp[s] - ld))
        rt[s] = r * jnp.exp(logp[s])
        bt[s] = kk * a * pinv
        kt[s] = kd * pinv
        vv[s] = v
        z[s] = z_scr[s]
        zb[s] = z[s].astype(BF16)
    lhs = {s: jnp.concatenate([at[s], rt[s]], axis=0).astype(BF16) for s in probs}
    gb = {s: _dot_nt(lhs[s], bd_rows(bt[s])) for s in probs}
    gk = {s: _dot_nt(lhs[s], bd_rows(kt[s])) for s in probs}
    a_kk = {s: jnp.where(mask_ar[s[0]], gk[s], 0.0).astype(BF16) for s in probs}
    a_bb = {s: jnp.where(mask_ar[s[0]], gb[s], 0.0).astype(BF16) for s in probs}
    zv = {s: _dot(lhs[s], zb[s]) + _dot(a_kk[s], bd_rows(vv[s])) for s in probs}
    x = {s: zv[s][0:c] for s in probs}
    ap = {s: a_bb[s][0:c] for s in probs}
    for i in range(6):
        if i < 5:
            prod = {s: _dot(ap[s], jnp.concatenate([bd_rows(x[s]), bd_rows(ap[s])], axis=1))
                    for s in probs}
            x = {s: x[s] + prod[s][:, 0:LANES] for s in probs}
            ap = {s: prod[s][:, LANES:].astype(BF16) for s in probs}
        else:
            x = {s: x[s] + _dot(ap[s], bd_rows(x[s])) for s in probs}
    for s in probs:
        d, p, bb = s
        out_refs[d][bb, :, sl(p)] = zv[s][c:] + _dot(a_bb[s][c:], bd_rows(x[s]))
        upd = _dot_tn(jnp.concatenate([bt[s], kt[s]], axis=0).astype(BF16),
                      jnp.concatenate([x[s], vv[s]], axis=0).astype(BF16))
        z_scr[s] = jnp.where(bdmask, jnp.exp(logpc[s]) * (z[s] + upd), 0.0)


def _rw_scan(r, v, kk, logd, a, kdir):
    b, s, c = r.shape
    nc = s // RW_CHUNK
    nb = RW_BATCH
    fwd = pl.BlockSpec((nb, RW_CHUNK, c), lambda bi, ci: (bi, ci, 0))
    bwd = pl.BlockSpec((nb, RW_CHUNK, c), lambda bi, ci: (bi, nc - 1 - ci, 0))
    bwd_dir = pl.BlockSpec((nb, RW_CHUNK, c), lambda bi, ci: (bi, nc - 1 - ci, 1))
    return pl.pallas_call(
        _rw_scan_kernel,
        name="rw_scan",
        grid=(b // nb, nc),
        in_specs=[fwd, fwd, fwd, fwd, fwd, fwd, bwd, bwd, bwd, bwd_dir, bwd_dir, bwd_dir],
        out_specs=[fwd, bwd],
        out_shape=[jax.ShapeDtypeStruct((b, s, c), F32)] * 2,
        scratch_shapes=[pltpu.VMEM((N_DIR, RW_PAIRS, nb, LANES, LANES), F32)],
        compiler_params=_cparams(("parallel", "arbitrary")),
    )(r, v, kk, logd, a, kdir, r, v, kk, logd, a, kdir)


def _rw_post_kernel(yf_ref, yb_ref, bonus_ref, g_ref, lg_ref, lb_ref, bd_ref, o_ref):
    y = yf_ref[...] + yb_ref[...]
    bd = bd_ref[...]

    def head_mean(x):
        hi, lo = _split_bf16(x)
        return (_dot(hi, bd) + _dot(lo, bd)) * (1.0 / RW_HEAD_DIM)

    mu = head_mean(y)
    yc = y - mu
    var = head_mean(yc * yc)
    yn = yc * lax.rsqrt(var + GN_EPS) * lg_ref[...] + lb_ref[...]
    o_ref[...] = ((yn + bonus_ref[...]) * g_ref[...]).astype(o_ref.dtype)


def _rw_post(yf, yb, bonus, g, p):
    t, c = yf.shape
    tm = _pick_tile(t, (1024, 512, 256))
    tile = pl.BlockSpec((tm, c), lambda i: (i, 0))
    full = lambda a: pl.BlockSpec(a.shape, lambda i: (0,) * a.ndim)
    consts = [p["lnx_g"], p["lnx_b"], p["bd"]]
    return pl.pallas_call(
        _rw_post_kernel,
        name="rw_post",
        grid=(t // tm,),
        in_specs=[tile] * 4 + [full(a) for a in consts],
        out_specs=tile,
        out_shape=jax.ShapeDtypeStruct((t, c), BF16),
        compiler_params=_cparams(("parallel",)),
    )(yf, yb, bonus, g, *consts)


def _rope_block(x, cos, s_up, s_dn, half):
    return x * cos + pltpu.roll(x, half, 1) * s_up + pltpu.roll(x, LANES - half, 1) * s_dn


def _rope_tables(s, dim, lane_of_x1, period):
    half = dim // 2
    inv_freq = jnp.power(ROPE_THETA, -jnp.arange(half, dtype=F32) * (2.0 / dim))
    ang = jnp.arange(s, dtype=F32)[:, None] * inv_freq[None, :]
    lane = jnp.arange(LANES) % period - lane_of_x1
    in_x1 = (lane >= 0) & (lane < half)
    in_x2 = (lane >= half) & (lane < dim)
    j = jnp.clip(jnp.where(in_x2, lane - half, lane), 0, half - 1)
    cos = jnp.cos(ang)[:, j]
    sin = jnp.sin(ang)[:, j]
    cos_t = jnp.where(in_x1 | in_x2, cos, 1.0)
    s_up = jnp.where(in_x2, sin, 0.0)
    s_dn = jnp.where(in_x1, -sin, 0.0)
    return cos_t.astype(F32), s_up.astype(F32), s_dn.astype(F32)


def _mla_prep_kernel(z_ref, qg_ref, kvg_ref, wq_ref, wk_ref, wv_ref,
                     cq_ref, squ_ref, sqd_ref, ck_ref, sku_ref, skd_ref, q_o, k_o, v_o):
    z = z_ref[0]
    c_q = z[:, 0:MLA_Q_LORA]
    c_kv = z[:, MLA_Q_LORA:MLA_Q_LORA + MLA_KV_LORA]
    kr = z[:, MLA_Q_LORA + MLA_KV_LORA:MLA_IN_PAD]
    c_q = c_q * lax.rsqrt(jnp.mean(c_q * c_q, -1, keepdims=True) + RMS_EPS) * qg_ref[...]
    c_kv = c_kv * lax.rsqrt(jnp.mean(c_kv * c_kv, -1, keepdims=True) + RMS_EPS) * kvg_ref[...]
    q = _dot(c_q.astype(BF16), wq_ref[...])
    scale = (MLA_NOPE + MLA_ROPE) ** -0.5 * LOG2E
    cq, squ, sqd = cq_ref[...], squ_ref[...], sqd_ref[...]
    for h in range(MLA_HEADS):
        sl = slice(h * MLA_HEAD_PAD, (h + 1) * MLA_HEAD_PAD)
        q_o[0, :, sl] = (_rope_block(q[:, sl], cq, squ, sqd, MLA_ROPE // 2) * scale).astype(BF16)
    kr = _rope_block(kr, ck_ref[...], sku_ref[...], skd_ref[...], MLA_ROPE // 2)
    ckv_b = c_kv.astype(BF16)
    k_in = jnp.concatenate([ckv_b, kr.astype(BF16)], axis=1)
    k_o[0] = _dot(k_in, wk_ref[...]).astype(BF16)
    v_o[0] = _dot_nt(wv_ref[...], ckv_b).astype(BF16)


def _mla_prep(z_mla, p, tabs):
    b, s, _ = z_mla.shape
    tr = 512
    tile = lambda w: pl.BlockSpec((1, tr, w), lambda bi, ti: (bi, ti, 0))
    full = lambda a: pl.BlockSpec(a.shape, lambda bi, ti: (0,) * a.ndim)
    tab = pl.BlockSpec((tr, LANES), lambda bi, ti: (ti, 0))
    consts = [p["q_norm"], p["kv_norm"], p["wq"], p["wk"], p["wv"]]
    hq = MLA_HEADS * MLA_HEAD_PAD
    return pl.pallas_call(
        _mla_prep_kernel,
        name="mla_prep",
        grid=(b, s // tr),
        in_specs=[tile(MLA_IN_PAD)] + [full(a) for a in consts] + [tab] * 6,
        out_specs=[tile(hq), tile(hq),
                   pl.BlockSpec((1, MLA_WIDTH, tr), lambda bi, ti: (bi, 0, ti))],
        out_shape=[jax.ShapeDtypeStruct((b, s, hq), BF16), jax.ShapeDtypeStruct((b, s, hq), BF16),
                   jax.ShapeDtypeStruct((b, MLA_WIDTH, s), BF16)],
        compiler_params=_cparams(("parallel", "parallel")),
    )(z_mla, *consts, *tabs["mla_q"], *tabs["mla_k"])


def _softmax_pv_t(s_t, v_t):
    pr = jnp.exp2(s_t - jnp.max(s_t, 0, keepdims=True))
    l = jnp.sum(pr, 0, keepdims=True)
    return _dot(v_t, pr.astype(BF16)) * (1.0 / l)


def _mla_attn_kernel(q_ref, k_ref, v_ref, o_ref):
    nh = MLA_HEADS_PER_STEP
    sl = lambda h: slice(h * MLA_HEAD_PAD, (h + 1) * MLA_HEAD_PAD)
    scores = [_dot_nt(k_ref[0, :, sl(h)], q_ref[0, :, sl(h)]) for h in range(nh)]
    outs = []
    for h, s_t in enumerate(scores):
        pair = h // 2
        o_t = _softmax_pv_t(s_t, v_ref[0, pair * LANES:(pair + 1) * LANES, :])
        outs.append(o_t[(h % 2) * MLA_V:(h % 2 + 1) * MLA_V])
    o_ref[0] = jnp.concatenate(outs, axis=0).T.astype(o_ref.dtype)


def _mla_attn(q, k, v):
    b, s, _ = q.shape
    tq = 256
    nh = MLA_HEADS_PER_STEP
    return pl.pallas_call(
        _mla_attn_kernel,
        name="mla_attn",
        grid=(b, MLA_HEADS // nh, s // tq),
        in_specs=[pl.BlockSpec((1, tq, nh * MLA_HEAD_PAD), lambda bi, pi, qi: (bi, qi, pi)),
                  pl.BlockSpec((1, s, nh * MLA_HEAD_PAD), lambda bi, pi, qi: (bi, 0, pi)),
                  pl.BlockSpec((1, nh * MLA_V, s), lambda bi, pi, qi: (bi, pi, 0))],
        out_specs=pl.BlockSpec((1, tq, nh * MLA_V), lambda bi, pi, qi: (bi, qi, pi)),
        out_shape=jax.ShapeDtypeStruct((b, s, MLA_WIDTH), BF16),
        compiler_params=_cparams(("parallel", "parallel", "parallel")),
    )(q, k, v)


def _df_prep_kernel(z_ref, c_ref, su_ref, sd_ref, q_o, k_o, v_o):
    cos, s_up, s_dn = c_ref[...], su_ref[...], sd_ref[...]
    scale = DF_HEAD_DIM ** -0.5 * LOG2E
    for h in range(DF_HEADS):
        sl = slice(h * LANES, (h + 1) * LANES)
        q_o[0, :, sl] = (_rope_block(z_ref[0, :, sl], cos, s_up, s_dn, DF_HEAD_DIM // 2)
                         * scale).astype(BF16)
        slk = slice(DF_WIDTH + h * LANES, DF_WIDTH + (h + 1) * LANES)
        k_o[0, :, sl] = _rope_block(z_ref[0, :, slk], cos, s_up, s_dn,
                                    DF_HEAD_DIM // 2).astype(BF16)
    v_o[0] = z_ref[0, :, 2 * DF_WIDTH:3 * DF_WIDTH].T.astype(BF16)


def _df_prep(z_df, tabs):
    b, s, _ = z_df.shape
    tr = 512
    tile = lambda w: pl.BlockSpec((1, tr, w), lambda bi, ti: (bi, ti, 0))
    tab = pl.BlockSpec((tr, LANES), lambda bi, ti: (ti, 0))
    return pl.pallas_call(
        _df_prep_kernel,
        name="df_prep",
        grid=(b, s // tr),
        in_specs=[tile(DF_IN)] + [tab] * 3,
        out_specs=[tile(DF_WIDTH), tile(DF_WIDTH),
                   pl.BlockSpec((1, DF_WIDTH, tr), lambda bi, ti: (bi, 0, ti))],
        out_shape=[jax.ShapeDtypeStruct((b, s, DF_WIDTH), BF16)] * 2
                  + [jax.ShapeDtypeStruct((b, DF_WIDTH, s), BF16)],
        compiler_params=_cparams(("parallel", "parallel")),
    )(z_df, *tabs["df"])


def _df_attn_kernel(q_ref, k_ref, v_ref, lq1, lk1, lq2, lk2, g_ref, o_ref, *, lambda_init):
    lam = (jnp.exp(jnp.sum(lq1[...] * lk1[...], -1, keepdims=True))
           - jnp.exp(jnp.sum(lq2[...] * lk2[...], -1, keepdims=True)) + lambda_init)
    nh = DF_HEADS_PER_STEP
    lane = lax.broadcasted_iota(jnp.int32, (1, LANES), 1)
    masks = ((lane < DF_HEAD_DIM), (lane >= DF_HEAD_DIM))
    sl = lambda h: slice(h * LANES, (h + 1) * LANES)
    scores = [_dot_nt(k_ref[0, :, sl(h)], jnp.where(m, q_ref[0, :, sl(h)], 0.0).astype(BF16))
              for h in range(nh) for m in masks]
    outs = []
    for h in range(nh):
        v_t = v_ref[0, sl(h), :]
        o = _softmax_pv_t(scores[2 * h], v_t) - lam * _softmax_pv_t(scores[2 * h + 1], v_t)
        o = o * lax.rsqrt(jnp.mean(o * o, 0, keepdims=True) + DF_EPS) * g_ref[...]
        outs.append(o * (1.0 - lambda_init))
    o_ref[0] = jnp.concatenate(outs, axis=0).T.astype(o_ref.dtype)


def _df_attn(q, k, v, p, lambda_init):
    b, s, _ = q.shape
    tq = 256
    nh = DF_HEADS_PER_STEP
    full = lambda a: pl.BlockSpec(a.shape, lambda bi, hi, qi: (0,) * a.ndim)
    consts = [p["lq1"], p["lk1"], p["lq2"], p["lk2"], p["subln"].reshape(-1, 1)]
    return pl.pallas_call(
        functools.partial(_df_attn_kernel, lambda_init=lambda_init),
        name="df_attn",
        grid=(b, DF_HEADS // nh, s // tq),
        in_specs=[pl.BlockSpec((1, tq, nh * LANES), lambda bi, hi, qi: (bi, qi, hi)),
                  pl.BlockSpec((1, s, nh * LANES), lambda bi, hi, qi: (bi, 0, hi)),
                  pl.BlockSpec((1, nh * LANES, s), lambda bi, hi, qi: (bi, hi, 0))]
                 + [full(a) for a in consts],
        out_specs=pl.BlockSpec((1, tq, nh * LANES), lambda bi, hi, qi: (bi, qi, hi)),
        out_shape=jax.ShapeDtypeStruct((b, s, DF_WIDTH), BF16),
        compiler_params=_cparams(("parallel", "parallel", "parallel")),
    )(q, k, v, *consts)


def _merge_kernel(orw, omla, odf, g0, g1, g2, w0, w1, w2, o_ref):
    acc = g0[...].astype(F32) * _dot(orw[...], w0[...])
    acc = acc + g1[...].astype(F32) * _dot(omla[...], w1[...])
    acc = acc + g2[...].astype(F32) * _dot(odf[...], w2[...])
    o_ref[...] = acc.astype(o_ref.dtype)


def _merge(o_rw, o_mla, o_df, gates, p):
    t = o_rw.shape[0]
    tm = _pick_tile(t, (1024, 512, 256))
    tn = 1024
    nj = D_MODEL // tn
    act = lambda w: pl.BlockSpec((tm, w), lambda i, j: (i, 0))
    gate = lambda br: pl.BlockSpec((tm, tn), lambda i, j: (i, j + br * nj))
    wt = lambda w: pl.BlockSpec((w, tn), lambda i, j: (0, j))
    return pl.pallas_call(
        _merge_kernel,
        name="merge",
        grid=(t // tm, nj),
        in_specs=[act(RW_WIDTH), act(MLA_WIDTH), act(DF_WIDTH), gate(0), gate(1), gate(2),
                  wt(RW_WIDTH), wt(MLA_WIDTH), wt(DF_WIDTH)],
        out_specs=pl.BlockSpec((tm, tn), lambda i, j: (i, j)),
        out_shape=jax.ShapeDtypeStruct((t, D_MODEL), BF16),
        compiler_params=_cparams(("parallel", "parallel")),
    )(o_rw, o_mla, o_df, gates, gates, gates, p["w_up_rw"], p["w_up_mla"], p["w_up_df"])


def _layer_norm(y, g, b):
    mu = jnp.mean(y, -1, keepdims=True)
    yc = y - mu
    var = jnp.mean(yc * yc, -1, keepdims=True)
    return yc * lax.rsqrt(var + LN_EPS) * g + b


def _wo_ln_kernel(m_ref, w_ref, x_ref, g_ref, b_ref, o_ref, ob_ref):
    y = ALPHA * x_ref[...] + _dot(m_ref[...], w_ref[...])
    out = _layer_norm(y, g_ref[...], b_ref[...])
    o_ref[...] = out
    ob_ref[...] = out.astype(BF16).reshape(ob_ref.shape)


def _wo_ln(merged, x, p):
    t = x.shape[0]
    tm = 256
    tile = pl.BlockSpec((tm, D_MODEL), lambda i: (i, 0))
    tile3 = pl.BlockSpec((tm, ROW_SUB, LANES), lambda i: (i, 0, 0))
    full = lambda a: pl.BlockSpec(a.shape, lambda i: (0,) * a.ndim)
    return pl.pallas_call(
        _wo_ln_kernel,
        name="wo_ln",
        grid=(t // tm,),
        in_specs=[tile, full(p["w_o"]), tile, full(p["ln1_g"]), full(p["ln1_b"])],
        out_specs=[tile, tile3],
        out_shape=[jax.ShapeDtypeStruct((t, D_MODEL), F32),
                   jax.ShapeDtypeStruct((t, ROW_SUB, LANES), BF16)],
        compiler_params=_cparams(("parallel",)),
    )(merged, p["w_o"], x, p["ln1_g"], p["ln1_b"])


ROUTER_TILE = 1024


def _router_kernel(x_ref, wh_ref, wl_ref, bias_ref, tri_ref, idx_o, wts_o, rank_o, cnt_o, cnt_scr):
    @pl.when(pl.program_id(0) == 0)
    def _():
        cnt_scr[...] = jnp.zeros_like(cnt_scr)

    xh, xl = _split_bf16(x_ref[...])
    wh, wl = wh_ref[...], wl_ref[...]
    logits = _dot_nt(wh, xh) + (_dot_nt(wh, xl) + _dot_nt(wl, xh))
    scores = jax.nn.sigmoid(logits)
    sel = scores + bias_ref[...]
    tm = sel.shape[1]

    def row(a, i):
        return a[i:i + 1, :]

    best = jnp.zeros((1, tm), jnp.int32)
    best_s = None
    for g in range(N_GROUPS):
        a, b, c, d = (row(sel, EXPERTS_PER_GROUP * g + j) for j in range(4))
        hi1, lo1 = jnp.maximum(a, b), jnp.minimum(a, b)
        hi2, lo2 = jnp.maximum(c, d), jnp.minimum(c, d)
        gs = jnp.maximum(hi1, hi2) + jnp.maximum(jnp.minimum(hi1, hi2), jnp.maximum(lo1, lo2))
        if g == 0:
            best_s = gs
        else:
            upd = gs > best_s
            best = jnp.where(upd, g, best)
            best_s = jnp.where(upd, gs, best_s)

    def pick(a, j):
        out = row(a, j)
        for g in range(1, N_GROUPS):
            out = jnp.where(best == g, row(a, EXPERTS_PER_GROUP * g + j), out)
        return out

    cand = [pick(sel, j) for j in range(EXPERTS_PER_GROUP)]
    csc = [pick(scores, j) for j in range(EXPERTS_PER_GROUP)]
    neg = jnp.float32(-jnp.inf)

    def argmax4(vals):
        bi, bv = jnp.zeros((1, tm), jnp.int32), vals[0]
        for j in range(1, EXPERTS_PER_GROUP):
            upd = vals[j] > bv
            bi = jnp.where(upd, j, bi)
            bv = jnp.where(upd, vals[j], bv)
        return bi

    i1 = argmax4(cand)
    i2 = argmax4([jnp.where(i1 == j, neg, cand[j]) for j in range(EXPERTS_PER_GROUP)])

    def take(vals, i):
        out = vals[0]
        for j in range(1, EXPERTS_PER_GROUP):
            out = jnp.where(i == j, vals[j], out)
        return out

    w1, w2 = take(csc, i1), take(csc, i2)
    tot = w1 + w2
    e1 = best * EXPERTS_PER_GROUP + i1
    e2 = best * EXPERTS_PER_GROUP + i2
    eid = lax.broadcasted_iota(jnp.int32, (N_EXPERTS, tm), 0)
    oh1 = eid == e1
    oh2 = eid == e2
    oh = (oh1 | oh2).astype(BF16)
    before = _dot(oh, tri_ref[...]) + cnt_scr[...][:, 0:1]
    r1 = jnp.sum(jnp.where(oh1, before, 0.0), 0, keepdims=True)
    r2 = jnp.sum(jnp.where(oh2, before, 0.0), 0, keepdims=True)
    idx_o[0:1, :] = e1
    idx_o[1:2, :] = e2
    wts_o[0:1, :] = w1 / tot
    wts_o[1:2, :] = w2 / tot
    rank_o[0:1, :] = r1.astype(jnp.int32)
    rank_o[1:2, :] = r2.astype(jnp.int32)
    new_cnt = cnt_scr[...] + jnp.sum(oh.astype(F32), 1, keepdims=True)
    cnt_scr[...] = new_cnt
    cnt_o[...] = new_cnt


def _router(x, p):
    t = x.shape[0]
    tm = ROUTER_TILE
    full = lambda a: pl.BlockSpec(a.shape, lambda i: (0,) * a.ndim)
    tok = pl.BlockSpec((TOP_K, tm), lambda i: (0, i))
    consts = [p["router_wt_hi"], p["router_wt_lo"], p["router_bias"], p["router_tri"]]
    return pl.pallas_call(
        _router_kernel,
        name="router",
        grid=(t // tm,),
        in_specs=[pl.BlockSpec((tm, D_MODEL), lambda i: (i, 0))] + [full(a) for a in consts],
        out_specs=[tok, tok, tok, pl.BlockSpec((N_EXPERTS, LANES), lambda i: (0, 0))],
        out_shape=[jax.ShapeDtypeStruct((TOP_K, t), jnp.int32), jax.ShapeDtypeStruct((TOP_K, t), F32),
                   jax.ShapeDtypeStruct((TOP_K, t), jnp.int32),
                   jax.ShapeDtypeStruct((N_EXPERTS, LANES), F32)],
        scratch_shapes=[pltpu.VMEM((N_EXPERTS, LANES), F32)],
        compiler_params=_cparams(("arbitrary",)),
    )(x, *consts)


DISPATCH_TILE = 512


def _dispatch_kernel(dest_ref, x_ref, init_ref, xs_ref, sem):
    del init_ref
    def copy(r, k):
        return pltpu.make_async_copy(x_ref.at[r], xs_ref.at[dest_ref[0, TOP_K * r + k]], sem)

    def start(r, carry):
        copy(r, 0).start()
        copy(r, 1).start()
        return carry

    def wait(r, carry):
        copy(r, 0).wait()
        copy(r, 1).wait()
        return carry

    lax.fori_loop(0, DISPATCH_TILE, start, 0, unroll=DMA_UNROLL)
    lax.fori_loop(0, DISPATCH_TILE, wait, 0, unroll=DMA_UNROLL)


def _dispatch(xb, dest, n_rows):
    t = xb.shape[0]
    nt = t // DISPATCH_TILE
    dest2 = dest.reshape(nt, 1, DISPATCH_TILE * TOP_K)
    init = jnp.zeros((n_rows, ROW_SUB, LANES), BF16)
    return pl.pallas_call(
        _dispatch_kernel,
        name="dispatch",
        grid=(nt,),
        in_specs=[pl.BlockSpec((None, 1, DISPATCH_TILE * TOP_K), lambda i: (i, 0, 0),
                               memory_space=pltpu.SMEM),
                  pl.BlockSpec((DISPATCH_TILE, ROW_SUB, LANES), lambda i: (i, 0, 0)),
                  pl.BlockSpec(memory_space=pl.ANY)],
        out_specs=pl.BlockSpec(memory_space=pl.ANY),
        out_shape=jax.ShapeDtypeStruct((n_rows, ROW_SUB, LANES), BF16),
        scratch_shapes=[pltpu.SemaphoreType.DMA(())],
        input_output_aliases={2: 0},
        compiler_params=_cparams(("arbitrary",)),
    )(dest2, xb, init)


def _ffn_kernel(te_ref, nu_ref, xs_ref, wg_ref, wu_ref, wd_ref, y_ref):
    del te_ref

    @pl.when(pl.program_id(0) < nu_ref[0])
    def _():
        xs = xs_ref[...].reshape(FFN_TILE, D_MODEL)
        h = jax.nn.silu(_dot(xs, wg_ref[0])) * _dot(xs, wu_ref[0])
        y_ref[...] = _dot(h.astype(BF16), wd_ref[0]).astype(BF16).reshape(y_ref.shape)

    @pl.when(pl.program_id(0) >= nu_ref[0])
    def _():
        y_ref[...] = jnp.zeros_like(y_ref)


def _ffn(xs, tile_expert, n_used, p):
    n_rows = xs.shape[0]
    tm = FFN_TILE
    grid_spec = pltpu.PrefetchScalarGridSpec(
        num_scalar_prefetch=2,
        grid=(n_rows // tm,),
        in_specs=[pl.BlockSpec((tm, ROW_SUB, LANES), lambda i, te, nu: (i, 0, 0)),
                  pl.BlockSpec((1, D_MODEL, D_EXPERT), lambda i, te, nu: (te[i], 0, 0)),
                  pl.BlockSpec((1, D_MODEL, D_EXPERT), lambda i, te, nu: (te[i], 0, 0)),
                  pl.BlockSpec((1, D_EXPERT, D_MODEL), lambda i, te, nu: (te[i], 0, 0))],
        out_specs=pl.BlockSpec((tm, ROW_SUB, LANES), lambda i, te, nu: (i, 0, 0)),
    )
    return pl.pallas_call(
        _ffn_kernel,
        name="ffn",
        grid_spec=grid_spec,
        out_shape=jax.ShapeDtypeStruct((n_rows, ROW_SUB, LANES), BF16),
        compiler_params=_cparams(("arbitrary",)),
    )(tile_expert, n_used, xs, p["ex_w_gate"], p["ex_w_up"], p["ex_w_down"])


COMBINE_TILE = 256


def _combine_ln_kernel(dest_ref, dest_next_ref, y_ref, x_ref, w_ref, g_ref, b_ref, o_ref, ob_ref,
                       buf, sem):
    i = pl.program_id(0)
    slot = i % 2

    def copy(idx_ref, s, r, k):
        return pltpu.make_async_copy(y_ref.at[idx_ref[0, TOP_K * r + k]], buf.at[s, k, r],
                                     sem.at[s])

    def issue(idx_ref, s):
        def body(r, carry):
            copy(idx_ref, s, r, 0).start()
            copy(idx_ref, s, r, 1).start()
            return carry
        lax.fori_loop(0, COMBINE_TILE, body, 0, unroll=DMA_UNROLL)

    @pl.when(i == 0)
    def _():
        issue(dest_ref, 0)

    @pl.when(i + 1 < pl.num_programs(0))
    def _():
        issue(dest_next_ref, 1 - slot)

    def wait(r, carry):
        copy(dest_ref, slot, r, 0).wait()
        copy(dest_ref, slot, r, 1).wait()
        return carry

    lax.fori_loop(0, COMBINE_TILE, wait, 0, unroll=DMA_UNROLL)
    w = w_ref[...]
    shape = (COMBINE_TILE, D_MODEL)
    ffn = (w[:, 0:1] * buf[slot, 0].reshape(shape).astype(F32)
           + w[:, 1:2] * buf[slot, 1].reshape(shape).astype(F32))
    out = _layer_norm(ALPHA * x_ref[...] + ffn, g_ref[...], b_ref[...])
    o_ref[...] = out
    ob_ref[...] = out.astype(BF16)


def _combine_ln(y, dest, x, wts, p):
    t = x.shape[0]
    tm = COMBINE_TILE
    nt = t // tm
    dest2 = dest.reshape(nt, 1, tm * TOP_K)
    tile = pl.BlockSpec((tm, D_MODEL), lambda i: (i, 0))
    full = lambda a: pl.BlockSpec(a.shape, lambda i: (0,) * a.ndim)
    return pl.pallas_call(
        _combine_ln_kernel,
        name="combine_ln",
        grid=(nt,),
        in_specs=[pl.BlockSpec((None, 1, tm * TOP_K), lambda i: (i, 0, 0), memory_space=pltpu.SMEM),
                  pl.BlockSpec((None, 1, tm * TOP_K), lambda i: (jnp.minimum(i + 1, nt - 1), 0, 0),
                               memory_space=pltpu.SMEM),
                  pl.BlockSpec(memory_space=pl.ANY), tile,
                  pl.BlockSpec((tm, TOP_K), lambda i: (i, 0)), full(p["ln2_g"]), full(p["ln2_b"])],
        out_specs=[tile, tile],
        out_shape=[jax.ShapeDtypeStruct((t, D_MODEL), F32), jax.ShapeDtypeStruct((t, D_MODEL), BF16)],
        scratch_shapes=[pltpu.VMEM((2, TOP_K, tm, ROW_SUB, LANES), BF16),
                        pltpu.SemaphoreType.DMA((2,))],
        compiler_params=_cparams(("arbitrary",)),
    )(dest2, dest2, y, x, wts, p["ln2_g"], p["ln2_b"])


def _moe_ln(x, xb, p):
    t = x.shape[0]
    idx, wts, rank, counts = _router(x, p)
    counts = counts[:, 0].astype(jnp.int32)
    tiles = (counts + FFN_TILE - 1) // FFN_TILE
    tile_end = jnp.cumsum(tiles)
    row_start = (tile_end - tiles) * FFN_TILE
    n_tiles = (t * TOP_K) // FFN_TILE + N_EXPERTS
    start_of = jnp.sum(jnp.where(idx[..., None] == jnp.arange(N_EXPERTS), row_start, 0), -1)
    dest = (start_of + rank).T.reshape(-1)
    tile_expert = jnp.minimum(jnp.sum(jnp.arange(n_tiles)[:, None] >= tile_end[None, :], axis=1),
                              N_EXPERTS - 1).astype(jnp.int32)
    n_used = tile_end[-1:].astype(jnp.int32)
    xs = _dispatch(xb, dest, n_tiles * FFN_TILE)
    y = _ffn(xs, tile_expert, n_used, p)
    return _combine_ln(y, dest, x, wts.T, p)


def _block_diag_ones(n, blk):
    i = jnp.arange(n) // blk
    return (i[:, None] == i[None, :]).astype(BF16)


def _prep_layer(l, w):
    row = lambda a: a.reshape(1, -1).astype(F32)
    w_in = w["w_in"][l]
    o1, o2, o3 = RW_IN, RW_IN + MLA_IN, RW_IN + MLA_IN + DF_IN
    p = {
        "w_in_rw": w_in[:, :o1].astype(BF16),
        "w_in_mla": jnp.pad(w_in[:, o1:o2], ((0, 0), (0, MLA_IN_PAD - MLA_IN))).astype(BF16),
        "w_in_df": w_in[:, o2:o3].astype(BF16),
        "w_in_gate": w_in[:, o3:].astype(BF16),
        "mu_prev": row(w["shift_prev"][l]), "mu_next": row(w["shift_next"][l]),
        "w0": row(w["rw_w0"][l]), "a0": row(w["rw_a0"][l]),
        "g2": w["rw_g2"][l].astype(BF16),
        "k_k": row(w["rw_k_k"][l]), "k_a": row(w["rw_k_a"][l]), "r_k": row(w["rw_r_k"][l]),
        "lnx_g": row(w["rw_lnx_g"][l]), "lnx_b": row(w["rw_lnx_b"][l]),
        "bd": _block_diag_ones(RW_WIDTH, RW_HEAD_DIM),
        "q_norm": row(w["mla_q_norm"][l]), "kv_norm": row(w["mla_kv_norm"][l]),
        "lq1": row(w["df_lq1"][l]), "lk1": row(w["df_lk1"][l]),
        "lq2": row(w["df_lq2"][l]), "lk2": row(w["df_lk2"][l]),
        "subln": row(w["df_subln"][l]),
        "w_up_rw": w["w_up_rw"][l].astype(BF16), "w_up_mla": w["w_up_mla"][l].astype(BF16),
        "w_up_df": w["w_up_df"][l].astype(BF16), "w_o": w["w_o"][l].astype(BF16),
        "ln1_g": row(w["ln1_g"][l]), "ln1_b": row(w["ln1_b"][l]),
        "ln2_g": row(w["ln2_g"][l]), "ln2_b": row(w["ln2_b"][l]),
        "ex_w_gate": w["ex_w_gate"][l].astype(BF16), "ex_w_up": w["ex_w_up"][l].astype(BF16),
        "ex_w_down": w["ex_w_down"][l].astype(BF16),
    }
    zc = jnp.zeros((DECAY_LORA, RW_WIDTH), F32)
    w2 = w["rw_w2"][l]
    a2 = w["rw_a2"][l]
    p["w2cat"] = jnp.block([[w2[0], zc], [zc, w2[1]]]).astype(BF16)
    p["a2cat"] = jnp.block([[a2[0], zc], [zc, a2[1]]]).astype(BF16)
    wq = w["mla_w_uq"][l].reshape(MLA_Q_LORA, MLA_HEADS, MLA_NOPE + MLA_ROPE)
    p["wq"] = jnp.pad(wq, ((0, 0), (0, 0), (0, MLA_HEAD_PAD - MLA_NOPE - MLA_ROPE))
                      ).reshape(MLA_Q_LORA, -1).astype(BF16)
    wkv = w["mla_w_ukv"][l].reshape(MLA_KV_LORA, MLA_HEADS, MLA_NOPE + MLA_V)
    wk_nope = jnp.pad(wkv[:, :, :MLA_NOPE], ((0, 0), (0, 0), (0, MLA_HEAD_PAD - MLA_NOPE)))
    place = jnp.zeros((LANES, MLA_HEADS, MLA_HEAD_PAD), F32)
    j = jnp.arange(MLA_ROPE)
    place = place.at[j, :, MLA_NOPE + j].set(1.0)
    p["wk"] = jnp.concatenate([wk_nope, place], axis=0).reshape(MLA_KV_LORA + LANES, -1).astype(BF16)
    p["wv"] = wkv[:, :, MLA_NOPE:].reshape(MLA_KV_LORA, -1).T.astype(BF16)
    return p


def _trunk(x3, layers, shared):
    b, s, d = x3.shape
    t = b * s
    tabs = {
        "mla_q": _rope_tables(s, MLA_ROPE, MLA_NOPE, LANES),
        "mla_k": _rope_tables(s, MLA_ROPE, 0, LANES),
        "df": _rope_tables(s, DF_HEAD_DIM, 0, DF_HEAD_DIM),
    }
    x = x3.reshape(t, d)
    xb = x.astype(BF16)
    for l, p in enumerate(layers):
        p = dict(p, **shared)
        lambda_init = 0.8 - 0.6 * math.exp(-0.3 * l)
        z_rw = _matmul(xb, p["w_in_rw"], F32).reshape(b, s, -1)
        z_mla = _matmul(xb, p["w_in_mla"], F32).reshape(b, s, -1)
        z_df = _matmul(xb, p["w_in_df"], F32).reshape(b, s, -1)
        gates = _matmul(xb, p["w_in_gate"], BF16, act="sigmoid")
        r, v, kk, g, bonus, logd, a, kdir = _rw_prep(z_rw, p)
        yf, yb = _rw_scan(r, v, kk, logd, a, kdir)
        o_rw = _rw_post(yf.reshape(t, -1), yb.reshape(t, -1), bonus.reshape(t, -1),
                        g.reshape(t, -1), p)
        q, k, v2 = _mla_prep(z_mla, p, tabs)
        o_mla = _mla_attn(q, k, v2).reshape(t, -1)
        q, k, v2 = _df_prep(z_df, tabs)
        o_df = _df_attn(q, k, v2, p, lambda_init).reshape(t, -1)
        merged = _merge(o_rw, o_mla, o_df, gates, p)
        x, xb = _wo_ln(merged, x, p)
        x, xb = _moe_ln(x, xb, p)
    return x.reshape(b, s, d)


def kernel(x_prompt, x_sample, w_in, shift_prev, shift_next, rw_w0, rw_w2, rw_a0, rw_a2, rw_g2,
           rw_k_k, rw_k_a, rw_r_k, rw_lnx_g, rw_lnx_b, mla_q_norm, mla_kv_norm, mla_w_uq,
           mla_w_ukv, df_lq1, df_lk1, df_lq2, df_lk2, df_subln, w_up_rw, w_up_mla, w_up_df, w_o,
           ln1_g, ln1_b, ln2_g, ln2_b, router_w, router_bias, ex_w_gate, ex_w_up, ex_w_down):
    w = dict(w_in=w_in, shift_prev=shift_prev, shift_next=shift_next, rw_w0=rw_w0, rw_w2=rw_w2,
             rw_a0=rw_a0, rw_a2=rw_a2, rw_g2=rw_g2, rw_k_k=rw_k_k, rw_k_a=rw_k_a, rw_r_k=rw_r_k,
             rw_lnx_g=rw_lnx_g, rw_lnx_b=rw_lnx_b, mla_q_norm=mla_q_norm,
             mla_kv_norm=mla_kv_norm, mla_w_uq=mla_w_uq, mla_w_ukv=mla_w_ukv, df_lq1=df_lq1,
             df_lk1=df_lk1, df_lq2=df_lq2, df_lk2=df_lk2, df_subln=df_subln, w_up_rw=w_up_rw,
             w_up_mla=w_up_mla, w_up_df=w_up_df, w_o=w_o, ln1_g=ln1_g, ln1_b=ln1_b, ln2_g=ln2_g,
             ln2_b=ln2_b, ex_w_gate=ex_w_gate, ex_w_up=ex_w_up, ex_w_down=ex_w_down)
    layers = [_prep_layer(l, w) for l in range(DEPTH)]
    rwt = router_w.T.astype(F32)
    rwt_hi = rwt.astype(BF16)
    ti = jnp.arange(ROUTER_TILE)
    shared = {
        "router_wt_hi": rwt_hi,
        "router_wt_lo": (rwt - rwt_hi.astype(F32)).astype(BF16),
        "router_bias": router_bias.reshape(N_EXPERTS, 1).astype(F32),
        "router_tri": (ti[:, None] < ti[None, :]).astype(BF16),
    }
    return (_trunk(x_prompt, layers, shared), _trunk(x_sample, layers, shared))
```

```python
import functools
import math

import jax
import jax.numpy as jnp
from jax import lax
from jax.experimental import pallas as pl
from jax.experimental.pallas import tpu as pltpu

F32 = jnp.float32
BF16 = jnp.bfloat16

D_MODEL = 2048
DEPTH = 2
RW_HEADS, RW_HEAD_DIM = 12, 64
RW_WIDTH = RW_HEADS * RW_HEAD_DIM
DECAY_LORA, ICLR_LORA, GATE_LORA, N_DIR = 64, 64, 128, 2
GN_EPS = 64e-5
MLA_HEADS, MLA_NOPE, MLA_ROPE, MLA_V = 8, 64, 32, 64
MLA_Q_LORA, MLA_KV_LORA = 512, 256
MLA_WIDTH = MLA_HEADS * MLA_V
DF_HEADS, DF_HEAD_DIM = 6, 64
DF_WIDTH = DF_HEADS * 2 * DF_HEAD_DIM
DF_EPS = 1e-5
N_BRANCH = 3
N_EXPERTS, N_GROUPS, TOP_K, D_EXPERT = 16, 4, 2, 1024
EXPERTS_PER_GROUP = N_EXPERTS // N_GROUPS
ROPE_THETA = 10000.0
LN_EPS = 1e-5
RMS_EPS = 1e-6
ALPHA = (2 * DEPTH) ** 0.25
RW_IN = 3 * RW_WIDTH + N_DIR * DECAY_LORA + N_DIR * ICLR_LORA + GATE_LORA
MLA_IN = MLA_Q_LORA + MLA_KV_LORA + MLA_ROPE
DF_IN = 3 * DF_WIDTH
GATE_IN = N_BRANCH * D_MODEL

LANES = 128
MLA_IN_PAD = 896
MLA_HEAD_PAD = 128
RW_CHUNK = 64
RW_PAIRS = RW_WIDTH // LANES
RW_BATCH = 4
VMEM_LIMIT = 58 * 1024 * 1024
FFN_TILE = 512
LOG2E = math.log2(math.e)
MLA_HEADS_PER_STEP = 8
DF_HEADS_PER_STEP = 3
DMA_UNROLL = 8
HALO = 8
ROW_SUB = D_MODEL // LANES


def _cparams(sem):
    return pltpu.CompilerParams(dimension_semantics=sem, vmem_limit_bytes=VMEM_LIMIT)


def _dot(a, b):
    return jnp.dot(a, b, preferred_element_type=F32)


def _dot_nt(a, b):
    return lax.dot_general(a, b, (((1,), (1,)), ((), ())), preferred_element_type=F32)


def _dot_tn(a, b):
    return lax.dot_general(a, b, (((0,), (0,)), ((), ())), preferred_element_type=F32)


def _split_bf16(x):
    hi = x.astype(BF16)
    lo = (x - hi.astype(F32)).astype(BF16)
    return hi, lo


def _pick_tile(n, candidates):
    for c in candidates:
        if n % c == 0:
            return c
    raise ValueError(f"no tile for {n}")


def _mm_kernel(x_ref, w_ref, o_ref, *, act):
    acc = _dot(x_ref[...], w_ref[...])
    if act == "sigmoid":
        acc = jax.nn.sigmoid(acc)
    o_ref[...] = acc.astype(o_ref.dtype)


def _matmul(x, w, out_dtype, act=None):
    m, k = x.shape
    n = w.shape[1]
    tm = _pick_tile(m, (1024, 512, 256, 128))
    tn = _pick_tile(n, (1024, 896, 768, 512, 256, 128))
    return pl.pallas_call(
        functools.partial(_mm_kernel, act=act),
        name="mm",
        grid=(n // tn, m // tm),
        in_specs=[pl.BlockSpec((tm, k), lambda j, i: (i, 0)),
                  pl.BlockSpec((k, tn), lambda j, i: (0, j))],
        out_specs=pl.BlockSpec((tm, tn), lambda j, i: (i, j)),
        out_shape=jax.ShapeDtypeStruct((m, n), out_dtype),
        compiler_params=_cparams(("parallel", "parallel")),
    )(x, w)


def _rw_prep_kernel(z_ref, hp_ref, hn_ref, mup_ref, mun_ref, w2_ref, a2_ref, g2_ref, w0_ref,
                    a0_ref, kk_k_ref, k_a_ref, r_k_ref, bd_ref,
                    r_o, v_o, kk_o, g_o, bonus_o, logd_o, a_o, kdir_o):
    z = z_ref[0]
    tr = z.shape[0]
    row = lax.broadcasted_iota(jnp.int32, (tr, 1), 0)
    ti = pl.program_id(1)
    halo_prev = jnp.where(ti == 0, 0.0, hp_ref[0, HALO - 1:HALO, :])
    halo_next = jnp.where(ti == pl.num_programs(1) - 1, 0.0, hn_ref[0, 0:1, :])
    prev = jnp.where(row == 0, halo_prev, pltpu.roll(z, 1, 0))
    nxt = jnp.where(row == tr - 1, halo_next, pltpu.roll(z, tr - 1, 0))
    zs = z + mup_ref[...] * (prev - z) + mun_ref[...] * (nxt - z)
    c = RW_WIDTH
    r, k, v = zs[:, 0:c], zs[:, c:2 * c], zs[:, 2 * c:3 * c]
    wl = zs[:, 3 * c:3 * c + 128]
    al = zs[:, 3 * c + 128:3 * c + 256]
    gl = zs[:, 3 * c + 256:3 * c + 384]
    w_raw = w0_ref[...] + _dot(jnp.tanh(wl).astype(BF16), w2_ref[...])
    a = jax.nn.sigmoid(a0_ref[...] + _dot(al.astype(BF16), a2_ref[...]))
    g = _dot(jax.nn.sigmoid(gl).astype(BF16), g2_ref[...])
    logd = (-math.exp(-0.5)) * jax.nn.sigmoid(w_raw)
    bd = bd_ref[...]

    def head_sum(x):
        hi, lo = _split_bf16(x)
        return _dot(hi, bd) + _dot(lo, bd)

    kk = k * kk_k_ref[...]
    kk = kk * lax.rsqrt(head_sum(kk * kk) + 1e-12)
    k_a = k_a_ref[...]
    kd0 = k * (1.0 + (a[:, 0:c] - 1.0) * k_a)
    kd1 = k * (1.0 + (a[:, c:2 * c] - 1.0) * k_a)
    bonus = head_sum(r * r_k_ref[...] * (kd0 + kd1)) * v
    r_o[0] = r
    v_o[0] = v
    kk_o[0] = kk
    g_o[0] = g
    bonus_o[0] = bonus
    logd_o[0] = logd
    a_o[0] = a
    kdir_o[0, :, 0:c] = kd0
    kdir_o[0, :, c:2 * c] = kd1


def _rw_prep(z_rw, p):
    b, s, _ = z_rw.shape
    tr = 256
    nt = s // tr
    hpt = tr // HALO
    c = RW_WIDTH
    tile = lambda w: pl.BlockSpec((1, tr, w), lambda bi, ti: (bi, ti, 0))
    halo_prev = pl.BlockSpec((1, HALO, RW_IN), lambda bi, ti: (bi, jnp.maximum(ti * hpt - 1, 0), 0))
    halo_next = pl.BlockSpec((1, HALO, RW_IN),
                             lambda bi, ti: (bi, jnp.minimum((ti + 1) * hpt, s // HALO - 1), 0))
    full = lambda a: pl.BlockSpec(a.shape, lambda bi, ti: (0,) * a.ndim)
    consts = [p["mu_prev"], p["mu_next"], p["w2cat"], p["a2cat"], p["g2"], p["w0"], p["a0"],
              p["k_k"], p["k_a"], p["r_k"], p["bd"]]
    out_w = [c, c, c, c, c, 2 * c, 2 * c, 2 * c]
    return pl.pallas_call(
        _rw_prep_kernel,
        name="rw_prep",
        grid=(b, nt),
        in_specs=[tile(RW_IN), halo_prev, halo_next] + [full(a) for a in consts],
        out_specs=[tile(w) for w in out_w],
        out_shape=[jax.ShapeDtypeStruct((b, s, w), F32) for w in out_w],
        compiler_params=_cparams(("parallel", "parallel")),
    )(z_rw, z_rw, z_rw, *consts)


def _rw_scan_kernel(rf, vf, kkf, ldf, af, kdf, rb, vb, kkb, ldb, ab, kdb, yf_o, yb_o, z_scr):
    @pl.when(pl.program_id(1) == 0)
    def _():
        z_scr[...] = jnp.zeros_like(z_scr)

    c = RW_CHUNK
    ti = lax.broadcasted_iota(jnp.int32, (c, c), 0)
    tj = lax.broadcasted_iota(jnp.int32, (c, c), 1)
    lane = lax.broadcasted_iota(jnp.int32, (1, LANES), 1)
    lane_m = (lane < RW_HEAD_DIM, lane >= RW_HEAD_DIM)
    bi = lax.broadcasted_iota(jnp.int32, (LANES, LANES), 0) // RW_HEAD_DIM
    bj = lax.broadcasted_iota(jnp.int32, (LANES, LANES), 1) // RW_HEAD_DIM
    bdmask = bi == bj
    incl = ((ti >= tj), (ti <= tj))
    strict = ((ti > tj), (ti < tj))
    incl_bf = tuple(m.astype(BF16) for m in incl)
    t2 = lax.broadcasted_iota(jnp.int32, (2 * c, 2 * c), 0)
    j2 = lax.broadcasted_iota(jnp.int32, (2 * c, 2 * c), 1) & (c - 1)
    diag_ok = (t2 >= c) & ((t2 & (c - 1)) == j2)
    t2 = t2 & (c - 1)
    mask_ar = ((t2 > j2) | diag_ok, (t2 < j2) | diag_ok)
    in_refs = ((rf, vf, kkf, ldf, af, kdf), (rb, vb, kkb, ldb, ab, kdb))
    out_refs = (yf_o, yb_o)

    def bd_rows(x):
        return jnp.concatenate([jnp.where(lane_m[0], x, 0.0), jnp.where(lane_m[1], x, 0.0)],
                               axis=0).astype(BF16)

    probs = [(d, p, bb) for d in range(N_DIR) for p in range(RW_PAIRS) for bb in range(RW_BATCH)]
    sl = lambda p: slice(p * LANES, (p + 1) * LANES)
    val = {s: [ref[s[2], :, sl(s[1])] for ref in in_refs[s[0]]] for s in probs}
    w = RW_WIDTH
    logp_all = {}
    for d in range(N_DIR):
        for bb in range(RW_BATCH):
            hi, lo = _split_bf16(in_refs[d][3][bb])
            both = _dot(incl_bf[d], jnp.concatenate([hi, lo], axis=1))
            logp_all[d, bb] = both[:, 0:w] + both[:, w:2 * w]
    logp = {s: logp_all[s[0], s[2]][:, sl(s[1])] for s in probs}
    ones = jnp.ones((2 * c, LANES), BF16)
    logpc = {s: _dot_tn(jnp.concatenate(_split_bf16(val[s][3]), axis=0), ones) for s in probs}
    at, rt, bt, kt, vv, z, zb = {}, {}, {}, {}, {}, {}, {}
    for s in probs:
        r, v, kk, ld, a, kd = val[s]
        pinv = jnp.exp(-logp[s])
        at[s] = -(kk * jnp.exp(logp[s] - ld))
        rt[s] = r * jnp.exp(logp[s])
        bt[s] = kk * a * pinv
        kt[s] = kd * pinv
        vv[s] = v
        z[s] = z_scr[s]
        zb[s] = z[s].astype(BF16)
    lhs = {s: jnp.concatenate([at[s], rt[s]], axis=0).astype(BF16) for s in probs}
    g = {s: _dot_nt(lhs[s], jnp.concatenate([bd_rows(bt[s]), bd_rows(kt[s])], axis=0))
         for s in probs}
    a_bb = {s: jnp.where(mask_ar[s[0]], g[s][:, 0:LANES], 0.0).astype(BF16) for s in probs}
    a_kk = {s: jnp.where(mask_ar[s[0]], g[s][:, LANES:], 0.0).astype(BF16) for s in probs}
    zv = {s: _dot(jnp.concatenate([lhs[s], a_kk[s]], axis=1),
                  jnp.concatenate([zb[s], bd_rows(vv[s])], axis=0)) for s in probs}
    x = {s: zv[s][0:c] for s in probs}
    ap = {s: a_bb[s][0:c] for s in probs}
    for i in range(6):
        if i < 5:
            prod = {s: _dot(ap[s], jnp.concatenate([bd_rows(x[s]), bd_rows(ap[s])], axis=1))
                    for s in probs}
            x = {s: x[s] + prod[s][:, 0:LANES] for s in probs}
            ap = {s: prod[s][:, LANES:].astype(BF16) for s in probs}
        else:
            x = {s: x[s] + _dot(ap[s], bd_rows(x[s])) for s in probs}
    for s in probs:
        d, p, bb = s
        out_refs[d][bb, :, sl(p)] = zv[s][c:] + _dot(a_bb[s][c:], bd_rows(x[s]))
        upd = _dot_tn(jnp.concatenate([bt[s], kt[s]], axis=0).astype(BF16),
                      jnp.concatenate([x[s], vv[s]], axis=0).astype(BF16))
        z_scr[s] = jnp.where(bdmask, jnp.exp(logpc[s]) * (z[s] + upd), 0.0)


def _rw_scan(r, v, kk, logd, a, kdir):
    b, s, c = r.shape
    nc = s // RW_CHUNK
    nb = RW_BATCH
    fwd = pl.BlockSpec((nb, RW_CHUNK, c), lambda bi, ci: (bi, ci, 0))
    bwd = pl.BlockSpec((nb, RW_CHUNK, c), lambda bi, ci: (bi, nc - 1 - ci, 0))
    bwd_dir = pl.BlockSpec((nb, RW_CHUNK, c), lambda bi, ci: (bi, nc - 1 - ci, 1))
    return pl.pallas_call(
        _rw_scan_kernel,
        name="rw_scan",
        grid=(b // nb, nc),
        in_specs=[fwd, fwd, fwd, fwd, fwd, fwd, bwd, bwd, bwd, bwd_dir, bwd_dir, bwd_dir],
        out_specs=[fwd, bwd],
        out_shape=[jax.ShapeDtypeStruct((b, s, c), F32)] * 2,
        scratch_shapes=[pltpu.VMEM((N_DIR, RW_PAIRS, nb, LANES, LANES), F32)],
        compiler_params=_cparams(("parallel", "arbitrary")),
    )(r, v, kk, logd, a, kdir, r, v, kk, logd, a, kdir)


def _rw_post_kernel(yf_ref, yb_ref, bonus_ref, g_ref, lg_ref, lb_ref, bd_ref, o_ref):
    y = yf_ref[...] + yb_ref[...]
    bd = bd_ref[...]

    def head_mean(x):
        hi, lo = _split_bf16(x)
        return (_dot(hi, bd) + _dot(lo, bd)) * (1.0 / RW_HEAD_DIM)

    mu = head_mean(y)
    yc = y - mu
    var = head_mean(yc * yc)
    yn = yc * lax.rsqrt(var + GN_EPS) * lg_ref[...] + lb_ref[...]
    o_ref[...] = ((yn + bonus_ref[...]) * g_ref[...]).astype(o_ref.dtype)


def _rw_post(yf, yb, bonus, g, p):
    t, c = yf.shape
    tm = _pick_tile(t, (1024, 512, 256))
    tile = pl.BlockSpec((tm, c), lambda i: (i, 0))
    full = lambda a: pl.BlockSpec(a.shape, lambda i: (0,) * a.ndim)
    consts = [p["lnx_g"], p["lnx_b"], p["bd"]]
    return pl.pallas_call(
        _rw_post_kernel,
        name="rw_post",
        grid=(t // tm,),
        in_specs=[tile] * 4 + [full(a) for a in consts],
        out_specs=tile,
        out_shape=jax.ShapeDtypeStruct((t, c), BF16),
        compiler_params=_cparams(("parallel",)),
    )(yf, yb, bonus, g, *consts)


def _rope_block(x, cos, s_up, s_dn, half):
    return x * cos + pltpu.roll(x, half, 1) * s_up + pltpu.roll(x, LANES - half, 1) * s_dn


def _rope_tables(s, dim, lane_of_x1, period):
    half = dim // 2
    inv_freq = jnp.power(ROPE_THETA, -jnp.arange(half, dtype=F32) * (2.0 / dim))
    ang = jnp.arange(s, dtype=F32)[:, None] * inv_freq[None, :]
    lane = jnp.arange(LANES) % period - lane_of_x1
    in_x1 = (lane >= 0) & (lane < half)
    in_x2 = (lane >= half) & (lane < dim)
    j = jnp.clip(jnp.where(in_x2, lane - half, lane), 0, half - 1)
    cos = jnp.cos(ang)[:, j]
    sin = jnp.sin(ang)[:, j]
    cos_t = jnp.where(in_x1 | in_x2, cos, 1.0)
    s_up = jnp.where(in_x2, sin, 0.0)
    s_dn = jnp.where(in_x1, -sin, 0.0)
    return cos_t.astype(F32), s_up.astype(F32), s_dn.astype(F32)


def _mla_prep_kernel(z_ref, qg_ref, kvg_ref, wq_ref, wk_ref, wv_ref,
                     cq_ref, squ_ref, sqd_ref, ck_ref, sku_ref, skd_ref, q_o, k_o, v_o):
    z = z_ref[0]
    c_q = z[:, 0:MLA_Q_LORA]
    c_kv = z[:, MLA_Q_LORA:MLA_Q_LORA + MLA_KV_LORA]
    kr = z[:, MLA_Q_LORA + MLA_KV_LORA:MLA_IN_PAD]
    c_q = c_q * lax.rsqrt(jnp.mean(c_q * c_q, -1, keepdims=True) + RMS_EPS) * qg_ref[...]
    c_kv = c_kv * lax.rsqrt(jnp.mean(c_kv * c_kv, -1, keepdims=True) + RMS_EPS) * kvg_ref[...]
    q = _dot(c_q.astype(BF16), wq_ref[...])
    scale = (MLA_NOPE + MLA_ROPE) ** -0.5 * LOG2E
    cq, squ, sqd = cq_ref[...], squ_ref[...], sqd_ref[...]
    for h in range(MLA_HEADS):
        sl = slice(h * MLA_HEAD_PAD, (h + 1) * MLA_HEAD_PAD)
        q_o[0, :, sl] = (_rope_block(q[:, sl], cq, squ, sqd, MLA_ROPE // 2) * scale).astype(BF16)
    kr = _rope_block(kr, ck_ref[...], sku_ref[...], skd_ref[...], MLA_ROPE // 2)
    ckv_b = c_kv.astype(BF16)
    k_in = jnp.concatenate([ckv_b, kr.astype(BF16)], axis=1)
    k_o[0] = _dot(k_in, wk_ref[...]).astype(BF16)
    v_o[0] = _dot_nt(wv_ref[...], ckv_b).astype(BF16)


def _mla_prep(z_mla, p, tabs):
    b, s, _ = z_mla.shape
    tr = 512
    tile = lambda w: pl.BlockSpec((1, tr, w), lambda bi, ti: (bi, ti, 0))
    full = lambda a: pl.BlockSpec(a.shape, lambda bi, ti: (0,) * a.ndim)
    tab = pl.BlockSpec((tr, LANES), lambda bi, ti: (ti, 0))
    consts = [p["q_norm"], p["kv_norm"], p["wq"], p["wk"], p["wv"]]
    hq = MLA_HEADS * MLA_HEAD_PAD
    return pl.pallas_call(
        _mla_prep_kernel,
        name="mla_prep",
        grid=(b, s // tr),
        in_specs=[tile(MLA_IN_PAD)] + [full(a) for a in consts] + [tab] * 6,
        out_specs=[tile(hq), tile(hq),
                   pl.BlockSpec((1, MLA_WIDTH, tr), lambda bi, ti: (bi, 0, ti))],
        out_shape=[jax.ShapeDtypeStruct((b, s, hq), BF16), jax.ShapeDtypeStruct((b, s, hq), BF16),
                   jax.ShapeDtypeStruct((b, MLA_WIDTH, s), BF16)],
        compiler_params=_cparams(("parallel", "parallel")),
    )(z_mla, *consts, *tabs["mla_q"], *tabs["mla_k"])


def _softmax_pv_t(s_t, v_t):
    pr = jnp.exp2(s_t - jnp.max(s_t, 0, keepdims=True))
    l = jnp.sum(pr, 0, keepdims=True)
    return _dot(v_t, pr.astype(BF16)) * (1.0 / l)


def _mla_attn_kernel(q_ref, k_ref, v_ref, o_ref):
    nh = MLA_HEADS_PER_STEP
    sl = lambda h: slice(h * MLA_HEAD_PAD, (h + 1) * MLA_HEAD_PAD)
    scores = [_dot_nt(k_ref[0, :, sl(h)], q_ref[0, :, sl(h)]) for h in range(nh)]
    outs = []
    for h, s_t in enumerate(scores):
        pair = h // 2
        o_t = _softmax_pv_t(s_t, v_ref[0, pair * LANES:(pair + 1) * LANES, :])
        outs.append(o_t[(h % 2) * MLA_V:(h % 2 + 1) * MLA_V])
    o_ref[0] = jnp.concatenate(outs, axis=0).T.astype(o_ref.dtype)


def _mla_attn(q, k, v):
    b, s, _ = q.shape
    tq = 256
    nh = MLA_HEADS_PER_STEP
    return pl.pallas_call(
        _mla_attn_kernel,
        name="mla_attn",
        grid=(b, MLA_HEADS // nh, s // tq),
        in_specs=[pl.BlockSpec((1, tq, nh * MLA_HEAD_PAD), lambda bi, pi, qi: (bi, qi, pi)),
                  pl.BlockSpec((1, s, nh * MLA_HEAD_PAD), lambda bi, pi, qi: (bi, 0, pi)),
                  pl.BlockSpec((1, nh * MLA_V, s), lambda bi, pi, qi: (bi, pi, 0))],
        out_specs=pl.BlockSpec((1, tq, nh * MLA_V), lambda bi, pi, qi: (bi, qi, pi)),
        out_shape=jax.ShapeDtypeStruct((b, s, MLA_WIDTH), BF16),
        compiler_params=_cparams(("parallel", "parallel", "parallel")),
    )(q, k, v)


def _df_prep_kernel(z_ref, c_ref, su_ref, sd_ref, q_o, k_o, v_o):
    cos, s_up, s_dn = c_ref[...], su_ref[...], sd_ref[...]
    scale = DF_HEAD_DIM ** -0.5 * LOG2E
    for h in range(DF_HEADS):
        sl = slice(h * LANES, (h + 1) * LANES)
        q_o[0, :, sl] = (_rope_block(z_ref[0, :, sl], cos, s_up, s_dn, DF_HEAD_DIM // 2)
                         * scale).astype(BF16)
        slk = slice(DF_WIDTH + h * LANES, DF_WIDTH + (h + 1) * LANES)
        k_o[0, :, sl] = _rope_block(z_ref[0, :, slk], cos, s_up, s_dn,
                                    DF_HEAD_DIM // 2).astype(BF16)
    v_o[0] = z_ref[0, :, 2 * DF_WIDTH:3 * DF_WIDTH].T.astype(BF16)


def _df_prep(z_df, tabs):
    b, s, _ = z_df.shape
    tr = 512
    tile = lambda w: pl.BlockSpec((1, tr, w), lambda bi, ti: (bi, ti, 0))
    tab = pl.BlockSpec((tr, LANES), lambda bi, ti: (ti, 0))
    return pl.pallas_call(
        _df_prep_kernel,
        name="df_prep",
        grid=(b, s // tr),
        in_specs=[tile(DF_IN)] + [tab] * 3,
        out_specs=[tile(DF_WIDTH), tile(DF_WIDTH),
                   pl.BlockSpec((1, DF_WIDTH, tr), lambda bi, ti: (bi, 0, ti))],
        out_shape=[jax.ShapeDtypeStruct((b, s, DF_WIDTH), BF16)] * 2
                  + [jax.ShapeDtypeStruct((b, DF_WIDTH, s), BF16)],
        compiler_params=_cparams(("parallel", "parallel")),
    )(z_df, *tabs["df"])


def _df_attn_kernel(q_ref, k_ref, v_ref, lq1, lk1, lq2, lk2, g_ref, o_ref, *, lambda_init):
    lam = (jnp.exp(jnp.sum(lq1[...] * lk1[...], -1, keepdims=True))
           - jnp.exp(jnp.sum(lq2[...] * lk2[...], -1, keepdims=True)) + lambda_init)
    nh = DF_HEADS_PER_STEP
    lane = lax.broadcasted_iota(jnp.int32, (1, LANES), 1)
    masks = ((lane < DF_HEAD_DIM), (lane >= DF_HEAD_DIM))
    sl = lambda h: slice(h * LANES, (h + 1) * LANES)
    scores = [_dot_nt(k_ref[0, :, sl(h)], jnp.where(m, q_ref[0, :, sl(h)], 0.0).astype(BF16))
              for h in range(nh) for m in masks]
    outs = []
    for h in range(nh):
        v_t = v_ref[0, sl(h), :]
        o = _softmax_pv_t(scores[2 * h], v_t) - lam * _softmax_pv_t(scores[2 * h + 1], v_t)
        o = o * lax.rsqrt(jnp.mean(o * o, 0, keepdims=True) + DF_EPS) * g_ref[...]
        outs.append(o * (1.0 - lambda_init))
    o_ref[0] = jnp.concatenate(outs, axis=0).T.astype(o_ref.dtype)


def _df_attn(q, k, v, p, lambda_init):
    b, s, _ = q.shape
    tq = 256
    nh = DF_HEADS_PER_STEP
    full = lambda a: pl.BlockSpec(a.shape, lambda bi, hi, qi: (0,) * a.ndim)
    consts = [p["lq1"], p["lk1"], p["lq2"], p["lk2"], p["subln"].reshape(-1, 1)]
    return pl.pallas_call(
        functools.partial(_df_attn_kernel, lambda_init=lambda_init),
        name="df_attn",
        grid=(b, DF_HEADS // nh, s // tq),
        in_specs=[pl.BlockSpec((1, tq, nh * LANES), lambda bi, hi, qi: (bi, qi, hi)),
                  pl.BlockSpec((1, s, nh * LANES), lambda bi, hi, qi: (bi, 0, hi)),
                  pl.BlockSpec((1, nh * LANES, s), lambda bi, hi, qi: (bi, hi, 0))]
                 + [full(a) for a in consts],
        out_specs=pl.BlockSpec((1, tq, nh * LANES), lambda bi, hi, qi: (bi, qi, hi)),
        out_shape=jax.ShapeDtypeStruct((b, s, DF_WIDTH), BF16),
        compiler_params=_cparams(("parallel", "parallel", "parallel")),
    )(q, k, v, *consts)


def _merge_kernel(orw, omla, odf, g0, g1, g2, w0, w1, w2, o_ref):
    acc = g0[...].astype(F32) * _dot(orw[...], w0[...])
    acc = acc + g1[...].astype(F32) * _dot(omla[...], w1[...])
    acc = acc + g2[...].astype(F32) * _dot(odf[...], w2[...])
    o_ref[...] = acc.astype(o_ref.dtype)


def _merge(o_rw, o_mla, o_df, gates, p):
    t = o_rw.shape[0]
    tm = _pick_tile(t, (1024, 512, 256))
    tn = 1024
    nj = D_MODEL // tn
    act = lambda w: pl.BlockSpec((tm, w), lambda i, j: (i, 0))
    gate = lambda br: pl.BlockSpec((tm, tn), lambda i, j: (i, j + br * nj))
    wt = lambda w: pl.BlockSpec((w, tn), lambda i, j: (0, j))
    return pl.pallas_call(
        _merge_kernel,
        name="merge",
        grid=(t // tm, nj),
        in_specs=[act(RW_WIDTH), act(MLA_WIDTH), act(DF_WIDTH), gate(0), gate(1), gate(2),
                  wt(RW_WIDTH), wt(MLA_WIDTH), wt(DF_WIDTH)],
        out_specs=pl.BlockSpec((tm, tn), lambda i, j: (i, j)),
        out_shape=jax.ShapeDtypeStruct((t, D_MODEL), BF16),
        compiler_params=_cparams(("parallel", "parallel")),
    )(o_rw, o_mla, o_df, gates, gates, gates, p["w_up_rw"], p["w_up_mla"], p["w_up_df"])


def _layer_norm(y, g, b):
    mu = jnp.mean(y, -1, keepdims=True)
    yc = y - mu
    var = jnp.mean(yc * yc, -1, keepdims=True)
    return yc * lax.rsqrt(var + LN_EPS) * g + b


def _wo_ln_kernel(m_ref, w_ref, x_ref, g_ref, b_ref, o_ref, ob_ref):
    y = ALPHA * x_ref[...] + _dot(m_ref[...], w_ref[...])
    out = _layer_norm(y, g_ref[...], b_ref[...])
    o_ref[...] = out
    ob_ref[...] = out.astype(BF16).reshape(ob_ref.shape)


def _wo_ln(merged, x, p):
    t = x.shape[0]
    tm = 256
    tile = pl.BlockSpec((tm, D_MODEL), lambda i: (i, 0))
    tile3 = pl.BlockSpec((tm, ROW_SUB, LANES), lambda i: (i, 0, 0))
    full = lambda a: pl.BlockSpec(a.shape, lambda i: (0,) * a.ndim)
    return pl.pallas_call(
        _wo_ln_kernel,
        name="wo_ln",
        grid=(t // tm,),
        in_specs=[tile, full(p["w_o"]), tile, full(p["ln1_g"]), full(p["ln1_b"])],
        out_specs=[tile, tile3],
        out_shape=[jax.ShapeDtypeStruct((t, D_MODEL), F32),
                   jax.ShapeDtypeStruct((t, ROW_SUB, LANES), BF16)],
        compiler_params=_cparams(("parallel",)),
    )(merged, p["w_o"], x, p["ln1_g"], p["ln1_b"])


ROUTER_TILE = 1024


def _router_kernel(x_ref, wh_ref, wl_ref, bias_ref, tri_ref, idx_o, wts_o, rank_o, cnt_o, cnt_scr):
    @pl.when(pl.program_id(0) == 0)
    def _():
        cnt_scr[...] = jnp.zeros_like(cnt_scr)

    xh, xl = _split_bf16(x_ref[...])
    wh, wl = wh_ref[...], wl_ref[...]
    logits = _dot_nt(wh, xh) + (_dot_nt(wh, xl) + _dot_nt(wl, xh))
    scores = jax.nn.sigmoid(logits)
    sel = scores + bias_ref[...]
    tm = sel.shape[1]

    def row(a, i):
        return a[i:i + 1, :]

    best = jnp.zeros((1, tm), jnp.int32)
    best_s = None
    for g in range(N_GROUPS):
        a, b, c, d = (row(sel, EXPERTS_PER_GROUP * g + j) for j in range(4))
        hi1, lo1 = jnp.maximum(a, b), jnp.minimum(a, b)
        hi2, lo2 = jnp.maximum(c, d), jnp.minimum(c, d)
        gs = jnp.maximum(hi1, hi2) + jnp.maximum(jnp.minimum(hi1, hi2), jnp.maximum(lo1, lo2))
        if g == 0:
            best_s = gs
        else:
            upd = gs > best_s
            best = jnp.where(upd, g, best)
            best_s = jnp.where(upd, gs, best_s)

    def pick(a, j):
        out = row(a, j)
        for g in range(1, N_GROUPS):
            out = jnp.where(best == g, row(a, EXPERTS_PER_GROUP * g + j), out)
        return out

    cand = [pick(sel, j) for j in range(EXPERTS_PER_GROUP)]
    csc = [pick(scores, j) for j in range(EXPERTS_PER_GROUP)]
    neg = jnp.float32(-jnp.inf)

    def argmax4(vals):
        bi, bv = jnp.zeros((1, tm), jnp.int32), vals[0]
        for j in range(1, EXPERTS_PER_GROUP):
            upd = vals[j] > bv
            bi = jnp.where(upd, j, bi)
            bv = jnp.where(upd, vals[j], bv)
        return bi

    i1 = argmax4(cand)
    i2 = argmax4([jnp.where(i1 == j, neg, cand[j]) for j in range(EXPERTS_PER_GROUP)])

    def take(vals, i):
        out = vals[0]
        for j in range(1, EXPERTS_PER_GROUP):
            out = jnp.where(i == j, vals[j], out)
        return out

    w1, w2 = take(csc, i1), take(csc, i2)
    tot = w1 + w2
    e1 = best * EXPERTS_PER_GROUP + i1
    e2 = best * EXPERTS_PER_GROUP + i2
    eid = lax.broadcasted_iota(jnp.int32, (N_EXPERTS, tm), 0)
    oh1 = eid == e1
    oh2 = eid == e2
    oh = (oh1 | oh2).astype(BF16)
    before = _dot(oh, tri_ref[...]) + cnt_scr[...][:, 0:1]
    r1 = jnp.sum(jnp.where(oh1, before, 0.0), 0, keepdims=True)
    r2 = jnp.sum(jnp.where(oh2, before, 0.0), 0, keepdims=True)
    idx_o[0:1, :] = e1
    idx_o[1:2, :] = e2
    wts_o[0:1, :] = w1 / tot
    wts_o[1:2, :] = w2 / tot
    rank_o[0:1, :] = r1.astype(jnp.int32)
    rank_o[1:2, :] = r2.astype(jnp.int32)
    new_cnt = cnt_scr[...] + jnp.sum(oh.astype(F32), 1, keepdims=True)
    cnt_scr[...] = new_cnt
    cnt_o[...] = new_cnt


def _router(x, p):
    t = x.shape[0]
    tm = ROUTER_TILE
    full = lambda a: pl.BlockSpec(a.shape, lambda i: (0,) * a.ndim)
    tok = pl.BlockSpec((TOP_K, tm), lambda i: (0, i))
    consts = [p["router_wt_hi"], p["router_wt_lo"], p["router_bias"], p["router_tri"]]
    return pl.pallas_call(
        _router_kernel,
        name="router",
        grid=(t // tm,),
        in_specs=[pl.BlockSpec((tm, D_MODEL), lambda i: (i, 0))] + [full(a) for a in consts],
        out_specs=[tok, tok, tok, pl.BlockSpec((N_EXPERTS, LANES), lambda i: (0, 0))],
        out_shape=[jax.ShapeDtypeStruct((TOP_K, t), jnp.int32), jax.ShapeDtypeStruct((TOP_K, t), F32),
                   jax.ShapeDtypeStruct((TOP_K, t), jnp.int32),
                   jax.ShapeDtypeStruct((N_EXPERTS, LANES), F32)],
        scratch_shapes=[pltpu.VMEM((N_EXPERTS, LANES), F32)],
        compiler_params=_cparams(("arbitrary",)),
    )(x, *consts)


DISPATCH_TILE = 512


def _dispatch_kernel(dest_ref, x_ref, init_ref, xs_ref, sem):
    del init_ref
    def copy(r, k):
        return pltpu.make_async_copy(x_ref.at[r], xs_ref.at[dest_ref[0, TOP_K * r + k]], sem)

    def start(r, carry):
        copy(r, 0).start()
        copy(r, 1).start()
        return carry

    def wait(r, carry):
        copy(r, 0).wait()
        copy(r, 1).wait()
        return carry

    lax.fori_loop(0, DISPATCH_TILE, start, 0, unroll=DMA_UNROLL)
    lax.fori_loop(0, DISPATCH_TILE, wait, 0, unroll=DMA_UNROLL)


def _dispatch(xb, dest, n_rows):
    t = xb.shape[0]
    nt = t // DISPATCH_TILE
    dest2 = dest.reshape(nt, 1, DISPATCH_TILE * TOP_K)
    init = jnp.zeros((n_rows, ROW_SUB, LANES), BF16)
    return pl.pallas_call(
        _dispatch_kernel,
        name="dispatch",
        grid=(nt,),
        in_specs=[pl.BlockSpec((None, 1, DISPATCH_TILE * TOP_K), lambda i: (i, 0, 0),
                               memory_space=pltpu.SMEM),
                  pl.BlockSpec((DISPATCH_TILE, ROW_SUB, LANES), lambda i: (i, 0, 0)),
                  pl.BlockSpec(memory_space=pl.ANY)],
        out_specs=pl.BlockSpec(memory_space=pl.ANY),
        out_shape=jax.ShapeDtypeStruct((n_rows, ROW_SUB, LANES), BF16),
        scratch_shapes=[pltpu.SemaphoreType.DMA(())],
        input_output_aliases={2: 0},
        compiler_params=_cparams(("arbitrary",)),
    )(dest2, xb, init)


def _ffn_kernel(te_ref, nu_ref, xs_ref, wg_ref, wu_ref, wd_ref, y_ref):
    del te_ref

    @pl.when(pl.program_id(0) < nu_ref[0])
    def _():
        xs = xs_ref[...].reshape(FFN_TILE, D_MODEL)
        h = jax.nn.silu(_dot(xs, wg_ref[0])) * _dot(xs, wu_ref[0])
        y_ref[...] = _dot(h.astype(BF16), wd_ref[0]).astype(BF16).reshape(y_ref.shape)

    @pl.when(pl.program_id(0) >= nu_ref[0])
    def _():
        y_ref[...] = jnp.zeros_like(y_ref)


def _ffn(xs, tile_expert, n_used, p):
    n_rows = xs.shape[0]
    tm = FFN_TILE
    grid_spec = pltpu.PrefetchScalarGridSpec(
        num_scalar_prefetch=2,
        grid=(n_rows // tm,),
        in_specs=[pl.BlockSpec((tm, ROW_SUB, LANES), lambda i, te, nu: (i, 0, 0)),
                  pl.BlockSpec((1, D_MODEL, D_EXPERT), lambda i, te, nu: (te[i], 0, 0)),
                  pl.BlockSpec((1, D_MODEL, D_EXPERT), lambda i, te, nu: (te[i], 0, 0)),
                  pl.BlockSpec((1, D_EXPERT, D_MODEL), lambda i, te, nu: (te[i], 0, 0))],
        out_specs=pl.BlockSpec((tm, ROW_SUB, LANES), lambda i, te, nu: (i, 0, 0)),
    )
    return pl.pallas_call(
        _ffn_kernel,
        name="ffn",
        grid_spec=grid_spec,
        out_shape=jax.ShapeDtypeStruct((n_rows, ROW_SUB, LANES), BF16),
        compiler_params=_cparams(("arbitrary",)),
    )(tile_expert, n_used, xs, p["ex_w_gate"], p["ex_w_up"], p["ex_w_down"])


COMBINE_TILE = 256


def _combine_ln_kernel(dest_ref, dest_next_ref, y_ref, x_ref, w_ref, g_ref, b_ref, o_ref, ob_ref,
                       buf, sem):
    i = pl.program_id(0)
    slot = i % 2

    def copy(idx_ref, s, r, k):
        return pltpu.make_async_copy(y_ref.at[idx_ref[0, TOP_K * r + k]], buf.at[s, k, r],
                                     sem.at[s])

    def issue(idx_ref, s):
        def body(r, carry):
            copy(idx_ref, s, r, 0).start()
            copy(idx_ref, s, r, 1).start()
            return carry
        lax.fori_loop(0, COMBINE_TILE, body, 0, unroll=DMA_UNROLL)

    @pl.when(i == 0)
    def _():
        issue(dest_ref, 0)

    @pl.when(i + 1 < pl.num_programs(0))
    def _():
        issue(dest_next_ref, 1 - slot)

    def wait(r, carry):
        copy(dest_ref, slot, r, 0).wait()
        copy(dest_ref, slot, r, 1).wait()
        return carry

    lax.fori_loop(0, COMBINE_TILE, wait, 0, unroll=DMA_UNROLL)
    w = w_ref[...]
    shape = (COMBINE_TILE, D_MODEL)
    ffn = (w[:, 0:1] * buf[slot, 0].reshape(shape).astype(F32)
           + w[:, 1:2] * buf[slot, 1].reshape(shape).astype(F32))
    out = _layer_norm(ALPHA * x_ref[...] + ffn, g_ref[...], b_ref[...])
    o_ref[...] = out
    ob_ref[...] = out.astype(BF16)


def _combine_ln(y, dest, x, wts, p):
    t = x.shape[0]
    tm = COMBINE_TILE
    nt = t // tm
    dest2 = dest.reshape(nt, 1, tm * TOP_K)
    tile = pl.BlockSpec((tm, D_MODEL), lambda i: (i, 0))
    full = lambda a: pl.BlockSpec(a.shape, lambda i: (0,) * a.ndim)
    return pl.pallas_call(
        _combine_ln_kernel,
        name="combine_ln",
        grid=(nt,),
        in_specs=[pl.BlockSpec((None, 1, tm * TOP_K), lambda i: (i, 0, 0), memory_space=pltpu.SMEM),
                  pl.BlockSpec((None, 1, tm * TOP_K), lambda i: (jnp.minimum(i + 1, nt - 1), 0, 0),
                               memory_space=pltpu.SMEM),
                  pl.BlockSpec(memory_space=pl.ANY), tile,
                  pl.BlockSpec((tm, TOP_K), lambda i: (i, 0)), full(p["ln2_g"]), full(p["ln2_b"])],
        out_specs=[tile, tile],
        out_shape=[jax.ShapeDtypeStruct((t, D_MODEL), F32), jax.ShapeDtypeStruct((t, D_MODEL), BF16)],
        scratch_shapes=[pltpu.VMEM((2, TOP_K, tm, ROW_SUB, LANES), BF16),
                        pltpu.SemaphoreType.DMA((2,))],
        compiler_params=_cparams(("arbitrary",)),
    )(dest2, dest2, y, x, wts, p["ln2_g"], p["ln2_b"])


def _moe_ln(x, xb, p):
    t = x.shape[0]
    idx, wts, rank, counts = _router(x, p)
    counts = counts[:, 0].astype(jnp.int32)
    tiles = (counts + FFN_TILE - 1) // FFN_TILE
    tile_end = jnp.cumsum(tiles)
    row_start = (tile_end - tiles) * FFN_TILE
    n_tiles = (t * TOP_K) // FFN_TILE + N_EXPERTS
    start_of = jnp.sum(jnp.where(idx[..., None] == jnp.arange(N_EXPERTS), row_start, 0), -1)
    dest = (start_of + rank).T.reshape(-1)
    tile_expert = jnp.minimum(jnp.sum(jnp.arange(n_tiles)[:, None] >= tile_end[None, :], axis=1),
                              N_EXPERTS - 1).astype(jnp.int32)
    n_used = tile_end[-1:].astype(jnp.int32)
    xs = _dispatch(xb, dest, n_tiles * FFN_TILE)
    y = _ffn(xs, tile_expert, n_used, p)
    return _combine_ln(y, dest, x, wts.T, p)


def _block_diag_ones(n, blk):
    i = jnp.arange(n) // blk
    return (i[:, None] == i[None, :]).astype(BF16)


def _prep_layer(l, w):
    row = lambda a: a.reshape(1, -1).astype(F32)
    w_in = w["w_in"][l]
    o1, o2, o3 = RW_IN, RW_IN + MLA_IN, RW_IN + MLA_IN + DF_IN
    p = {
        "w_in_rw": w_in[:, :o1].astype(BF16),
        "w_in_mla": jnp.pad(w_in[:, o1:o2], ((0, 0), (0, MLA_IN_PAD - MLA_IN))).astype(BF16),
        "w_in_df": w_in[:, o2:o3].astype(BF16),
        "w_in_gate": w_in[:, o3:].astype(BF16),
        "mu_prev": row(w["shift_prev"][l]), "mu_next": row(w["shift_next"][l]),
        "w0": row(w["rw_w0"][l]), "a0": row(w["rw_a0"][l]),
        "g2": w["rw_g2"][l].astype(BF16),
        "k_k": row(w["rw_k_k"][l]), "k_a": row(w["rw_k_a"][l]), "r_k": row(w["rw_r_k"][l]),
        "lnx_g": row(w["rw_lnx_g"][l]), "lnx_b": row(w["rw_lnx_b"][l]),
        "bd": _block_diag_ones(RW_WIDTH, RW_HEAD_DIM),
        "q_norm": row(w["mla_q_norm"][l]), "kv_norm": row(w["mla_kv_norm"][l]),
        "lq1": row(w["df_lq1"][l]), "lk1": row(w["df_lk1"][l]),
        "lq2": row(w["df_lq2"][l]), "lk2": row(w["df_lk2"][l]),
        "subln": row(w["df_subln"][l]),
        "w_up_rw": w["w_up_rw"][l].astype(BF16), "w_up_mla": w["w_up_mla"][l].astype(BF16),
        "w_up_df": w["w_up_df"][l].astype(BF16), "w_o": w["w_o"][l].astype(BF16),
        "ln1_g": row(w["ln1_g"][l]), "ln1_b": row(w["ln1_b"][l]),
        "ln2_g": row(w["ln2_g"][l]), "ln2_b": row(w["ln2_b"][l]),
        "ex_w_gate": w["ex_w_gate"][l].astype(BF16), "ex_w_up": w["ex_w_up"][l].astype(BF16),
        "ex_w_down": w["ex_w_down"][l].astype(BF16),
    }
    zc = jnp.zeros((DECAY_LORA, RW_WIDTH), F32)
    w2 = w["rw_w2"][l]
    a2 = w["rw_a2"][l]
    p["w2cat"] = jnp.block([[w2[0], zc], [zc, w2[1]]]).astype(BF16)
    p["a2cat"] = jnp.block([[a2[0], zc], [zc, a2[1]]]).astype(BF16)
    wq = w["mla_w_uq"][l].reshape(MLA_Q_LORA, MLA_HEADS, MLA_NOPE + MLA_ROPE)
    p["wq"] = jnp.pad(wq, ((0, 0), (0, 0), (0, MLA_HEAD_PAD - MLA_NOPE - MLA_ROPE))
                      ).reshape(MLA_Q_LORA, -1).astype(BF16)
    wkv = w["mla_w_ukv"][l].reshape(MLA_KV_LORA, MLA_HEADS, MLA_NOPE + MLA_V)
    wk_nope = jnp.pad(wkv[:, :, :MLA_NOPE], ((0, 0), (0, 0), (0, MLA_HEAD_PAD - MLA_NOPE)))
    place = jnp.zeros((LANES, MLA_HEADS, MLA_HEAD_PAD), F32)
    j = jnp.arange(MLA_ROPE)
    place = place.at[j, :, MLA_NOPE + j].set(1.0)
    p["wk"] = jnp.concatenate([wk_nope, place], axis=0).reshape(MLA_KV_LORA + LANES, -1).astype(BF16)
    p["wv"] = wkv[:, :, MLA_NOPE:].reshape(MLA_KV_LORA, -1).T.astype(BF16)
    return p


def _trunk(x3, layers, shared):
    b, s, d = x3.shape
    t = b * s
    tabs = {
        "mla_q": _rope_tables(s, MLA_ROPE, MLA_NOPE, LANES),
        "mla_k": _rope_tables(s, MLA_ROPE, 0, LANES),
        "df": _rope_tables(s, DF_HEAD_DIM, 0, DF_HEAD_DIM),
    }
    x = x3.reshape(t, d)
    xb = x.astype(BF16)
    for l, p in enumerate(layers):
        p = dict(p, **shared)
        lambda_init = 0.8 - 0.6 * math.exp(-0.3 * l)
        z_rw = _matmul(xb, p["w_in_rw"], F32).reshape(b, s, -1)
        z_mla = _matmul(xb, p["w_in_mla"], F32).reshape(b, s, -1)
        z_df = _matmul(xb, p["w_in_df"], F32).reshape(b, s, -1)
        gates = _matmul(xb, p["w_in_gate"], BF16, act="sigmoid")
        r, v, kk, g, bonus, logd, a, kdir = _rw_prep(z_rw, p)
        yf, yb = _rw_scan(r, v, kk, logd, a, kdir)
        o_rw = _rw_post(yf.reshape(t, -1), yb.reshape(t, -1), bonus.reshape(t, -1),
                        g.reshape(t, -1), p)
        q, k, v2 = _mla_prep(z_mla, p, tabs)
        o_mla = _mla_attn(q, k, v2).reshape(t, -1)
        q, k, v2 = _df_prep(z_df, tabs)
        o_df = _df_attn(q, k, v2, p, lambda_init).reshape(t, -1)
        merged = _merge(o_rw, o_mla, o_df, gates, p)
        x, xb = _wo_ln(merged, x, p)
        x, xb = _moe_ln(x, xb, p)
    return x.reshape(b, s, d)


def kernel(x_prompt, x_sample, w_in, shift_prev, shift_next, rw_w0, rw_w2, rw_a0, rw_a2, rw_g2,
           rw_k_k, rw_k_a, rw_r_k, rw_lnx_g, rw_lnx_b, mla_q_norm, mla_kv_norm, mla_w_uq,
           mla_w_ukv, df_lq1, df_lk1, df_lq2, df_lk2, df_subln, w_up_rw, w_up_mla, w_up_df, w_o,
           ln1_g, ln1_b, ln2_g, ln2_b, router_w, router_bias, ex_w_gate, ex_w_up, ex_w_down):
    w = dict(w_in=w_in, shift_prev=shift_prev, shift_next=shift_next, rw_w0=rw_w0, rw_w2=rw_w2,
             rw_a0=rw_a0, rw_a2=rw_a2, rw_g2=rw_g2, rw_k_k=rw_k_k, rw_k_a=rw_k_a, rw_r_k=rw_r_k,
             rw_lnx_g=rw_lnx_g, rw_lnx_b=rw_lnx_b, mla_q_norm=mla_q_norm,
             mla_kv_norm=mla_kv_norm, mla_w_uq=mla_w_uq, mla_w_ukv=mla_w_ukv, df_lq1=df_lq1,
             df_lk1=df_lk1, df_lq2=df_lq2, df_lk2=df_lk2, df_subln=df_subln, w_up_rw=w_up_rw,
             w_up_mla=w_up_mla, w_up_df=w_up_df, w_o=w_o, ln1_g=ln1_g, ln1_b=ln1_b, ln2_g=ln2_g,
             ln2_b=ln2_b, ex_w_gate=ex_w_gate, ex_w_up=ex_w_up, ex_w_down=ex_w_down)
    layers = [_prep_layer(l, w) for l in range(DEPTH)]
    rwt = router_w.T.astype(F32)
    rwt_hi = rwt.astype(BF16)
    ti = jnp.arange(ROUTER_TILE)
    shared = {
        "router_wt_hi": rwt_hi,
        "router_wt_lo": (rwt - rwt_hi.astype(F32)).astype(BF16),
        "router_bias": router_bias.reshape(N_EXPERTS, 1).astype(F32),
        "router_tri": (ti[:, None] < ti[None, :]).astype(BF16),
    }
    return (_trunk(x_prompt, layers, shared), _trunk(x_sample, layers, shared))
```

```python
import functools
import math

import jax
import jax.numpy as jnp
import numpy as np
from jax import lax
from jax.experimental import pallas as pl
from jax.experimental.pallas import tpu as pltpu

F32 = jnp.float32
BF16 = jnp.bfloat16

D_MODEL = 2048
DEPTH = 2
RW_HEADS, RW_HEAD_DIM = 12, 64
RW_WIDTH = RW_HEADS * RW_HEAD_DIM
DECAY_LORA, ICLR_LORA, GATE_LORA, N_DIR = 64, 64, 128, 2
GN_EPS = 64e-5
MLA_HEADS, MLA_NOPE, MLA_ROPE, MLA_V = 8, 64, 32, 64
MLA_Q_LORA, MLA_KV_LORA = 512, 256
MLA_WIDTH = MLA_HEADS * MLA_V
DF_HEADS, DF_HEAD_DIM = 6, 64
DF_WIDTH = DF_HEADS * 2 * DF_HEAD_DIM
DF_EPS = 1e-5
N_BRANCH = 3
N_EXPERTS, N_GROUPS, TOP_K, D_EXPERT = 16, 4, 2, 1024
EXPERTS_PER_GROUP = N_EXPERTS // N_GROUPS
ROPE_THETA = 10000.0
LN_EPS = 1e-5
RMS_EPS = 1e-6
ALPHA = (2 * DEPTH) ** 0.25
RW_IN = 3 * RW_WIDTH + N_DIR * DECAY_LORA + N_DIR * ICLR_LORA + GATE_LORA
MLA_IN = MLA_Q_LORA + MLA_KV_LORA + MLA_ROPE
DF_IN = 3 * DF_WIDTH
GATE_IN = N_BRANCH * D_MODEL

LANES = 128
MLA_IN_PAD = 896
MLA_HEAD_PAD = 128
MLA_LANE_OF_DIM = np.concatenate([np.arange(16, 64), np.arange(80, 96),
                                  np.arange(0, 16), np.arange(64, 80)])
RW_CHUNK = 64
RW_PAIRS = RW_WIDTH // LANES
RW_BATCH = 4
VMEM_LIMIT = 58 * 1024 * 1024
FFN_TILE = 512
LOG2E = math.log2(math.e)
MLA_HEADS_PER_STEP = 8
DF_HEADS_PER_STEP = 3
DMA_UNROLL = 8
HALO = 8
ROW_SUB = D_MODEL // LANES


def _cparams(sem):
    return pltpu.CompilerParams(dimension_semantics=sem, vmem_limit_bytes=VMEM_LIMIT)


def _dot(a, b):
    return jnp.dot(a, b, preferred_element_type=F32)


def _dot_nt(a, b):
    return lax.dot_general(a, b, (((1,), (1,)), ((), ())), preferred_element_type=F32)


def _dot_tn(a, b):
    return lax.dot_general(a, b, (((0,), (0,)), ((), ())), preferred_element_type=F32)


def _sigmoid(x):
    return 0.5 * jnp.tanh(0.5 * x) + 0.5


def _split_bf16(x):
    hi = x.astype(BF16)
    lo = (x - hi.astype(F32)).astype(BF16)
    return hi, lo


def _pick_tile(n, candidates):
    for c in candidates:
        if n % c == 0:
            return c
    raise ValueError(f"no tile for {n}")


def _mm_kernel(x_ref, w_ref, o_ref, *, act):
    acc = _dot(x_ref[...], w_ref[...])
    if act == "sigmoid":
        acc = _sigmoid(acc)
    o_ref[...] = acc.astype(o_ref.dtype)


def _matmul(x, w, out_dtype, act=None):
    m, k = x.shape
    n = w.shape[1]
    tm = _pick_tile(m, (1024, 512, 256, 128))
    tn = _pick_tile(n, (1024, 896, 768, 512, 256, 128))
    return pl.pallas_call(
        functools.partial(_mm_kernel, act=act),
        name="mm",
        grid=(n // tn, m // tm),
        in_specs=[pl.BlockSpec((tm, k), lambda j, i: (i, 0)),
                  pl.BlockSpec((k, tn), lambda j, i: (0, j))],
        out_specs=pl.BlockSpec((tm, tn), lambda j, i: (i, j)),
        out_shape=jax.ShapeDtypeStruct((m, n), out_dtype),
        compiler_params=_cparams(("parallel", "parallel")),
    )(x, w)


def _rw_prep_kernel(z_ref, hp_ref, hn_ref, mup_ref, mun_ref, w2_ref, a2_ref, g2_ref, w0_ref,
                    a0_ref, kk_k_ref, k_a_ref, r_k_ref, bd_ref,
                    r_o, v_o, kk_o, g_o, bonus_o, logd_o, a_o, kdir_o):
    z = z_ref[0]
    tr = z.shape[0]
    row = lax.broadcasted_iota(jnp.int32, (tr, 1), 0)
    ti = pl.program_id(1)
    halo_prev = jnp.where(ti == 0, 0.0, hp_ref[0, HALO - 1:HALO, :])
    halo_next = jnp.where(ti == pl.num_programs(1) - 1, 0.0, hn_ref[0, 0:1, :])
    prev = jnp.where(row == 0, halo_prev, pltpu.roll(z, 1, 0))
    nxt = jnp.where(row == tr - 1, halo_next, pltpu.roll(z, tr - 1, 0))
    zs = z + mup_ref[...] * (prev - z) + mun_ref[...] * (nxt - z)
    c = RW_WIDTH
    r, k, v = zs[:, 0:c], zs[:, c:2 * c], zs[:, 2 * c:3 * c]
    wl = zs[:, 3 * c:3 * c + 128]
    al = zs[:, 3 * c + 128:3 * c + 256]
    gl = zs[:, 3 * c + 256:3 * c + 384]
    w_raw = w0_ref[...] + _dot(jnp.tanh(wl).astype(BF16), w2_ref[...])
    a = _sigmoid(a0_ref[...] + _dot(al.astype(BF16), a2_ref[...]))
    g = _dot(_sigmoid(gl).astype(BF16), g2_ref[...])
    logd = (-math.exp(-0.5)) * _sigmoid(w_raw)
    bd = bd_ref[...]

    def head_sum(x):
        hi, lo = _split_bf16(x)
        return _dot(hi, bd) + _dot(lo, bd)

    kk = k * kk_k_ref[...]
    kk = kk * lax.rsqrt(head_sum(kk * kk) + 1e-12)
    k_a = k_a_ref[...]
    kd0 = k * (1.0 + (a[:, 0:c] - 1.0) * k_a)
    kd1 = k * (1.0 + (a[:, c:2 * c] - 1.0) * k_a)
    bonus = head_sum(r * r_k_ref[...] * (kd0 + kd1)) * v
    r_o[0] = r
    v_o[0] = v
    kk_o[0] = kk
    g_o[0] = g
    bonus_o[0] = bonus
    logd_o[0] = logd
    a_o[0] = a
    kdir_o[0, :, 0:c] = kd0
    kdir_o[0, :, c:2 * c] = kd1


def _rw_prep(z_rw, p):
    b, s, _ = z_rw.shape
    tr = 256
    nt = s // tr
    hpt = tr // HALO
    c = RW_WIDTH
    tile = lambda w: pl.BlockSpec((1, tr, w), lambda bi, ti: (bi, ti, 0))
    halo_prev = pl.BlockSpec((1, HALO, RW_IN), lambda bi, ti: (bi, jnp.maximum(ti * hpt - 1, 0), 0))
    halo_next = pl.BlockSpec((1, HALO, RW_IN),
                             lambda bi, ti: (bi, jnp.minimum((ti + 1) * hpt, s // HALO - 1), 0))
    full = lambda a: pl.BlockSpec(a.shape, lambda bi, ti: (0,) * a.ndim)
    consts = [p["mu_prev"], p["mu_next"], p["w2cat"], p["a2cat"], p["g2"], p["w0"], p["a0"],
              p["k_k"], p["k_a"], p["r_k"], p["bd"]]
    out_w = [c, c, c, c, c, 2 * c, 2 * c, 2 * c]
    return pl.pallas_call(
        _rw_prep_kernel,
        name="rw_prep",
        grid=(b, nt),
        in_specs=[tile(RW_IN), halo_prev, halo_next] + [full(a) for a in consts],
        out_specs=[tile(w) for w in out_w],
        out_shape=[jax.ShapeDtypeStruct((b, s, w), F32) for w in out_w],
        compiler_params=_cparams(("parallel", "parallel")),
    )(z_rw, z_rw, z_rw, *consts)


def _rw_scan_kernel(rf, vf, kkf, ldf, af, kdf, rb, vb, kkb, ldb, ab, kdb, yf_o, yb_o, z_scr):
    @pl.when(pl.program_id(1) == 0)
    def _():
        z_scr[...] = jnp.zeros_like(z_scr)

    c = RW_CHUNK
    ti = lax.broadcasted_iota(jnp.int32, (c, c), 0)
    tj = lax.broadcasted_iota(jnp.int32, (c, c), 1)
    lane = lax.broadcasted_iota(jnp.int32, (1, LANES), 1)
    lane_m = (lane < RW_HEAD_DIM, lane >= RW_HEAD_DIM)
    bi = lax.broadcasted_iota(jnp.int32, (LANES, LANES), 0) // RW_HEAD_DIM
    bj = lax.broadcasted_iota(jnp.int32, (LANES, LANES), 1) // RW_HEAD_DIM
    bdmask = bi == bj
    incl = ((ti >= tj), (ti <= tj))
    strict = ((ti > tj), (ti < tj))
    incl_bf = tuple(m.astype(BF16) for m in incl)
    t2 = lax.broadcasted_iota(jnp.int32, (2 * c, 2 * c), 0)
    j2 = lax.broadcasted_iota(jnp.int32, (2 * c, 2 * c), 1) & (c - 1)
    diag_ok = (t2 >= c) & ((t2 & (c - 1)) == j2)
    t2 = t2 & (c - 1)
    mask_ar = ((t2 > j2) | diag_ok, (t2 < j2) | diag_ok)
    in_refs = ((rf, vf, kkf, ldf, af, kdf), (rb, vb, kkb, ldb, ab, kdb))
    out_refs = (yf_o, yb_o)

    def bd_rows(x):
        return jnp.concatenate([jnp.where(lane_m[0], x, 0.0), jnp.where(lane_m[1], x, 0.0)],
                               axis=0).astype(BF16)

    probs = [(d, p, bb) for d in range(N_DIR) for p in range(RW_PAIRS) for bb in range(RW_BATCH)]
    sl = lambda p: slice(p * LANES, (p + 1) * LANES)
    val = {s: [ref[s[2], :, sl(s[1])] for ref in in_refs[s[0]]] for s in probs}
    w = RW_WIDTH
    logp_all = {}
    for d in range(N_DIR):
        for bb in range(RW_BATCH):
            hi, lo = _split_bf16(in_refs[d][3][bb])
            both = _dot(incl_bf[d], jnp.concatenate([hi, lo], axis=1))
            logp_all[d, bb] = both[:, 0:w] + both[:, w:2 * w]
    logp = {s: logp_all[s[0], s[2]][:, sl(s[1])] for s in probs}
    ones = jnp.ones((2 * c, LANES), BF16)
    logpc = {s: _dot_tn(jnp.concatenate(_split_bf16(val[s][3]), axis=0), ones) for s in probs}
    at, rt, bt, kt, vv, z, zb = {}, {}, {}, {}, {}, {}, {}
    for s in probs:
        r, v, kk, ld, a, kd = val[s]
        pinv = jnp.exp(-logp[s])
        at[s] = -(kk * jnp.exp(logp[s] - ld))
        rt[s] = r * jnp.exp(logp[s])
        bt[s] = kk * a * pinv
        kt[s] = kd * pinv
        vv[s] = v
        z[s] = z_scr[s]
        zb[s] = z[s].astype(BF16)
    lhs = {s: jnp.concatenate([at[s], rt[s]], axis=0).astype(BF16) for s in probs}
    g = {s: _dot_nt(lhs[s], jnp.concatenate([bd_rows(bt[s]), bd_rows(kt[s])], axis=0))
         for s in probs}
    a_bb = {s: jnp.where(mask_ar[s[0]], g[s][:, 0:LANES], 0.0).astype(BF16) for s in probs}
    a_kk = {s: jnp.where(mask_ar[s[0]], g[s][:, LANES:], 0.0).astype(BF16) for s in probs}
    zv = {s: _dot(jnp.concatenate([lhs[s], a_kk[s]], axis=1),
                  jnp.concatenate([zb[s], bd_rows(vv[s])], axis=0)) for s in probs}
    x = {s: zv[s][0:c] for s in probs}
    ap = {s: a_bb[s][0:c] for s in probs}
    for i in range(6):
        if i < 5:
            prod = {s: _dot(ap[s], jnp.concatenate([bd_rows(x[s]), bd_rows(ap[s])], axis=1))
                    for s in probs}
            x = {s: x[s] + prod[s][:, 0:LANES] for s in probs}
            ap = {s: prod[s][:, LANES:].astype(BF16) for s in probs}
        else:
            x = {s: x[s] + _dot(ap[s], bd_rows(x[s])) for s in probs}
    for s in probs:
        d, p, bb = s
        out_refs[d][bb, :, sl(p)] = zv[s][c:] + _dot(a_bb[s][c:], bd_rows(x[s]))
        upd = _dot_tn(jnp.concatenate([bt[s], kt[s]], axis=0).astype(BF16),
                      jnp.concatenate([x[s], vv[s]], axis=0).astype(BF16))
        z_scr[s] = jnp.where(bdmask, jnp.exp(logpc[s]) * (z[s] + upd), 0.0)


def _rw_scan(r, v, kk, logd, a, kdir):
    b, s, c = r.shape
    nc = s // RW_CHUNK
    nb = RW_BATCH
    fwd = pl.BlockSpec((nb, RW_CHUNK, c), lambda bi, ci: (bi, ci, 0))
    bwd = pl.BlockSpec((nb, RW_CHUNK, c), lambda bi, ci: (bi, nc - 1 - ci, 0))
    bwd_dir = pl.BlockSpec((nb, RW_CHUNK, c), lambda bi, ci: (bi, nc - 1 - ci, 1))
    return pl.pallas_call(
        _rw_scan_kernel,
        name="rw_scan",
        grid=(b // nb, nc),
        in_specs=[fwd, fwd, fwd, fwd, fwd, fwd, bwd, bwd, bwd, bwd_dir, bwd_dir, bwd_dir],
        out_specs=[fwd, bwd],
        out_shape=[jax.ShapeDtypeStruct((b, s, c), F32)] * 2,
        scratch_shapes=[pltpu.VMEM((N_DIR, RW_PAIRS, nb, LANES, LANES), F32)],
        compiler_params=_cparams(("parallel", "arbitrary")),
    )(r, v, kk, logd, a, kdir, r, v, kk, logd, a, kdir)


def _rw_post_kernel(yf_ref, yb_ref, bonus_ref, g_ref, lg_ref, lb_ref, bd_ref, o_ref):
    y = yf_ref[...] + yb_ref[...]
    bd = bd_ref[...]

    def head_mean(x):
        hi, lo = _split_bf16(x)
        return (_dot(hi, bd) + _dot(lo, bd)) * (1.0 / RW_HEAD_DIM)

    mu = head_mean(y)
    yc = y - mu
    var = head_mean(yc * yc)
    yn = yc * lax.rsqrt(var + GN_EPS) * lg_ref[...] + lb_ref[...]
    o_ref[...] = ((yn + bonus_ref[...]) * g_ref[...]).astype(o_ref.dtype)


def _rw_post(yf, yb, bonus, g, p):
    t, c = yf.shape
    tm = _pick_tile(t, (1024, 512, 256))
    tile = pl.BlockSpec((tm, c), lambda i: (i, 0))
    full = lambda a: pl.BlockSpec(a.shape, lambda i: (0,) * a.ndim)
    consts = [p["lnx_g"], p["lnx_b"], p["bd"]]
    return pl.pallas_call(
        _rw_post_kernel,
        name="rw_post",
        grid=(t // tm,),
        in_specs=[tile] * 4 + [full(a) for a in consts],
        out_specs=tile,
        out_shape=jax.ShapeDtypeStruct((t, c), BF16),
        compiler_params=_cparams(("parallel",)),
    )(yf, yb, bonus, g, *consts)


def _rope_tables_mla(s):
    half = MLA_ROPE // 2
    inv_freq = jnp.power(ROPE_THETA, -jnp.arange(half, dtype=F32) * (2.0 / MLA_ROPE))
    ang = jnp.arange(s, dtype=F32)[:, None] * inv_freq[None, :]
    lo, hi = slice(0, half), slice(LANES // 2, LANES // 2 + half)
    cos = jnp.ones((s, LANES), F32).at[:, lo].set(jnp.cos(ang)).at[:, hi].set(jnp.cos(ang))
    sg = jnp.zeros((s, LANES), F32).at[:, lo].set(-jnp.sin(ang)).at[:, hi].set(jnp.sin(ang))
    return cos, sg


def _mla_in_columns(w_mla):
    half = MLA_ROPE // 2
    base = MLA_Q_LORA + MLA_KV_LORA
    block = (jnp.zeros((w_mla.shape[0], LANES), w_mla.dtype)
             .at[:, 0:half].set(w_mla[:, base:base + half])
             .at[:, LANES // 2:LANES // 2 + half].set(w_mla[:, base + half:base + 2 * half]))
    return jnp.concatenate([w_mla[:, :base], block], axis=1)


def _rope_tables_df(s):
    half = DF_HEAD_DIM // 2
    inv_freq = jnp.power(ROPE_THETA, -jnp.arange(half, dtype=F32) * (2.0 / DF_HEAD_DIM))
    ang = jnp.arange(s, dtype=F32)[:, None] * inv_freq[None, :]
    lane = jnp.arange(LANES)
    j = lane % half
    sign = jnp.where(lane < LANES // 2, -1.0, 1.0)
    return jnp.cos(ang)[:, j].astype(F32), (jnp.sin(ang)[:, j] * sign).astype(F32)


def _df_columns(w_df):
    half = DF_HEAD_DIM // 2
    lane = jnp.arange(LANES)
    old = jnp.where((lane >= half) & (lane < 2 * half), lane + half,
                    jnp.where((lane >= 2 * half) & (lane < 3 * half), lane - half, lane))
    cols = (jnp.arange(2 * DF_HEADS)[:, None] * LANES + old[None, :]).reshape(-1)
    return jnp.concatenate([w_df[:, cols], w_df[:, 2 * DF_WIDTH:]], axis=1)


def _mla_prep_kernel(z_ref, qg_ref, kvg_ref, wq_ref, wk_ref, wv_ref,
                     cos_ref, sg_ref, q_o, k_o, v_o):
    z = z_ref[0]
    c_q = z[:, 0:MLA_Q_LORA]
    c_kv = z[:, MLA_Q_LORA:MLA_Q_LORA + MLA_KV_LORA]
    kr = z[:, MLA_Q_LORA + MLA_KV_LORA:MLA_IN_PAD]
    c_q = c_q * lax.rsqrt(jnp.mean(c_q * c_q, -1, keepdims=True) + RMS_EPS) * qg_ref[...]
    c_kv = c_kv * lax.rsqrt(jnp.mean(c_kv * c_kv, -1, keepdims=True) + RMS_EPS) * kvg_ref[...]
    q = _dot(c_q.astype(BF16), wq_ref[...])
    scale = (MLA_NOPE + MLA_ROPE) ** -0.5 * LOG2E
    cos, sg = cos_ref[...], sg_ref[...]
    rope = lambda x: x * cos + pltpu.roll(x, LANES // 2, 1) * sg
    for h in range(MLA_HEADS):
        sl = slice(h * MLA_HEAD_PAD, (h + 1) * MLA_HEAD_PAD)
        q_o[0, :, sl] = (rope(q[:, sl]) * scale).astype(BF16)
    kr = rope(kr)
    ckv_b = c_kv.astype(BF16)
    k_in = jnp.concatenate([ckv_b, kr.astype(BF16)], axis=1)
    k_o[0] = _dot(k_in, wk_ref[...]).astype(BF16)
    v_o[0] = _dot_nt(wv_ref[...], ckv_b).astype(BF16)


def _mla_prep(z_mla, p, tabs):
    b, s, _ = z_mla.shape
    tr = 512
    tile = lambda w: pl.BlockSpec((1, tr, w), lambda bi, ti: (bi, ti, 0))
    full = lambda a: pl.BlockSpec(a.shape, lambda bi, ti: (0,) * a.ndim)
    tab = pl.BlockSpec((tr, LANES), lambda bi, ti: (ti, 0))
    consts = [p["q_norm"], p["kv_norm"], p["wq"], p["wk"], p["wv"]]
    hq = MLA_HEADS * MLA_HEAD_PAD
    return pl.pallas_call(
        _mla_prep_kernel,
        name="mla_prep",
        grid=(b, s // tr),
        in_specs=[tile(MLA_IN_PAD)] + [full(a) for a in consts] + [tab] * 2,
        out_specs=[tile(hq), tile(hq),
                   pl.BlockSpec((1, MLA_WIDTH, tr), lambda bi, ti: (bi, 0, ti))],
        out_shape=[jax.ShapeDtypeStruct((b, s, hq), BF16), jax.ShapeDtypeStruct((b, s, hq), BF16),
                   jax.ShapeDtypeStruct((b, MLA_WIDTH, s), BF16)],
        compiler_params=_cparams(("parallel", "parallel")),
    )(z_mla, *consts, *tabs["mla"])


def _softmax_pv_t(s_t, v_t):
    pr = jnp.exp2(s_t - jnp.max(s_t, 0, keepdims=True))
    l = jnp.sum(pr, 0, keepdims=True)
    return _dot(v_t, pr.astype(BF16)) * (1.0 / l)


def _mla_attn_kernel(q_ref, k_ref, v_ref, o_ref):
    nh = MLA_HEADS_PER_STEP
    sl = lambda h: slice(h * MLA_HEAD_PAD, (h + 1) * MLA_HEAD_PAD)
    scores = [_dot_nt(k_ref[0, :, sl(h)], q_ref[0, :, sl(h)]) for h in range(nh)]
    outs = []
    for h, s_t in enumerate(scores):
        pair = h // 2
        o_t = _softmax_pv_t(s_t, v_ref[0, pair * LANES:(pair + 1) * LANES, :])
        outs.append(o_t[(h % 2) * MLA_V:(h % 2 + 1) * MLA_V])
    o_ref[0] = jnp.concatenate(outs, axis=0).T.astype(o_ref.dtype)


def _mla_attn(q, k, v):
    b, s, _ = q.shape
    tq = 256
    nh = MLA_HEADS_PER_STEP
    return pl.pallas_call(
        _mla_attn_kernel,
        name="mla_attn",
        grid=(b, MLA_HEADS // nh, s // tq),
        in_specs=[pl.BlockSpec((1, tq, nh * MLA_HEAD_PAD), lambda bi, pi, qi: (bi, qi, pi)),
                  pl.BlockSpec((1, s, nh * MLA_HEAD_PAD), lambda bi, pi, qi: (bi, 0, pi)),
                  pl.BlockSpec((1, nh * MLA_V, s), lambda bi, pi, qi: (bi, pi, 0))],
        out_specs=pl.BlockSpec((1, tq, nh * MLA_V), lambda bi, pi, qi: (bi, qi, pi)),
        out_shape=jax.ShapeDtypeStruct((b, s, MLA_WIDTH), BF16),
        compiler_params=_cparams(("parallel", "parallel", "parallel")),
    )(q, k, v)


def _df_prep_kernel(z_ref, c_ref, sg_ref, q_o, k_o, v_o):
    cos, sg = c_ref[...], sg_ref[...]
    scale = DF_HEAD_DIM ** -0.5 * LOG2E
    rope = lambda x: x * cos + pltpu.roll(x, LANES // 2, 1) * sg
    for h in range(DF_HEADS):
        sl = slice(h * LANES, (h + 1) * LANES)
        q_o[0, :, sl] = (rope(z_ref[0, :, sl]) * scale).astype(BF16)
        slk = slice(DF_WIDTH + h * LANES, DF_WIDTH + (h + 1) * LANES)
        k_o[0, :, sl] = rope(z_ref[0, :, slk]).astype(BF16)
    v_o[0] = z_ref[0, :, 2 * DF_WIDTH:3 * DF_WIDTH].T.astype(BF16)


def _df_prep(z_df, tabs):
    b, s, _ = z_df.shape
    tr = 512
    tile = lambda w: pl.BlockSpec((1, tr, w), lambda bi, ti: (bi, ti, 0))
    tab = pl.BlockSpec((tr, LANES), lambda bi, ti: (ti, 0))
    return pl.pallas_call(
        _df_prep_kernel,
        name="df_prep",
        grid=(b, s // tr),
        in_specs=[tile(DF_IN)] + [tab] * 2,
        out_specs=[tile(DF_WIDTH), tile(DF_WIDTH),
                   pl.BlockSpec((1, DF_WIDTH, tr), lambda bi, ti: (bi, 0, ti))],
        out_shape=[jax.ShapeDtypeStruct((b, s, DF_WIDTH), BF16)] * 2
                  + [jax.ShapeDtypeStruct((b, DF_WIDTH, s), BF16)],
        compiler_params=_cparams(("parallel", "parallel")),
    )(z_df, *tabs["df"])


def _df_attn_kernel(q_ref, k_ref, v_ref, lq1, lk1, lq2, lk2, g_ref, o_ref, *, lambda_init):
    lam = (jnp.exp(jnp.sum(lq1[...] * lk1[...], -1, keepdims=True))
           - jnp.exp(jnp.sum(lq2[...] * lk2[...], -1, keepdims=True)) + lambda_init)
    nh = DF_HEADS_PER_STEP
    lane = lax.broadcasted_iota(jnp.int32, (1, LANES), 1)
    comp = lane & (DF_HEAD_DIM // 2)
    masks = (comp == 0, comp != 0)
    sl = lambda h: slice(h * LANES, (h + 1) * LANES)
    scores = [_dot_nt(k_ref[0, :, sl(h)], jnp.where(m, q_ref[0, :, sl(h)], 0.0).astype(BF16))
              for h in range(nh) for m in masks]
    outs = []
    for h in range(nh):
        v_t = v_ref[0, sl(h), :]
        o = _softmax_pv_t(scores[2 * h], v_t) - lam * _softmax_pv_t(scores[2 * h + 1], v_t)
        o = o * lax.rsqrt(jnp.mean(o * o, 0, keepdims=True) + DF_EPS) * g_ref[...]
        outs.append(o * (1.0 - lambda_init))
    o_ref[0] = jnp.concatenate(outs, axis=0).T.astype(o_ref.dtype)


def _df_attn(q, k, v, p, lambda_init):
    b, s, _ = q.shape
    tq = 256
    nh = DF_HEADS_PER_STEP
    full = lambda a: pl.BlockSpec(a.shape, lambda bi, hi, qi: (0,) * a.ndim)
    consts = [p["lq1"], p["lk1"], p["lq2"], p["lk2"], p["subln"].reshape(-1, 1)]
    return pl.pallas_call(
        functools.partial(_df_attn_kernel, lambda_init=lambda_init),
        name="df_attn",
        grid=(b, DF_HEADS // nh, s // tq),
        in_specs=[pl.BlockSpec((1, tq, nh * LANES), lambda bi, hi, qi: (bi, qi, hi)),
                  pl.BlockSpec((1, s, nh * LANES), lambda bi, hi, qi: (bi, 0, hi)),
                  pl.BlockSpec((1, nh * LANES, s), lambda bi, hi, qi: (bi, hi, 0))]
                 + [full(a) for a in consts],
        out_specs=pl.BlockSpec((1, tq, nh * LANES), lambda bi, hi, qi: (bi, qi, hi)),
        out_shape=jax.ShapeDtypeStruct((b, s, DF_WIDTH), BF16),
        compiler_params=_cparams(("parallel", "parallel", "parallel")),
    )(q, k, v, *consts)


def _merge_kernel(orw, omla, odf, g0, g1, g2, w0, w1, w2, o_ref):
    acc = g0[...].astype(F32) * _dot(orw[...], w0[...])
    acc = acc + g1[...].astype(F32) * _dot(omla[...], w1[...])
    acc = acc + g2[...].astype(F32) * _dot(odf[...], w2[...])
    o_ref[...] = acc.astype(o_ref.dtype)


def _merge(o_rw, o_mla, o_df, gates, p):
    t = o_rw.shape[0]
    tm = _pick_tile(t, (1024, 512, 256))
    tn = 1024
    nj = D_MODEL // tn
    act = lambda w: pl.BlockSpec((tm, w), lambda i, j: (i, 0))
    gate = lambda br: pl.BlockSpec((tm, tn), lambda i, j: (i, j + br * nj))
    wt = lambda w: pl.BlockSpec((w, tn), lambda i, j: (0, j))
    return pl.pallas_call(
        _merge_kernel,
        name="merge",
        grid=(t // tm, nj),
        in_specs=[act(RW_WIDTH), act(MLA_WIDTH), act(DF_WIDTH), gate(0), gate(1), gate(2),
                  wt(RW_WIDTH), wt(MLA_WIDTH), wt(DF_WIDTH)],
        out_specs=pl.BlockSpec((tm, tn), lambda i, j: (i, j)),
        out_shape=jax.ShapeDtypeStruct((t, D_MODEL), BF16),
        compiler_params=_cparams(("parallel", "parallel")),
    )(o_rw, o_mla, o_df, gates, gates, gates, p["w_up_rw"], p["w_up_mla"], p["w_up_df"])


def _layer_norm(y, g, b):
    mu = jnp.mean(y, -1, keepdims=True)
    yc = y - mu
    var = jnp.mean(yc * yc, -1, keepdims=True)
    return yc * lax.rsqrt(var + LN_EPS) * g + b


def _wo_ln_kernel(m_ref, w_ref, x_ref, g_ref, b_ref, o_ref, ob_ref):
    y = ALPHA * x_ref[...] + _dot(m_ref[...], w_ref[...])
    out = _layer_norm(y, g_ref[...], b_ref[...])
    o_ref[...] = out
    ob_ref[...] = out.astype(BF16).reshape(ob_ref.shape)


def _wo_ln(merged, x, p):
    t = x.shape[0]
    tm = 256
    tile = pl.BlockSpec((tm, D_MODEL), lambda i: (i, 0))
    tile3 = pl.BlockSpec((tm, ROW_SUB, LANES), lambda i: (i, 0, 0))
    full = lambda a: pl.BlockSpec(a.shape, lambda i: (0,) * a.ndim)
    return pl.pallas_call(
        _wo_ln_kernel,
        name="wo_ln",
        grid=(t // tm,),
        in_specs=[tile, full(p["w_o"]), tile, full(p["ln1_g"]), full(p["ln1_b"])],
        out_specs=[tile, tile3],
        out_shape=[jax.ShapeDtypeStruct((t, D_MODEL), F32),
                   jax.ShapeDtypeStruct((t, ROW_SUB, LANES), BF16)],
        compiler_params=_cparams(("parallel",)),
    )(merged, p["w_o"], x, p["ln1_g"], p["ln1_b"])


ROUTER_TILE = 1024


def _router_kernel(x_ref, wh_ref, wl_ref, bias_ref, tri_ref, idx_o, wts_o, rank_o, cnt_o, cnt_scr):
    @pl.when(pl.program_id(0) == 0)
    def _():
        cnt_scr[...] = jnp.zeros_like(cnt_scr)

    xh, xl = _split_bf16(x_ref[...])
    wh, wl = wh_ref[...], wl_ref[...]
    logits = _dot_nt(wh, xh) + (_dot_nt(wh, xl) + _dot_nt(wl, xh))
    scores = jax.nn.sigmoid(logits)
    sel = scores + bias_ref[...]
    tm = sel.shape[1]

    def row(a, i):
        return a[i:i + 1, :]

    best = jnp.zeros((1, tm), jnp.int32)
    best_s = None
    for g in range(N_GROUPS):
        a, b, c, d = (row(sel, EXPERTS_PER_GROUP * g + j) for j in range(4))
        hi1, lo1 = jnp.maximum(a, b), jnp.minimum(a, b)
        hi2, lo2 = jnp.maximum(c, d), jnp.minimum(c, d)
        gs = jnp.maximum(hi1, hi2) + jnp.maximum(jnp.minimum(hi1, hi2), jnp.maximum(lo1, lo2))
        if g == 0:
            best_s = gs
        else:
            upd = gs > best_s
            best = jnp.where(upd, g, best)
            best_s = jnp.where(upd, gs, best_s)

    def pick(a, j):
        out = row(a, j)
        for g in range(1, N_GROUPS):
            out = jnp.where(best == g, row(a, EXPERTS_PER_GROUP * g + j), out)
        return out

    cand = [pick(sel, j) for j in range(EXPERTS_PER_GROUP)]
    csc = [pick(scores, j) for j in range(EXPERTS_PER_GROUP)]
    neg = jnp.float32(-jnp.inf)

    def argmax4(vals):
        bi, bv = jnp.zeros((1, tm), jnp.int32), vals[0]
        for j in range(1, EXPERTS_PER_GROUP):
            upd = vals[j] > bv
            bi = jnp.where(upd, j, bi)
            bv = jnp.where(upd, vals[j], bv)
        return bi

    i1 = argmax4(cand)
    i2 = argmax4([jnp.where(i1 == j, neg, cand[j]) for j in range(EXPERTS_PER_GROUP)])

    def take(vals, i):
        out = vals[0]
        for j in range(1, EXPERTS_PER_GROUP):
            out = jnp.where(i == j, vals[j], out)
        return out

    w1, w2 = take(csc, i1), take(csc, i2)
    tot = w1 + w2
    e1 = best * EXPERTS_PER_GROUP + i1
    e2 = best * EXPERTS_PER_GROUP + i2
    eid = lax.broadcasted_iota(jnp.int32, (N_EXPERTS, tm), 0)
    oh1 = eid == e1
    oh2 = eid == e2
    oh = (oh1 | oh2).astype(BF16)
    before = _dot(oh, tri_ref[...]) + cnt_scr[...][:, 0:1]
    r1 = jnp.sum(jnp.where(oh1, before, 0.0), 0, keepdims=True)
    r2 = jnp.sum(jnp.where(oh2, before, 0.0), 0, keepdims=True)
    idx_o[0:1, :] = e1
    idx_o[1:2, :] = e2
    wts_o[0:1, :] = w1 / tot
    wts_o[1:2, :] = w2 / tot
    rank_o[0:1, :] = r1.astype(jnp.int32)
    rank_o[1:2, :] = r2.astype(jnp.int32)
    new_cnt = cnt_scr[...] + jnp.sum(oh.astype(F32), 1, keepdims=True)
    cnt_scr[...] = new_cnt
    cnt_o[...] = new_cnt


def _router(x, p):
    t = x.shape[0]
    tm = ROUTER_TILE
    full = lambda a: pl.BlockSpec(a.shape, lambda i: (0,) * a.ndim)
    tok = pl.BlockSpec((TOP_K, tm), lambda i: (0, i))
    consts = [p["router_wt_hi"], p["router_wt_lo"], p["router_bias"], p["router_tri"]]
    return pl.pallas_call(
        _router_kernel,
        name="router",
        grid=(t // tm,),
        in_specs=[pl.BlockSpec((tm, D_MODEL), lambda i: (i, 0))] + [full(a) for a in consts],
        out_specs=[tok, tok, tok, pl.BlockSpec((N_EXPERTS, LANES), lambda i: (0, 0))],
        out_shape=[jax.ShapeDtypeStruct((TOP_K, t), jnp.int32), jax.ShapeDtypeStruct((TOP_K, t), F32),
                   jax.ShapeDtypeStruct((TOP_K, t), jnp.int32),
                   jax.ShapeDtypeStruct((N_EXPERTS, LANES), F32)],
        scratch_shapes=[pltpu.VMEM((N_EXPERTS, LANES), F32)],
        compiler_params=_cparams(("arbitrary",)),
    )(x, *consts)


DISPATCH_TILE = 512


def _dispatch_kernel(dest_ref, x_ref, init_ref, xs_ref, sem):
    del init_ref
    def copy(r, k):
        return pltpu.make_async_copy(x_ref.at[r], xs_ref.at[dest_ref[0, TOP_K * r + k]], sem)

    def start(r, carry):
        copy(r, 0).start()
        copy(r, 1).start()
        return carry

    def wait(r, carry):
        copy(r, 0).wait()
        copy(r, 1).wait()
        return carry

    lax.fori_loop(0, DISPATCH_TILE, start, 0, unroll=DMA_UNROLL)
    lax.fori_loop(0, DISPATCH_TILE, wait, 0, unroll=DMA_UNROLL)


def _dispatch(xb, dest, n_rows):
    t = xb.shape[0]
    nt = t // DISPATCH_TILE
    dest2 = dest.reshape(nt, 1, DISPATCH_TILE * TOP_K)
    init = jnp.zeros((n_rows, ROW_SUB, LANES), BF16)
    return pl.pallas_call(
        _dispatch_kernel,
        name="dispatch",
        grid=(nt,),
        in_specs=[pl.BlockSpec((None, 1, DISPATCH_TILE * TOP_K), lambda i: (i, 0, 0),
                               memory_space=pltpu.SMEM),
                  pl.BlockSpec((DISPATCH_TILE, ROW_SUB, LANES), lambda i: (i, 0, 0)),
                  pl.BlockSpec(memory_space=pl.ANY)],
        out_specs=pl.BlockSpec(memory_space=pl.ANY),
        out_shape=jax.ShapeDtypeStruct((n_rows, ROW_SUB, LANES), BF16),
        scratch_shapes=[pltpu.SemaphoreType.DMA(())],
        input_output_aliases={2: 0},
        compiler_params=_cparams(("arbitrary",)),
    )(dest2, xb, init)


def _ffn_kernel(te_ref, nu_ref, xs_ref, wg_ref, wu_ref, wd_ref, y_ref):
    del te_ref

    @pl.when(pl.program_id(0) < nu_ref[0])
    def _():
        xs = xs_ref[...].reshape(FFN_TILE, D_MODEL)
        gate = _dot(xs, wg_ref[0])
        h = gate * _sigmoid(gate) * _dot(xs, wu_ref[0])
        y_ref[...] = _dot(h.astype(BF16), wd_ref[0]).astype(BF16).reshape(y_ref.shape)

    @pl.when(pl.program_id(0) >= nu_ref[0])
    def _():
        y_ref[...] = jnp.zeros_like(y_ref)


def _ffn(xs, tile_expert, n_used, p):
    n_rows = xs.shape[0]
    tm = FFN_TILE
    grid_spec = pltpu.PrefetchScalarGridSpec(
        num_scalar_prefetch=2,
        grid=(n_rows // tm,),
        in_specs=[pl.BlockSpec((tm, ROW_SUB, LANES), lambda i, te, nu: (i, 0, 0)),
                  pl.BlockSpec((1, D_MODEL, D_EXPERT), lambda i, te, nu: (te[i], 0, 0)),
                  pl.BlockSpec((1, D_MODEL, D_EXPERT), lambda i, te, nu: (te[i], 0, 0)),
                  pl.BlockSpec((1, D_EXPERT, D_MODEL), lambda i, te, nu: (te[i], 0, 0))],
        out_specs=pl.BlockSpec((tm, ROW_SUB, LANES), lambda i, te, nu: (i, 0, 0)),
    )
    return pl.pallas_call(
        _ffn_kernel,
        name="ffn",
        grid_spec=grid_spec,
        out_shape=jax.ShapeDtypeStruct((n_rows, ROW_SUB, LANES), BF16),
        compiler_params=_cparams(("arbitrary",)),
    )(tile_expert, n_used, xs, p["ex_w_gate"], p["ex_w_up"], p["ex_w_down"])


COMBINE_TILE = 256


def _combine_ln_kernel(dest_ref, dest_next_ref, y_ref, x_ref, w_ref, g_ref, b_ref, o_ref, ob_ref,
                       buf, sem):
    i = pl.program_id(0)
    slot = i % 2

    def copy(idx_ref, s, r, k):
        return pltpu.make_async_copy(y_ref.at[idx_ref[0, TOP_K * r + k]], buf.at[s, k, r],
                                     sem.at[s])

    def issue(idx_ref, s):
        def body(r, carry):
            copy(idx_ref, s, r, 0).start()
            copy(idx_ref, s, r, 1).start()
            return carry
        lax.fori_loop(0, COMBINE_TILE, body, 0, unroll=DMA_UNROLL)

    @pl.when(i == 0)
    def _():
        issue(dest_ref, 0)

    @pl.when(i + 1 < pl.num_programs(0))
    def _():
        issue(dest_next_ref, 1 - slot)

    def wait(r, carry):
        copy(dest_ref, slot, r, 0).wait()
        copy(dest_ref, slot, r, 1).wait()
        return carry

    lax.fori_loop(0, COMBINE_TILE, wait, 0, unroll=DMA_UNROLL)
    w = w_ref[...]
    shape = (COMBINE_TILE, D_MODEL)
    ffn = (w[:, 0:1] * buf[slot, 0].reshape(shape).astype(F32)
           + w[:, 1:2] * buf[slot, 1].reshape(shape).astype(F32))
    out = _layer_norm(ALPHA * x_ref[...] + ffn, g_ref[...], b_ref[...])
    o_ref[...] = out
    ob_ref[...] = out.astype(BF16)


def _combine_ln(y, dest, x, wts, p):
    t = x.shape[0]
    tm = COMBINE_TILE
    nt = t // tm
    dest2 = dest.reshape(nt, 1, tm * TOP_K)
    tile = pl.BlockSpec((tm, D_MODEL), lambda i: (i, 0))
    full = lambda a: pl.BlockSpec(a.shape, lambda i: (0,) * a.ndim)
    return pl.pallas_call(
        _combine_ln_kernel,
        name="combine_ln",
        grid=(nt,),
        in_specs=[pl.BlockSpec((None, 1, tm * TOP_K), lambda i: (i, 0, 0), memory_space=pltpu.SMEM),
                  pl.BlockSpec((None, 1, tm * TOP_K), lambda i: (jnp.minimum(i + 1, nt - 1), 0, 0),
                               memory_space=pltpu.SMEM),
                  pl.BlockSpec(memory_space=pl.ANY), tile,
                  pl.BlockSpec((tm, TOP_K), lambda i: (i, 0)), full(p["ln2_g"]), full(p["ln2_b"])],
        out_specs=[tile, tile],
        out_shape=[jax.ShapeDtypeStruct((t, D_MODEL), F32), jax.ShapeDtypeStruct((t, D_MODEL), BF16)],
        scratch_shapes=[pltpu.VMEM((2, TOP_K, tm, ROW_SUB, LANES), BF16),
                        pltpu.SemaphoreType.DMA((2,))],
        compiler_params=_cparams(("arbitrary",)),
    )(dest2, dest2, y, x, wts, p["ln2_g"], p["ln2_b"])


def _moe_ln(x, xb, p):
    t = x.shape[0]
    idx, wts, rank, counts = _router(x, p)
    counts = counts[:, 0].astype(jnp.int32)
    tiles = (counts + FFN_TILE - 1) // FFN_TILE
    tile_end = jnp.cumsum(tiles)
    row_start = (tile_end - tiles) * FFN_TILE
    n_tiles = (t * TOP_K) // FFN_TILE + N_EXPERTS
    start_of = jnp.sum(jnp.where(idx[..., None] == jnp.arange(N_EXPERTS), row_start, 0), -1)
    dest = (start_of + rank).T.reshape(-1)
    tile_expert = jnp.minimum(jnp.sum(jnp.arange(n_tiles)[:, None] >= tile_end[None, :], axis=1),
                              N_EXPERTS - 1).astype(jnp.int32)
    n_used = tile_end[-1:].astype(jnp.int32)
    xs = _dispatch(xb, dest, n_tiles * FFN_TILE)
    y = _ffn(xs, tile_expert, n_used, p)
    return _combine_ln(y, dest, x, wts.T, p)


def _block_diag_ones(n, blk):
    i = jnp.arange(n) // blk
    return (i[:, None] == i[None, :]).astype(BF16)


def _prep_layer(l, w):
    row = lambda a: a.reshape(1, -1).astype(F32)
    w_in = w["w_in"][l]
    o1, o2, o3 = RW_IN, RW_IN + MLA_IN, RW_IN + MLA_IN + DF_IN
    p = {
        "w_in_rw": w_in[:, :o1].astype(BF16),
        "w_in_mla": _mla_in_columns(w_in[:, o1:o2]).astype(BF16),
        "w_in_df": _df_columns(w_in[:, o2:o3]).astype(BF16),
        "w_in_gate": w_in[:, o3:].astype(BF16),
        "mu_prev": row(w["shift_prev"][l]), "mu_next": row(w["shift_next"][l]),
        "w0": row(w["rw_w0"][l]), "a0": row(w["rw_a0"][l]),
        "g2": w["rw_g2"][l].astype(BF16),
        "k_k": row(w["rw_k_k"][l]), "k_a": row(w["rw_k_a"][l]), "r_k": row(w["rw_r_k"][l]),
        "lnx_g": row(w["rw_lnx_g"][l]), "lnx_b": row(w["rw_lnx_b"][l]),
        "bd": _block_diag_ones(RW_WIDTH, RW_HEAD_DIM),
        "q_norm": row(w["mla_q_norm"][l]), "kv_norm": row(w["mla_kv_norm"][l]),
        "lq1": row(w["df_lq1"][l]), "lk1": row(w["df_lk1"][l]),
        "lq2": row(w["df_lq2"][l]), "lk2": row(w["df_lk2"][l]),
        "subln": row(w["df_subln"][l]),
        "w_up_rw": w["w_up_rw"][l].astype(BF16), "w_up_mla": w["w_up_mla"][l].astype(BF16),
        "w_up_df": w["w_up_df"][l].astype(BF16), "w_o": w["w_o"][l].astype(BF16),
        "ln1_g": row(w["ln1_g"][l]), "ln1_b": row(w["ln1_b"][l]),
        "ln2_g": row(w["ln2_g"][l]), "ln2_b": row(w["ln2_b"][l]),
        "ex_w_gate": w["ex_w_gate"][l].astype(BF16), "ex_w_up": w["ex_w_up"][l].astype(BF16),
        "ex_w_down": w["ex_w_down"][l].astype(BF16),
    }
    zc = jnp.zeros((DECAY_LORA, RW_WIDTH), F32)
    w2 = w["rw_w2"][l]
    a2 = w["rw_a2"][l]
    p["w2cat"] = jnp.block([[w2[0], zc], [zc, w2[1]]]).astype(BF16)
    p["a2cat"] = jnp.block([[a2[0], zc], [zc, a2[1]]]).astype(BF16)
    wq = w["mla_w_uq"][l].reshape(MLA_Q_LORA, MLA_HEADS, MLA_NOPE + MLA_ROPE)
    p["wq"] = (jnp.zeros((MLA_Q_LORA, MLA_HEADS, MLA_HEAD_PAD), F32)
               .at[:, :, MLA_LANE_OF_DIM].set(wq).reshape(MLA_Q_LORA, -1).astype(BF16))
    wkv = w["mla_w_ukv"][l].reshape(MLA_KV_LORA, MLA_HEADS, MLA_NOPE + MLA_V)
    wk_nope = (jnp.zeros((MLA_KV_LORA, MLA_HEADS, MLA_HEAD_PAD), F32)
               .at[:, :, MLA_LANE_OF_DIM[:MLA_NOPE]].set(wkv[:, :, :MLA_NOPE]))
    place = jnp.zeros((LANES, MLA_HEADS, MLA_HEAD_PAD), F32)
    j = MLA_LANE_OF_DIM[MLA_NOPE:]
    place = place.at[j, :, j].set(1.0)
    p["wk"] = jnp.concatenate([wk_nope, place], axis=0).reshape(MLA_KV_LORA + LANES, -1).astype(BF16)
    p["wv"] = wkv[:, :, MLA_NOPE:].reshape(MLA_KV_LORA, -1).T.astype(BF16)
    return p


def _trunk(x3, layers, shared):
    b, s, d = x3.shape
    t = b * s
    tabs = {
        "mla": _rope_tables_mla(s),
        "df": _rope_tables_df(s),
    }
    x = x3.reshape(t, d)
    xb = x.astype(BF16)
    for l, p in enumerate(layers):
        p = dict(p, **shared)
        lambda_init = 0.8 - 0.6 * math.exp(-0.3 * l)
        z_rw = _matmul(xb, p["w_in_rw"], F32).reshape(b, s, -1)
        z_mla = _matmul(xb, p["w_in_mla"], F32).reshape(b, s, -1)
        z_df = _matmul(xb, p["w_in_df"], F32).reshape(b, s, -1)
        gates = _matmul(xb, p["w_in_gate"], BF16, act="sigmoid")
        r, v, kk, g, bonus, logd, a, kdir = _rw_prep(z_rw, p)
        yf, yb = _rw_scan(r, v, kk, logd, a, kdir)
        o_rw = _rw_post(yf.reshape(t, -1), yb.reshape(t, -1), bonus.reshape(t, -1),
                        g.reshape(t, -1), p)
        q, k, v2 = _mla_prep(z_mla, p, tabs)
        o_mla = _mla_attn(q, k, v2).reshape(t, -1)
        q, k, v2 = _df_prep(z_df, tabs)
        o_df = _df_attn(q, k, v2, p, lambda_init).reshape(t, -1)
        merged = _merge(o_rw, o_mla, o_df, gates, p)
        x, xb = _wo_ln(merged, x, p)
        x, xb = _moe_ln(x, xb, p)
    return x.reshape(b, s, d)


def kernel(x_prompt, x_sample, w_in, shift_prev, shift_next, rw_w0, rw_w2, rw_a0, rw_a2, rw_g2,
           rw_k_k, rw_k_a, rw_r_k, rw_lnx_g, rw_lnx_b, mla_q_norm, mla_kv_norm, mla_w_uq,
           mla_w_ukv, df_lq1, df_lk1, df_lq2, df_lk2, df_subln, w_up_rw, w_up_mla, w_up_df, w_o,
           ln1_g, ln1_b, ln2_g, ln2_b, router_w, router_bias, ex_w_gate, ex_w_up, ex_w_down):
    w = dict(w_in=w_in, shift_prev=shift_prev, shift_next=shift_next, rw_w0=rw_w0, rw_w2=rw_w2,
             rw_a0=rw_a0, rw_a2=rw_a2, rw_g2=rw_g2, rw_k_k=rw_k_k, rw_k_a=rw_k_a, rw_r_k=rw_r_k,
             rw_lnx_g=rw_lnx_g, rw_lnx_b=rw_lnx_b, mla_q_norm=mla_q_norm,
             mla_kv_norm=mla_kv_norm, mla_w_uq=mla_w_uq, mla_w_ukv=mla_w_ukv, df_lq1=df_lq1,
             df_lk1=df_lk1, df_lq2=df_lq2, df_lk2=df_lk2, df_subln=df_subln, w_up_rw=w_up_rw,
             w_up_mla=w_up_mla, w_up_df=w_up_df, w_o=w_o, ln1_g=ln1_g, ln1_b=ln1_b, ln2_g=ln2_g,
             ln2_b=ln2_b, ex_w_gate=ex_w_gate, ex_w_up=ex_w_up, ex_w_down=ex_w_down)
    layers = [_prep_layer(l, w) for l in range(DEPTH)]
    rwt = router_w.T.astype(F32)
    rwt_hi = rwt.astype(BF16)
    ti = jnp.arange(ROUTER_TILE)
    shared = {
        "router_wt_hi": rwt_hi,
        "router_wt_lo": (rwt - rwt_hi.astype(F32)).astype(BF16),
        "router_bias": router_bias.reshape(N_EXPERTS, 1).astype(F32),
        "router_tri": (ti[:, None] < ti[None, :]).astype(BF16),
    }
    return (_trunk(x_prompt, layers, shared), _trunk(x_sample, layers, shared))
```
